```python
import math
import jax
import jax.numpy as jnp
from jax import lax
import numpy as np


D_MODEL = 1024
BATCH = 8
SEQ = 8192
DEPTH = 4

CTX_LEN = 256
GRID_W = 64
N_MIXERS = 3
ROPE_THETA = 10000.0
EPS = 1e-6
Q_BLOCK = 128
F32 = jnp.float32

GDN_HEADS = 8
GDN_DK = D_MODEL // GDN_HEADS
GDN_DV = D_MODEL // GDN_HEADS
GDN_CONV = 5
GDN_CHUNK = 64

DIFF_HEADS = 8
DIFF_HEAD_DIM = D_MODEL // DIFF_HEADS // 2
DIFF_SUBLN_EPS = 1e-5

MLA_HEADS = 8
MLA_Q_LORA = (3 * D_MODEL) // 4
MLA_KV_LORA = D_MODEL // 4
MLA_NOPE = 128
MLA_ROPE = 64
MLA_V = 128
MLA_SCALE = (MLA_NOPE + MLA_ROPE) ** -0.5

MOE_GROUPS = 4
MOE_EXPERTS_PER_GROUP = 8
MOE_EXPERTS = MOE_GROUPS * MOE_EXPERTS_PER_GROUP
MOE_TOP_K = 2
MOE_D_FF = D_MODEL // 2
MOE_BLOCK = 128

kernel_name = 'hybrid_gdn_diffattn_mla_hmoe_dit'


def rmsnorm(x, gain, eps=EPS):
    xf = x.astype(F32)
    y = xf * lax.rsqrt(jnp.mean(xf * xf, axis=-1, keepdims=True) + eps)
    return (y * gain.astype(F32)).astype(x.dtype)


def l2norm(x):
    return x * lax.rsqrt(jnp.sum(x * x, axis=-1, keepdims=True) + EPS)


def axial_rope_tables(rows, rot_dim):
    quarter = rot_dim // 4
    inv_freq = ROPE_THETA ** (-jnp.arange(quarter, dtype=F32) / quarter)
    t = jnp.arange(rows * GRID_W)
    row = (t // GRID_W).astype(F32)
    col = (t % GRID_W).astype(F32)
    ang = jnp.stack([row[:, None] * inv_freq, col[:, None] * inv_freq], axis=1)
    return jnp.cos(ang), jnp.sin(ang)


def apply_axial_rope(x, cos, sin):
    shp = x.shape
    xr = x.reshape(shp[:-1] + (2, 2, shp[-1] // 4))
    x1, x2 = xr[..., 0, :], xr[..., 1, :]
    cos = cos.astype(x.dtype)
    sin = sin.astype(x.dtype)
    return jnp.stack([x1 * cos - x2 * sin, x2 * cos + x1 * sin], axis=-2).reshape(shp)


def to_query_blocks(q):
    b, h, s, d = q.shape
    return q.reshape(b, h, s // Q_BLOCK, Q_BLOCK, d).transpose(2, 0, 1, 3, 4)


def from_query_blocks(o):
    nb, b, h, qb, d = o.shape
    return o.transpose(1, 2, 0, 3, 4).reshape(b, h, nb * qb, d)


def centred_depthwise_conv(x, w):
    k, ch = w.shape
    return lax.conv_general_dilated(x, w.astype(x.dtype)[:, None, :], window_strides=(1,),
                                    padding=[(k // 2, k // 2)],
                                    dimension_numbers=('NWC', 'WIO', 'NWC'),
                                    feature_group_count=ch)


def chunk_gated_delta(q, k, v, g, beta, s0):
    b, h, length, dk = k.shape
    dv = v.shape[-1]
    c = GDN_CHUNK
    n = length // c
    chunk = lambda t: t.reshape(b, h, n, c, t.shape[-1])
    q, k, v = chunk(q), chunk(k), chunk(v)
    gc = jnp.cumsum(g.reshape(b, h, n, c), axis=-1)
    beta = beta.reshape(b, h, n, c, 1)
    incl = jnp.tril(jnp.ones((c, c), dtype=bool))
    strict = jnp.tril(jnp.ones((c, c), dtype=bool), -1)
    decay = jnp.exp(jnp.where(incl, gc[..., :, None] - gc[..., None, :], -jnp.inf))
    kb = k * beta
    lower = jnp.where(strict, jnp.einsum('bhnid,bhnjd->bhnij', kb, k) * decay, 0.0)
    tmat = lower + jnp.eye(c, dtype=F32)
    u = lax.linalg.triangular_solve(tmat, v * beta, left_side=True, lower=True)
    w = lax.linalg.triangular_solve(tmat, kb * jnp.exp(gc)[..., None], left_side=True, lower=True)
    intra = jnp.einsum('bhnid,bhnjd->bhnij', q, k) * decay

    def step(state, xs):
        q_i, k_i, u_i, w_i, a_i, g_i = xs
        v_new = u_i - w_i @ state
        o_i = (q_i * jnp.exp(g_i)[..., None]) @ state + a_i @ v_new
        g_last = g_i[..., -1:]
        state = state * jnp.exp(g_last)[..., None] + jnp.einsum(
            'bhck,bhcv->bhkv', k_i * jnp.exp(g_last - g_i)[..., None], v_new)
        return state, o_i

    xs = tuple(jnp.moveaxis(t, 2, 0) for t in (q, k, u, w, intra, gc))
    s_fin, o = lax.scan(step, s0, xs)
    return jnp.moveaxis(o, 0, 2).reshape(b, h, length, dv), s_fin


def gdn_mixer(h_ctx, h_lat, w_in, conv_w, a_log, dt_bias, o_gain, w_out):
    wk = GDN_HEADS * GDN_DK
    wv = GDN_HEADS * GDN_DV

    def project(h):
        bsz, length, _ = h.shape
        p = h @ w_in
        qkv = jax.nn.silu(centred_depthwise_conv(p[..., :2 * wk + wv], conv_w))
        z = p[..., 2 * wk + wv:2 * wk + 2 * wv]
        ab = p[..., 2 * wk + 2 * wv:].astype(F32).reshape(bsz, length, 2, 2, GDN_HEADS)
        heads = lambda t: t.reshape(bsz, length, GDN_HEADS, -1).transpose(0, 2, 1, 3).astype(F32)
        q = l2norm(heads(qkv[..., :wk])) * GDN_DK ** -0.5
        k = l2norm(heads(qkv[..., wk:2 * wk]))
        v = heads(qkv[..., 2 * wk:])
        g = -jnp.exp(a_log.astype(F32)) * jax.nn.softplus(ab[:, :, :, 0] + dt_bias.astype(F32))
        beta = jax.nn.sigmoid(ab[:, :, :, 1])
        return q, k, v, z, g.transpose(0, 2, 3, 1), beta.transpose(0, 2, 3, 1)

    qc, kc, vc, zc, gc, bc = project(h_ctx)
    ql, kl, vl, zl, gl, bl = project(h_lat)
    s0 = jnp.zeros(qc.shape[:2] + (GDN_DK, GDN_DV), F32)
    flip = lambda t: jnp.flip(t, axis=2)
    oc_f, sc_f = chunk_gated_delta(qc, kc, vc, gc[:, 0], bc[:, 0], s0)
    ol_f, _ = chunk_gated_delta(ql, kl, vl, gl[:, 0], bl[:, 0], sc_f)
    oc_b, sc_b = chunk_gated_delta(flip(qc), flip(kc), flip(vc), flip(gc[:, 1]), flip(bc[:, 1]), s0)
    ol_b, _ = chunk_gated_delta(flip(ql), flip(kl), flip(vl), flip(gl[:, 1]), flip(bl[:, 1]), sc_b)

    def finish(o, z):
        bsz, _, length, _ = o.shape
        o = rmsnorm(o, o_gain).transpose(0, 2, 1, 3).reshape(bsz, length, wv)
        return (o.astype(z.dtype) * jax.nn.silu(z)) @ w_out

    return finish(oc_f + flip(oc_b), zc), finish(ol_f + flip(ol_b), zl)


def diff_core(q, k, v, lam):
    b, h2, lq, dh = q.shape
    s = jnp.einsum('bhqd,bhkd->bhqk', q, k).astype(F32) * dh ** -0.5
    p = jax.nn.softmax(s, axis=-1).reshape(b, h2 // 2, 2, lq, -1)
    a = (p[:, :, 0] - lam * p[:, :, 1]).astype(v.dtype)
    return jnp.einsum('bhqk,bhkd->bhqd', a, v)


def diff_mixer(h_ctx, h_lat, cos, sin, w_qkv, lam_vec, sub_gain, w_out, lam_init):
    def project(h):
        bsz, length, _ = h.shape
        q, k, v = jnp.split(h @ w_qkv, 3, axis=-1)
        q = q.reshape(bsz, length, 2 * DIFF_HEADS, DIFF_HEAD_DIM).transpose(0, 2, 1, 3)
        k = k.reshape(bsz, length, 2 * DIFF_HEADS, DIFF_HEAD_DIM).transpose(0, 2, 1, 3)
        v = v.reshape(bsz, length, DIFF_HEADS, 2 * DIFF_HEAD_DIM).transpose(0, 2, 1, 3)
        return q, k, v

    qc, kc, vc = project(h_ctx)
    ql, kl, vl = project(h_lat)
    ql = apply_axial_rope(ql, cos, sin)
    kl = apply_axial_rope(kl, cos, sin)
    lv = lam_vec.astype(F32)
    lam = jnp.exp(jnp.sum(lv[0] * lv[1])) - jnp.exp(jnp.sum(lv[2] * lv[3])) + lam_init
    o_ctx = diff_core(qc, kc, vc, lam)
    k_all = jnp.concatenate([kc, kl], axis=2)
    v_all = jnp.concatenate([vc, vl], axis=2)
    o_lat = from_query_blocks(lax.map(lambda qb: diff_core(qb, k_all, v_all, lam), to_query_blocks(ql)))

    def finish(o):
        bsz, _, length, _ = o.shape
        o = rmsnorm(o, sub_gain, DIFF_SUBLN_EPS) * (1.0 - lam_init)
        return o.transpose(0, 2, 1, 3).reshape(bsz, length, -1) @ w_out

    return finish(o_ctx), finish(o_lat)


def mla_core(qn, qr, kn, kr, v):
    s = (jnp.einsum('bhqd,bhkd->bhqk', qn, kn) + jnp.einsum('bhqr,bkr->bhqk', qr, kr)).astype(F32) * MLA_SCALE
    p = jax.nn.softmax(s, axis=-1).astype(v.dtype)
    return jnp.einsum('bhqk,bhkd->bhqd', p, v)


def mla_mixer(h_ctx, h_lat, cos, sin, w_down, q_gain, kv_gain, w_uq, w_ukv, w_out):
    def project(h, rotate):
        bsz, length, _ = h.shape
        p = h @ w_down
        cq = rmsnorm(p[..., :MLA_Q_LORA], q_gain)
        ckv = rmsnorm(p[..., MLA_Q_LORA:MLA_Q_LORA + MLA_KV_LORA], kv_gain)
        k_rope = p[..., MLA_Q_LORA + MLA_KV_LORA:]
        q = (cq @ w_uq).reshape(bsz, length, MLA_HEADS, MLA_NOPE + MLA_ROPE).transpose(0, 2, 1, 3)
        kv = (ckv @ w_ukv).reshape(bsz, length, MLA_HEADS, MLA_NOPE + MLA_V).transpose(0, 2, 1, 3)
        q_nope, q_rope = q[..., :MLA_NOPE], q[..., MLA_NOPE:]
        k_nope, v = kv[..., :MLA_NOPE], kv[..., MLA_NOPE:]
        if rotate:
            q_rope = apply_axial_rope(q_rope, cos, sin)
            k_rope = apply_axial_rope(k_rope, cos, sin)
        return q_nope, q_rope, k_nope, k_rope, v

    qnc, qrc, knc, krc, vc = project(h_ctx, False)
    qnl, qrl, knl, krl, vl = project(h_lat, True)
    o_ctx = mla_core(qnc, qrc, knc, krc, vc)
    kn_all = jnp.concatenate([knc, knl], axis=2)
    kr_all = jnp.concatenate([krc, krl], axis=1)
    v_all = jnp.concatenate([vc, vl], axis=2)
    o_lat = from_query_blocks(lax.map(lambda qs: mla_core(qs[0], qs[1], kn_all, kr_all, v_all),
                                      (to_query_blocks(qnl), to_query_blocks(qrl))))

    def finish(o):
        bsz, _, length, _ = o.shape
        return o.transpose(0, 2, 1, 3).reshape(bsz, length, -1) @ w_out

    return finish(o_ctx), finish(o_lat)


def grouped_expert_mlp(h, expert_id, gate, w1, w3, w2):
    t, d = h.shape
    kk = expert_id.shape[1]
    e = w1.shape[0]
    a = t * kk
    flat_e = expert_id.reshape(a)
    order = jnp.argsort(flat_e)
    e_sorted = flat_e[order]
    tok_sorted = (order // kk).astype(jnp.int32)
    counts = jnp.bincount(flat_e, length=e)
    padded = (counts + MOE_BLOCK - 1) // MOE_BLOCK * MOE_BLOCK
    pad_end = jnp.cumsum(padded)
    pad_start = pad_end - padded
    start = jnp.cumsum(counts) - counts
    dest = pad_start[e_sorted] + jnp.arange(a) - start[e_sorted]
    n_blocks = -(-(a + e * (MOE_BLOCK - 1)) // MOE_BLOCK)
    slot_tok = jnp.full((n_blocks * MOE_BLOCK,), t, jnp.int32).at[dest].set(tok_sorted)
    h_pad = jnp.concatenate([h, jnp.zeros((1, d), h.dtype)], axis=0)
    xb = h_pad[slot_tok].reshape(n_blocks, MOE_BLOCK, d)
    blk_expert = jnp.minimum(jnp.searchsorted(pad_end, jnp.arange(n_blocks) * MOE_BLOCK, side='right'), e - 1)

    def expert_block(args):
        xi, ei = args
        return (jax.nn.silu(xi @ w1[ei]) * (xi @ w3[ei])) @ w2[ei]

    yb = lax.map(expert_block, (xb, blk_expert)).reshape(-1, d)
    y = yb[dest].astype(F32) * gate.reshape(a)[order][:, None]
    out = jnp.zeros((t, d), F32).at[tok_sorted].add(y)
    return out.astype(h.dtype)


def hier_moe(h, w_group, b_group, w_expert, b_expert, w1, w3, w2):
    t, _ = h.shape
    hf = h.astype(F32)
    g_logits = hf @ w_group.astype(F32) + b_group.astype(F32)
    g_prob = jax.nn.softmax(g_logits, axis=-1)
    g_sel = jnp.argmax(g_logits, axis=-1)
    p_group = jnp.take_along_axis(g_prob, g_sel[:, None], axis=-1)
    e_logits = (hf @ w_expert.astype(F32) + b_expert.astype(F32)).reshape(t, MOE_GROUPS, MOE_EXPERTS_PER_GROUP)
    e_in = jnp.take_along_axis(e_logits, g_sel[:, None, None], axis=1)[:, 0]
    top_v, top_i = lax.top_k(e_in, MOE_TOP_K)
    gate = p_group * jax.nn.softmax(top_v, axis=-1)
    expert_id = g_sel[:, None].astype(jnp.int32) * MOE_EXPERTS_PER_GROUP + top_i.astype(jnp.int32)
    return grouped_expert_mlp(h, expert_id, gate, w1, w3, w2)


def setup_inputs(seed: int = 0) -> dict:
    key = jax.random.key(seed)
    keys = iter(jax.random.split(key, 48))

    def normal(shape, scale):
        return jax.random.normal(next(keys), shape, F32) * scale

    def gain(shape):
        return 1.0 + normal(shape, 0.01)

    d = D_MODEL
    n_a = len(range(0, DEPTH, N_MIXERS))
    n_b = len(range(1, DEPTH, N_MIXERS))
    n_c = len(range(2, DEPTH, N_MIXERS))
    wk = GDN_HEADS * GDN_DK
    wv = GDN_HEADS * GDN_DV
    gdn_cols = 2 * wk + 2 * wv + 4 * GDN_HEADS
    a_log = jnp.log(jax.random.uniform(next(keys), (n_a, 2, GDN_HEADS), F32, minval=1.0, maxval=16.0))
    dt = jnp.exp(jax.random.uniform(next(keys), (n_a, 2, GDN_HEADS), F32,
                                    minval=math.log(1e-3), maxval=math.log(1e-1)))
    dt_bias = dt + jnp.log(-jnp.expm1(-dt))
    return {
        'x': normal((BATCH, SEQ, d), 1.0),
        'c': normal((BATCH, d), 1.0),
        'ctx': normal((BATCH, CTX_LEN, d), 1.0),
        'c_ctx': normal((d,), 1.0),
        'w_mod': normal((DEPTH, d, 6 * d), 0.5 * d ** -0.5),
        'b_mod': normal((DEPTH, 6 * d), 0.02),
        'norm_mix': gain((DEPTH, d)),
        'norm_ffn': gain((DEPTH, d)),
        'gdn_w_in': normal((n_a, d, gdn_cols), d ** -0.5),
        'gdn_conv': normal((n_a, GDN_CONV, 2 * wk + wv), GDN_CONV ** -0.5),
        'gdn_a_log': a_log,
        'gdn_dt_bias': dt_bias,
        'gdn_norm': gain((n_a, GDN_DV)),
        'gdn_w_out': normal((n_a, wv, d), wv ** -0.5),
        'diff_w_qkv': normal((n_b, d, 3 * d), d ** -0.5),
        'diff_lambda': normal((n_b, 4, DIFF_HEAD_DIM), 0.1),
        'diff_norm': gain((n_b, 2 * DIFF_HEAD_DIM)),
        'diff_w_out': normal((n_b, d, d), d ** -0.5),
        'mla_w_down': normal((n_c, d, MLA_Q_LORA + MLA_KV_LORA + MLA_ROPE), d ** -0.5),
        'mla_q_norm': gain((n_c, MLA_Q_LORA)),
        'mla_kv_norm': gain((n_c, MLA_KV_LORA)),
        'mla_w_uq': normal((n_c, MLA_Q_LORA, MLA_HEADS * (MLA_NOPE + MLA_ROPE)), MLA_Q_LORA ** -0.5),
        'mla_w_ukv': normal((n_c, MLA_KV_LORA, MLA_HEADS * (MLA_NOPE + MLA_V)), MLA_KV_LORA ** -0.5),
        'mla_w_out': normal((n_c, MLA_HEADS * MLA_V, d), (MLA_HEADS * MLA_V) ** -0.5),
        'moe_w_group': normal((DEPTH, d, MOE_GROUPS), d ** -0.5),
        'moe_b_group': normal((DEPTH, MOE_GROUPS), 0.01),
        'moe_w_expert': normal((DEPTH, d, MOE_EXPERTS), d ** -0.5),
        'moe_b_expert': normal((DEPTH, MOE_EXPERTS), 0.01),
        'moe_w1': normal((DEPTH, MOE_EXPERTS, d, MOE_D_FF), d ** -0.5),
        'moe_w3': normal((DEPTH, MOE_EXPERTS, d, MOE_D_FF), d ** -0.5),
        'moe_w2': normal((DEPTH, MOE_EXPERTS, MOE_D_FF, d), MOE_D_FF ** -0.5),
        'final_norm': gain((d,)),
    }


def reference(x, c, ctx, c_ctx, w_mod, b_mod, norm_mix, norm_ffn, gdn_w_in, gdn_conv, gdn_a_log,
              gdn_dt_bias, gdn_norm, gdn_w_out, diff_w_qkv, diff_lambda, diff_norm, diff_w_out,
              mla_w_down, mla_q_norm, mla_kv_norm, mla_w_uq, mla_w_ukv, mla_w_out, moe_w_group,
              moe_b_group, moe_w_expert, moe_b_expert, moe_w1, moe_w3, moe_w2, final_norm):
    bsz, seq, d = x.shape
    ctx_len = ctx.shape[1]
    rows = seq // GRID_W
    cos_d, sin_d = axial_rope_tables(rows, DIFF_HEAD_DIM)
    cos_m, sin_m = axial_rope_tables(rows, MLA_ROPE)
    silu_c = jax.nn.silu(c)
    silu_cc = jax.nn.silu(c_ctx)
    for i in range(DEPTH):
        kind, j = i % N_MIXERS, i // N_MIXERS
        last = i == DEPTH - 1
        mod = jnp.split((silu_c @ w_mod[i] + b_mod[i])[:, None, :], 6, axis=-1)
        mod_c = jnp.split(silu_cc @ w_mod[i] + b_mod[i], 6, axis=-1)
        h_lat = rmsnorm(x, norm_mix[i]) * (1.0 + mod[1]) + mod[0]
        h_ctx = rmsnorm(ctx, norm_mix[i]) * (1.0 + mod_c[1]) + mod_c[0]
        if kind == 0:
            o_ctx, o_lat = gdn_mixer(h_ctx, h_lat, gdn_w_in[j], gdn_conv[j], gdn_a_log[j],
                                     gdn_dt_bias[j], gdn_norm[j], gdn_w_out[j])
        elif kind == 1:
            lam_init = 0.8 - 0.6 * math.exp(-0.3 * i)
            o_ctx, o_lat = diff_mixer(h_ctx, h_lat, cos_d, sin_d, diff_w_qkv[j], diff_lambda[j],
                                      diff_norm[j], diff_w_out[j], lam_init)
        else:
            o_ctx, o_lat = mla_mixer(h_ctx, h_lat, cos_m, sin_m, mla_w_down[j], mla_q_norm[j],
                                     mla_kv_norm[j], mla_w_uq[j], mla_w_ukv[j], mla_w_out[j])
        x = x + mod[2] * o_lat
        h_lat = rmsnorm(x, norm_ffn[i]) * (1.0 + mod[4]) + mod[3]
        if last:
            y_lat = hier_moe(h_lat.reshape(-1, d), moe_w_group[i], moe_b_group[i], moe_w_expert[i],
                             moe_b_expert[i], moe_w1[i], moe_w3[i], moe_w2[i]).reshape(bsz, seq, d)
            x = x + mod[5] * y_lat
        else:
            ctx = ctx + mod_c[2] * o_ctx
            h_ctx = rmsnorm(ctx, norm_ffn[i]) * (1.0 + mod_c[4]) + mod_c[3]
            h_all = jnp.concatenate([h_ctx, h_lat], axis=1).reshape(-1, d)
            y = hier_moe(h_all, moe_w_group[i], moe_b_group[i], moe_w_expert[i], moe_b_expert[i],
                         moe_w1[i], moe_w3[i], moe_w2[i]).reshape(bsz, ctx_len + seq, d)
            ctx = ctx + mod_c[5] * y[:, :ctx_len]
            x = x + mod[5] * y[:, ctx_len:]
    return rmsnorm(x, final_norm)
```

```python
import functools
import math

import jax
import jax.numpy as jnp
from jax import lax
from jax.experimental import pallas as pl
from jax.experimental.pallas import tpu as pltpu

F32 = jnp.float32
BF16 = jnp.bfloat16
I32 = jnp.int32
HIGHEST = lax.Precision.HIGHEST

LANES = 128
SUBLANES = 8
VMEM_LIMIT = 56 * 1024 * 1024

EPS = 1e-6
GRID_W = 64
ROPE_THETA = 10000.0
N_MIXERS = 3
HEADS = 8
GDN_CONV = 5
GDN_CHUNK = 64
DIFF_SUBLN_EPS = 1e-5
MLA_NOPE = 128
MLA_ROPE = 64
MLA_V = 128
MOE_GROUPS = 4
MOE_PER_GROUP = 8
MOE_EXPERTS = MOE_GROUPS * MOE_PER_GROUP
MOE_TOP_K = 2
MOE_BLOCK = 256
NEG = -1e30


def _cparams(*sem):
    return pltpu.CompilerParams(dimension_semantics=sem, vmem_limit_bytes=VMEM_LIMIT)


def _row_tile(ctx_len):
    return 256 if ctx_len % 256 == 0 else 128


def _mod_row_map(nt, nct, bsz):
    def f(j):
        return jnp.where(j % nt < nct, bsz, j // nt)
    return f


def _norm_mod(x, gain, shift, scale, eps=EPS):
    var = jnp.mean(x * x, axis=-1, keepdims=True)
    y = x * lax.rsqrt(var + eps) * gain
    return y * (1.0 + scale) + shift


def _silu(x):
    return x * jax.nn.sigmoid(x)


def _mod_kernel(c_ref, w_ref, b_ref, o_ref):
    s = _silu(c_ref[...])
    o_ref[...] = jnp.dot(s, w_ref[...], precision=HIGHEST, preferred_element_type=F32) + b_ref[...]


def _mod_vectors(c, c_ctx, w_mod, b_mod):
    depth, d, n = w_mod.shape
    bsz = c.shape[0]
    rows = -(-(bsz + 1) // SUBLANES) * SUBLANES
    cc = jnp.zeros((rows, d), F32).at[:bsz].set(c).at[bsz].set(c_ctx)
    tn = 512
    out = pl.pallas_call(
        _mod_kernel,
        grid=(depth, n // tn),
        in_specs=[pl.BlockSpec((rows, d), lambda i, j: (0, 0)),
                  pl.BlockSpec((None, d, tn), lambda i, j: (i, 0, j)),
                  pl.BlockSpec((None, 1, tn), lambda i, j: (i, 0, j))],
        out_specs=pl.BlockSpec((None, rows, tn), lambda i, j: (i, 0, j)),
        out_shape=jax.ShapeDtypeStruct((depth, rows, n), F32),
        compiler_params=_cparams("parallel", "parallel"),
        name="mod_vectors",
    )(cc, w_mod, b_mod.reshape(depth, 1, n))
    return out.reshape(depth, rows, 6, d)


def _rope_tables(seq, ctx_len):
    quarter = 16
    inv_freq = ROPE_THETA ** (-jnp.arange(quarter, dtype=F32) / quarter)
    t = jnp.arange(seq)
    row = (t // GRID_W).astype(F32)[:, None] * inv_freq
    col = (t % GRID_W).astype(F32)[:, None] * inv_freq
    cos = jnp.concatenate([jnp.cos(row), jnp.cos(row), jnp.cos(col), jnp.cos(col)], axis=1)
    sin = jnp.concatenate([-jnp.sin(row), jnp.sin(row), -jnp.sin(col), jnp.sin(col)], axis=1)
    cos = jnp.concatenate([jnp.ones((ctx_len, 64), F32), cos], axis=0)
    sin = jnp.concatenate([jnp.zeros((ctx_len, 64), F32), sin], axis=0)
    return jnp.tile(cos, (1, 2)), jnp.tile(sin, (1, 2))


def _rope128(blk, cos, sin):
    lane = lax.broadcasted_iota(I32, blk.shape, 1)
    first = (lane % 32) < 16
    partner = jnp.where(first, pltpu.roll(blk, LANES - 16, 1), pltpu.roll(blk, 16, 1))
    return blk * cos + partner * sin


def _diff_proj_kernel(x_ref, mod_ref, g_ref, w_ref, cos_ref, sin_ref, o_ref, *, d, q_scale):
    h = _norm_mod(x_ref[...], g_ref[...], mod_ref[0:1, :], mod_ref[1:2, :])
    p = jnp.dot(h.astype(BF16), w_ref[...], preferred_element_type=F32)
    cos = cos_ref[...]
    sin = sin_ref[...]
    nqk = 2 * d // LANES
    for cb in range(nqk):
        r = _rope128(p[:, cb * LANES:(cb + 1) * LANES], cos, sin)
        if cb < nqk // 2:
            r = r * q_scale
        o_ref[:, cb * LANES:(cb + 1) * LANES] = r.astype(BF16)
    o_ref[:, 2 * d:] = p[:, 2 * d:].astype(BF16)


def _flash(qs, k_ref, v_ref, nk, tk):
    tq = qs[0].shape[0]
    dv = v_ref.shape[-1]

    def step(i, carry):
        off = pl.multiple_of(i * tk, tk)
        kc = k_ref[pl.ds(off, tk), :]
        vc = v_ref[pl.ds(off, tk), :]
        out = []
        for q, (m, l, acc) in zip(qs, carry):
            s = lax.dot_general(q, kc, (((1,), (1,)), ((), ())), preferred_element_type=F32)
            m_new = jnp.maximum(m, jnp.max(s, axis=-1, keepdims=True))
            alpha = jnp.exp(m - m_new)
            p = jnp.exp(s - m_new)
            l = l * alpha + jnp.sum(p, axis=-1, keepdims=True)
            acc = acc * alpha + jnp.dot(p.astype(BF16), vc, preferred_element_type=F32)
            out.append((m_new, l, acc))
        return tuple(out)

    init = tuple((jnp.full((tq, 1), NEG, F32), jnp.zeros((tq, 1), F32), jnp.zeros((tq, dv), F32))
                 for _ in qs)
    res = lax.fori_loop(0, nk, step, init)
    return [(l, acc) for (_, l, acc) in res]


def _diff_attn_kernel(lam_ref, gain_ref, q_ref, k_ref, v_ref, o_ref, *, nct, tk, nk_ctx, nk_all,
                      lam_init):
    i = pl.program_id(2)
    nk = jnp.where(i < nct, nk_ctx, nk_all)
    q = q_ref[...]
    lane = lax.broadcasted_iota(I32, q.shape, 1)
    half = q.shape[1] // 2
    zero = jnp.zeros_like(q)
    (l0, a0), (l1, a1) = _flash([jnp.where(lane < half, q, zero), jnp.where(lane >= half, q, zero)],
                                k_ref, v_ref, nk, tk)
    lv = lam_ref[...]
    lam = (jnp.exp(jnp.sum(lv[0:1] * lv[1:2], keepdims=True))
           - jnp.exp(jnp.sum(lv[2:3] * lv[3:4], keepdims=True)) + lam_init)
    o = a0 / l0 - lam * (a1 / l1)
    var = jnp.mean(o * o, axis=-1, keepdims=True)
    o = o * lax.rsqrt(var + DIFF_SUBLN_EPS) * gain_ref[...] * (1.0 - lam_init)
    o_ref[...] = o.astype(BF16)


def _diff_mixer(xall, mod_i, norm_gain, w_qkv, lam_vec, sub_gain, dims, tables, lam_init):
    bsz, ctx_len, seq, d, tm = dims
    ltot = ctx_len + seq
    nt, nct = ltot // tm, ctx_len // tm
    n_tiles = bsz * nt
    dh = d // HEADS // 2
    cos, sin = tables
    mrow = _mod_row_map(nt, nct, bsz)
    qkv = pl.pallas_call(
        functools.partial(_diff_proj_kernel, d=d, q_scale=dh ** -0.5),
        grid=(n_tiles,),
        in_specs=[pl.BlockSpec((tm, d), lambda j: (j, 0)),
                  pl.BlockSpec((None, 6, d), lambda j: (mrow(j), 0, 0)),
                  pl.BlockSpec((1, d), lambda j: (0, 0)),
                  pl.BlockSpec((d, 3 * d), lambda j: (0, 0)),
                  pl.BlockSpec((tm, LANES), lambda j: (j % nt, 0)),
                  pl.BlockSpec((tm, LANES), lambda j: (j % nt, 0))],
        out_specs=pl.BlockSpec((tm, 3 * d), lambda j: (j, 0)),
        out_shape=jax.ShapeDtypeStruct((n_tiles * tm, 3 * d), BF16),
        compiler_params=_cparams("parallel"),
        name="diff_proj",
    )(xall, mod_i, norm_gain.reshape(1, d), w_qkv.astype(BF16), cos, sin)
    tk = tm
    hw = 2 * dh
    o = pl.pallas_call(
        functools.partial(_diff_attn_kernel, nct=nct, tk=tk, nk_ctx=ctx_len // tk, nk_all=ltot // tk,
                          lam_init=lam_init),
        grid=(bsz, HEADS, nt),
        in_specs=[pl.BlockSpec((4, dh), lambda b, h, i: (0, 0)),
                  pl.BlockSpec((1, hw), lambda b, h, i: (0, 0)),
                  pl.BlockSpec((tm, hw), lambda b, h, i: (b * nt + i, h)),
                  pl.BlockSpec((ltot, hw), lambda b, h, i: (b, HEADS + h)),
                  pl.BlockSpec((ltot, hw), lambda b, h, i: (b, 2 * HEADS + h))],
        out_specs=pl.BlockSpec((tm, hw), lambda b, h, i: (b * nt + i, h)),
        out_shape=jax.ShapeDtypeStruct((n_tiles * tm, d), BF16),
        compiler_params=_cparams("parallel", "parallel", "parallel"),
        name="diff_attn",
    )(lam_vec, sub_gain.reshape(1, hw), qkv, qkv, qkv)
    return (o,)


def _mla_proj_kernel(x_ref, mod_ref, g_ref, wd_ref, qg_ref, kvg_ref, wq_ref, wkv_ref, cos_ref, sin_ref,
                     q_ref, k_ref, v_ref, *, q_lora, kv_lora, scale):
    h = _norm_mod(x_ref[...], g_ref[...], mod_ref[0:1, :], mod_ref[1:2, :])
    p = jnp.dot(h.astype(BF16), wd_ref[...], preferred_element_type=F32)
    cq = p[:, :q_lora]
    cq = cq * lax.rsqrt(jnp.mean(cq * cq, axis=-1, keepdims=True) + EPS) * qg_ref[...]
    ckv = p[:, q_lora:q_lora + kv_lora]
    ckv = ckv * lax.rsqrt(jnp.mean(ckv * ckv, axis=-1, keepdims=True) + EPS) * kvg_ref[...]
    cos = cos_ref[...]
    sin = sin_ref[...]
    kr = _rope128(p[:, q_lora + kv_lora:], cos, sin).astype(BF16)
    q = jnp.dot(cq.astype(BF16), wq_ref[...], preferred_element_type=F32)
    kv = jnp.dot(ckv.astype(BF16), wkv_ref[...], preferred_element_type=F32)
    hq = MLA_NOPE + LANES
    for hh in range(HEADS):
        q_ref[:, hh * hq:hh * hq + MLA_NOPE] = (q[:, hh * hq:hh * hq + MLA_NOPE] * scale).astype(BF16)
        qr = _rope128(q[:, hh * hq + MLA_NOPE:(hh + 1) * hq], cos, sin) * scale
        q_ref[:, hh * hq + MLA_NOPE:(hh + 1) * hq] = qr.astype(BF16)
        k_ref[:, hh * hq:hh * hq + MLA_NOPE] = kv[:, hh * MLA_NOPE:(hh + 1) * MLA_NOPE].astype(BF16)
        k_ref[:, hh * hq + MLA_NOPE:(hh + 1) * hq] = kr
    v_ref[...] = kv[:, HEADS * MLA_NOPE:].astype(BF16)


def _mla_attn_kernel(q_ref, k_ref, v_ref, o_ref, *, nct, tk, nk_ctx, nk_all):
    i = pl.program_id(2)
    nk = jnp.where(i < nct, nk_ctx, nk_all)
    ((l, acc),) = _flash([q_ref[...]], k_ref, v_ref, nk, tk)
    o_ref[...] = (acc / l).astype(BF16)


def _mla_mixer(xall, mod_i, norm_gain, w_down, q_gain, kv_gain, w_uq, w_ukv, dims, tables):
    bsz, ctx_len, seq, d, tm = dims
    ltot = ctx_len + seq
    nt, nct = ltot // tm, ctx_len // tm
    n_tiles = bsz * nt
    rows = n_tiles * tm
    q_lora, kv_lora = q_gain.shape[0], kv_gain.shape[0]
    cos, sin = tables
    mrow = _mod_row_map(nt, nct, bsz)
    hq = MLA_NOPE + LANES
    wd = jnp.pad(w_down, ((0, 0), (0, LANES - MLA_ROPE))).astype(BF16)
    nd = wd.shape[1]
    wq = jnp.pad(w_uq.reshape(q_lora, HEADS, MLA_NOPE + MLA_ROPE),
                 ((0, 0), (0, 0), (0, LANES - MLA_ROPE))).reshape(q_lora, HEADS * hq).astype(BF16)
    wkv = w_ukv.reshape(kv_lora, HEADS, MLA_NOPE + MLA_V)
    wkv = jnp.concatenate([wkv[:, :, :MLA_NOPE].reshape(kv_lora, HEADS * MLA_NOPE),
                           wkv[:, :, MLA_NOPE:].reshape(kv_lora, HEADS * MLA_V)], axis=1).astype(BF16)
    scale = (MLA_NOPE + MLA_ROPE) ** -0.5
    const = lambda j: (0, 0)
    q, k, v = pl.pallas_call(
        functools.partial(_mla_proj_kernel, q_lora=q_lora, kv_lora=kv_lora, scale=scale),
        grid=(n_tiles,),
        in_specs=[pl.BlockSpec((tm, d), lambda j: (j, 0)),
                  pl.BlockSpec((None, 6, d), lambda j: (mrow(j), 0, 0)),
                  pl.BlockSpec((1, d), const),
                  pl.BlockSpec((d, nd), const),
                  pl.BlockSpec((1, q_lora), const),
                  pl.BlockSpec((1, kv_lora), const),
                  pl.BlockSpec((q_lora, HEADS * hq), const),
                  pl.BlockSpec((kv_lora, HEADS * (MLA_NOPE + MLA_V)), const),
                  pl.BlockSpec((tm, LANES), lambda j: (j % nt, 0)),
                  pl.BlockSpec((tm, LANES), lambda j: (j % nt, 0))],
        out_specs=[pl.BlockSpec((tm, HEADS * hq), lambda j: (j, 0)),
                   pl.BlockSpec((tm, HEADS * hq), lambda j: (j, 0)),
                   pl.BlockSpec((tm, HEADS * MLA_V), lambda j: (j, 0))],
        out_shape=[jax.ShapeDtypeStruct((rows, HEADS * hq), BF16),
                   jax.ShapeDtypeStruct((rows, HEADS * hq), BF16),
                   jax.ShapeDtypeStruct((rows, HEADS * MLA_V), BF16)],
        compiler_params=_cparams("parallel"),
        name="mla_proj",
    )(xall, mod_i, norm_gain.reshape(1, d), wd, q_gain.reshape(1, q_lora), kv_gain.reshape(1, kv_lora),
      wq, wkv, cos, sin)
    tk = tm
    o = pl.pallas_call(
        functools.partial(_mla_attn_kernel, nct=nct, tk=tk, nk_ctx=ctx_len // tk, nk_all=ltot // tk),
        grid=(bsz, HEADS, nt),
        in_specs=[pl.BlockSpec((tm, hq), lambda b, h, i: (b * nt + i, h)),
                  pl.BlockSpec((ltot, hq), lambda b, h, i: (b, h)),
                  pl.BlockSpec((ltot, MLA_V), lambda b, h, i: (b, h))],
        out_specs=pl.BlockSpec((tm, MLA_V), lambda b, h, i: (b * nt + i, h)),
        out_shape=jax.ShapeDtypeStruct((rows, HEADS * MLA_V), BF16),
        compiler_params=_cparams("parallel", "parallel", "parallel"),
        name="mla_attn",
    )(q, k, v)
    return (o,)


def _gdn_proj_kernel(xp_ref, x_ref, xn_ref, mod_ref, g_ref, w_ref, cw_ref, alog_ref, dtb_ref,
                     q_ref, k_ref, v_ref, z_ref, gb_ref, pbuf, *, d, nt, nct, tm, dk):
    j = pl.program_id(0)
    r = j % nt
    first = jnp.logical_or(r == 0, r == nct)
    last = jnp.logical_or(r == nct - 1, r == nt - 1)
    halo = SUBLANES
    xe = jnp.concatenate([xp_ref[...], x_ref[...], xn_ref[...]], axis=0)
    h = _norm_mod(xe, g_ref[...], mod_ref[0:1, :], mod_ref[1:2, :])
    rid = lax.broadcasted_iota(I32, (tm + 2 * halo, 1), 0)
    keep = jnp.logical_and(jnp.logical_or(rid >= halo, jnp.logical_not(first)),
                           jnp.logical_or(rid < tm + halo, jnp.logical_not(last)))
    h = jnp.where(keep, h, 0.0)
    pbuf[...] = jnp.dot(h.astype(BF16), w_ref[...], preferred_element_type=F32)
    half = GDN_CONV // 2

    def conv_block(c0):
        acc = None
        for t in range(GDN_CONV):
            term = pbuf[pl.ds(halo - half + t, tm), pl.ds(c0, LANES)] * cw_ref[t:t + 1, pl.ds(c0, LANES)]
            acc = term if acc is None else acc + term
        return _silu(acc)

    for hh in range(3 * d // LANES):
        c0 = hh * LANES
        blk = conv_block(c0)
        if hh < 2 * d // LANES:
            blk = blk * lax.rsqrt(jnp.sum(blk * blk, axis=-1, keepdims=True) + EPS)
        if hh < d // LANES:
            q_ref[:, c0:c0 + LANES] = (blk * dk ** -0.5).astype(BF16)
        elif hh < 2 * d // LANES:
            k_ref[:, c0 - d:c0 - d + LANES] = blk.astype(BF16)
        else:
            v_ref[:, c0 - 2 * d:c0 - 2 * d + LANES] = blk.astype(BF16)
    z_ref[...] = pbuf[halo:halo + tm, 3 * d:4 * d].astype(BF16)
    ab = pbuf[halo:halo + tm, 4 * d:4 * d + LANES]
    lane = lax.broadcasted_iota(I32, ab.shape, 1)
    is_a = (lane % 16) < 8
    g = -jnp.exp(alog_ref[...]) * jax.nn.softplus(ab + dtb_ref[...])
    gb_ref[...] = jnp.where(is_a, g, jax.nn.sigmoid(ab))


TRI_BASE = 16


def _tri_inverse(lm, ri, ci):
    n = lm.shape[0]

    def same(s):
        shift = int(math.log2(s))
        return (ri >> shift) == (ci >> shift)

    ld = jnp.where(same(TRI_BASE), lm, 0.0)
    p = jnp.where(ri == ci, 1.0, 0.0) - ld
    m = ld
    for _ in range(int(math.log2(TRI_BASE)) - 1):
        mb = m.astype(BF16)
        m = jnp.dot(mb, mb, preferred_element_type=F32)
        p = p + jnp.dot(p.astype(BF16), m.astype(BF16), preferred_element_type=F32)
    s = TRI_BASE
    while s < n:
        off = jnp.where(jnp.logical_and(same(2 * s), jnp.logical_not(same(s))), lm, 0.0)
        pb = p.astype(BF16)
        t = jnp.dot(pb, off.astype(BF16), preferred_element_type=F32)
        p = p - jnp.dot(t.astype(BF16), pb, preferred_element_type=F32)
        s *= 2
    return p


def _gdn_chunk(q, k, v, gc, gct, beta, state, hh, upper):
    c = q.shape[0]
    ri = lax.broadcasted_iota(I32, (c, c), 0)
    ci = lax.broadcasted_iota(I32, (c, c), 1)
    incl = (ri <= ci) if upper else (ri >= ci)
    strict = (ri < ci) if upper else (ri > ci)
    decay = jnp.exp(jnp.where(incl, gc - gct, NEG))
    kf = k.astype(F32)
    kb = kf * beta
    nt_dims = (((1,), (1,)), ((), ()))
    kk = lax.dot_general(kb.astype(BF16), k, nt_dims, preferred_element_type=F32)
    qk = lax.dot_general(q, k, nt_dims, preferred_element_type=F32)
    lower = jnp.where(strict, kk * decay, 0.0)
    intra = qk * decay
    tinv = _tri_inverse(lower, ri, ci)
    eg = jnp.exp(gc)
    rhs = jnp.concatenate([v.astype(F32) * beta, kb * eg], axis=1).astype(BF16)
    uw = jnp.dot(tinv.astype(BF16), rhs, preferred_element_type=F32)
    dv = v.shape[1]
    u, w = uw[:, :dv], uw[:, dv:]
    sb = state.astype(BF16)
    v_new = u - jnp.dot(w.astype(BF16), sb, preferred_element_type=F32)
    o = (jnp.dot((q.astype(F32) * eg).astype(BF16), sb, preferred_element_type=F32)
         + jnp.dot(intra.astype(BF16), v_new.astype(BF16), preferred_element_type=F32))
    g_last = gc[0:1, :] if upper else gc[c - 1:c, :]
    kdec = (kf * jnp.exp(g_last - gc)).astype(BF16)
    tn_dims = (((0,), (0,)), ((), ()))
    new_state = state * jnp.exp(g_last) + lax.dot_general(kdec, v_new.astype(BF16), tn_dims,
                                                          preferred_element_type=F32)
    return o, new_state


def _gdn_scan_kernel(qf_ref, kf_ref, vf_ref, gf_ref, qb_ref, kb_ref, vb_ref, gbk_ref,
                     of_ref, ob_ref, sf, sb, *, dk):
    s = pl.program_id(1)

    @pl.when(s == 0)
    def _():
        sf[...] = jnp.zeros_like(sf)
        sb[...] = jnp.zeros_like(sb)

    c = qf_ref.shape[0]
    ri = lax.broadcasted_iota(I32, (c, c), 0)
    ci = lax.broadcasted_iota(I32, (c, c), 1)
    tri_l = (ri >= ci).astype(F32)
    tri_u = (ri <= ci).astype(F32)
    for upper, (q_ref, k_ref, v_ref, g_ref, o_ref, st) in enumerate(
            ((qf_ref, kf_ref, vf_ref, gf_ref, of_ref, sf), (qb_ref, kb_ref, vb_ref, gbk_ref, ob_ref, sb))):
        gbv = g_ref[...]
        csum = jnp.dot(tri_u if upper else tri_l, gbv, precision=HIGHEST, preferred_element_type=F32)
        csum_t = csum.T
        base = 16 * upper
        for hh in range(HEADS):
            sl = slice(hh * dk, (hh + 1) * dk)
            o, new_state = _gdn_chunk(q_ref[:, sl], k_ref[:, sl], v_ref[:, sl],
                                      csum[:, base + hh:base + hh + 1],
                                      csum_t[base + hh:base + hh + 1, :],
                                      gbv[:, base + 8 + hh:base + 9 + hh], st[hh], hh, bool(upper))
            o_ref[:, sl] = o.astype(BF16)
            st[hh] = new_state


def _gdn_mixer(xall, mod_i, norm_gain, w_in, conv_w, a_log, dt_bias, dims):
    bsz, ctx_len, seq, d, tm = dims
    ltot = ctx_len + seq
    nt, nct = ltot // tm, ctx_len // tm
    n_tiles = bsz * nt
    rows = n_tiles * tm
    dk = d // HEADS
    n_in = w_in.shape[1]
    n_pad = -(-n_in // LANES) * LANES
    wp = jnp.pad(w_in, ((0, 0), (0, n_pad - n_in))).astype(BF16)
    zeros8 = jnp.zeros((2, HEADS), F32)
    lay = lambda t: jnp.pad(jnp.concatenate([t, zeros8], axis=1).reshape(1, 4 * HEADS),
                            ((0, 0), (0, LANES - 4 * HEADS)))
    mrow = _mod_row_map(nt, nct, bsz)
    hb = tm // SUBLANES
    last_hblk = rows // SUBLANES - 1
    const = lambda j: (0, 0)
    q, k, v, z, gb = pl.pallas_call(
        functools.partial(_gdn_proj_kernel, d=d, nt=nt, nct=nct, tm=tm, dk=dk),
        grid=(n_tiles,),
        in_specs=[pl.BlockSpec((SUBLANES, d), lambda j: (jnp.maximum(j * hb - 1, 0), 0)),
                  pl.BlockSpec((tm, d), lambda j: (j, 0)),
                  pl.BlockSpec((SUBLANES, d), lambda j: (jnp.minimum((j + 1) * hb, last_hblk), 0)),
                  pl.BlockSpec((None, 6, d), lambda j: (mrow(j), 0, 0)),
                  pl.BlockSpec((1, d), const),
                  pl.BlockSpec((d, n_pad), const),
                  pl.BlockSpec((GDN_CONV, 3 * d), const),
                  pl.BlockSpec((1, LANES), const),
                  pl.BlockSpec((1, LANES), const)],
        out_specs=[pl.BlockSpec((tm, d), lambda j: (j, 0))] * 4 + [pl.BlockSpec((tm, LANES), lambda j: (j, 0))],
        out_shape=[jax.ShapeDtypeStruct((rows, d), BF16)] * 4 + [jax.ShapeDtypeStruct((rows, LANES), F32)],
        scratch_shapes=[pltpu.VMEM((tm + 2 * SUBLANES, n_pad), F32)],
        compiler_params=_cparams("parallel"),
        name="gdn_proj",
    )(xall, xall, xall, mod_i, norm_gain.reshape(1, d), wp, conv_w, lay(a_log), lay(dt_bias))
    c = GDN_CHUNK
    ncl, ncc = ltot // c, ctx_len // c

    def fwd(b, s):
        return (b * ncl + s, 0)

    def bwd(b, s):
        return (b * ncl + jnp.where(s < ncc, ncc - 1 - s, ncl + ncc - 1 - s), 0)

    blk = lambda m: pl.BlockSpec((c, d), m)
    gblk = lambda m: pl.BlockSpec((c, LANES), m)
    o_f, o_b = pl.pallas_call(
        functools.partial(_gdn_scan_kernel, dk=dk),
        grid=(bsz, ncl),
        in_specs=[blk(fwd), blk(fwd), blk(fwd), gblk(fwd), blk(bwd), blk(bwd), blk(bwd), gblk(bwd)],
        out_specs=[blk(fwd), blk(bwd)],
        out_shape=[jax.ShapeDtypeStruct((rows, d), BF16)] * 2,
        scratch_shapes=[pltpu.VMEM((HEADS, dk, dk), F32), pltpu.VMEM((HEADS, dk, dk), F32)],
        compiler_params=_cparams("parallel", "arbitrary"),
        name="gdn_scan",
    )(q, k, v, gb, q, k, v, gb)
    return (o_f, o_b, z)


def _split_bf16(x):
    hi = x.astype(BF16)
    lo = (x - hi.astype(F32)).astype(BF16)
    return hi, lo


def _post_kernel(*refs, kind, d, tm, dk):
    if kind == 0:
        of_ref, ob_ref, z_ref, og_ref = refs[:4]
        refs = refs[4:]
    else:
        o_ref = refs[0]
        refs = refs[1:]
    (x_ref, mod_ref, wo_ref, g_ref, wrh_ref, wrl_ref, br_ref,
     xo_ref, h_ref, ids_ref, gate_ref, cnt_ref, base) = refs
    j = pl.program_id(0)

    @pl.when(j == 0)
    def _():
        base[...] = jnp.zeros_like(base)

    if kind == 0:
        parts = []
        for hh in range(d // dk):
            sl = slice(hh * dk, (hh + 1) * dk)
            o = of_ref[:, sl].astype(F32) + ob_ref[:, sl].astype(F32)
            o = o * lax.rsqrt(jnp.mean(o * o, axis=-1, keepdims=True) + EPS) * og_ref[...]
            parts.append((o * _silu(z_ref[:, sl].astype(F32))).astype(BF16))
        o_in = jnp.concatenate(parts, axis=1)
    else:
        o_in = o_ref[...]
    mod = mod_ref[...]
    x = x_ref[...] + mod[2:3, :] * jnp.dot(o_in, wo_ref[...], preferred_element_type=F32)
    xo_ref[...] = x
    h = _norm_mod(x, g_ref[...], mod[3:4, :], mod[4:5, :])
    h_ref[...] = h
    hi, lo = _split_bf16(h)
    logits = (jnp.dot(hi, wrh_ref[...], preferred_element_type=F32)
              + jnp.dot(lo, wrh_ref[...], preferred_element_type=F32)
              + jnp.dot(hi, wrl_ref[...], preferred_element_type=F32)) + br_ref[...]
    lane = lax.broadcasted_iota(I32, logits.shape, 1)
    big = jnp.int32(1 << 20)
    is_g = lane < MOE_GROUPS
    gl = jnp.where(is_g, logits, NEG)
    gmax = jnp.max(gl, axis=-1, keepdims=True)
    gsel = jnp.min(jnp.where(gl == gmax, lane, big), axis=-1, keepdims=True)
    p_group = 1.0 / jnp.sum(jnp.where(is_g, jnp.exp(gl - gmax), 0.0), axis=-1, keepdims=True)
    in_grp = jnp.logical_and(lane >= MOE_GROUPS + gsel * MOE_PER_GROUP,
                             lane < MOE_GROUPS + (gsel + 1) * MOE_PER_GROUP)
    el = jnp.where(in_grp, logits, NEG)
    v0 = jnp.max(el, axis=-1, keepdims=True)
    i0 = jnp.min(jnp.where(el == v0, lane, big), axis=-1, keepdims=True)
    el1 = jnp.where(lane == i0, NEG, el)
    v1 = jnp.max(el1, axis=-1, keepdims=True)
    i1 = jnp.min(jnp.where(el1 == v1, lane, big), axis=-1, keepdims=True)
    e1 = jnp.exp(v1 - v0)
    w0 = p_group / (1.0 + e1)
    w1 = p_group * e1 / (1.0 + e1)
    oh0 = lane == i0
    oh1 = lane == i1
    onehot = jnp.where(jnp.logical_or(oh0, oh1), 1.0, 0.0)
    ri = lax.broadcasted_iota(I32, (tm, tm), 0)
    ci = lax.broadcasted_iota(I32, (tm, tm), 1)
    tri = jnp.where(ri > ci, 1.0, 0.0).astype(BF16)
    before = base[...] + jnp.dot(tri, onehot.astype(BF16), preferred_element_type=F32)
    r0 = jnp.sum(jnp.where(oh0, before, 0.0), axis=-1, keepdims=True)
    r1 = jnp.sum(jnp.where(oh1, before, 0.0), axis=-1, keepdims=True)
    new_base = base[...] + jnp.sum(onehot, axis=0, keepdims=True)
    base[...] = new_base
    cnt_ref[...] = new_base
    e0 = i0 - MOE_GROUPS
    e1i = i1 - MOE_GROUPS
    ids_ref[...] = jnp.where(lane == 0, e0, jnp.where(lane == 1, e1i, jnp.where(
        lane == 2, r0.astype(I32), jnp.where(lane == 3, r1.astype(I32), 0))))
    gate_ref[...] = jnp.where(lane == 0, w0, jnp.where(lane == 1, w1, 0.0))


def _post_mixer(kind, mixer_out, xall, mod_i, w_out, ffn_gain, w_group, b_group, w_expert, b_expert,
                dims, o_gain=None):
    bsz, ctx_len, seq, d, tm = dims
    ltot = ctx_len + seq
    nt, nct = ltot // tm, ctx_len // tm
    n_tiles = bsz * nt
    rows = n_tiles * tm
    dk = d // HEADS
    mrow = _mod_row_map(nt, nct, bsz)
    wr = jnp.pad(jnp.concatenate([w_group, w_expert], axis=1),
                 ((0, 0), (0, LANES - MOE_GROUPS - MOE_EXPERTS)))
    wr_hi = wr.astype(BF16)
    wr_lo = (wr - wr_hi.astype(F32)).astype(BF16)
    br = jnp.pad(jnp.concatenate([b_group, b_expert]), (0, LANES - MOE_GROUPS - MOE_EXPERTS)).reshape(1, LANES)
    const = lambda j: (0, 0)
    row = lambda j: (j, 0)
    lead_specs = [pl.BlockSpec((tm, d), row)] * len(mixer_out)
    lead_args = list(mixer_out)
    if kind == 0:
        lead_specs.append(pl.BlockSpec((1, dk), const))
        lead_args.append(o_gain.reshape(1, dk))
    n_lead = len(lead_args)
    outs = pl.pallas_call(
        functools.partial(_post_kernel, kind=kind, d=d, tm=tm, dk=dk),
        grid=(n_tiles,),
        in_specs=lead_specs + [pl.BlockSpec((tm, d), row),
                               pl.BlockSpec((None, 6, d), lambda j: (mrow(j), 0, 0)),
                               pl.BlockSpec((w_out.shape[0], d), const),
                               pl.BlockSpec((1, d), const),
                               pl.BlockSpec((d, LANES), const),
                               pl.BlockSpec((d, LANES), const),
                               pl.BlockSpec((1, LANES), const)],
        out_specs=[pl.BlockSpec((tm, d), row), pl.BlockSpec((tm, d), row),
                   pl.BlockSpec((tm, LANES), row), pl.BlockSpec((tm, LANES), row),
                   pl.BlockSpec((1, LANES), const)],
        out_shape=[jax.ShapeDtypeStruct((rows, d), F32), jax.ShapeDtypeStruct((rows, d), F32),
                   jax.ShapeDtypeStruct((rows, LANES), I32), jax.ShapeDtypeStruct((rows, LANES), F32),
                   jax.ShapeDtypeStruct((1, LANES), F32)],
        scratch_shapes=[pltpu.VMEM((1, LANES), F32)],
        input_output_aliases={n_lead: 0},
        compiler_params=_cparams("arbitrary"),
        name="post_mixer",
    )(*lead_args, xall, mod_i, w_out.astype(BF16), ffn_gain.reshape(1, d), wr_hi, wr_lo, br)
    return outs


def _dispatch_kernel(zlo_ref, zhi_ref, nu_ref, dest_ref, h_ref, xs_ref, zblk, sem, zsem, *, tm, blk, n_blocks):
    j = pl.program_id(0)

    def row_copy(src, dst_row, s):
        return pltpu.make_async_copy(src, xs_ref.at[pl.ds(dst_row, 1)], s)

    def blk_copy(bi):
        return pltpu.make_async_copy(zblk, xs_ref.at[pl.ds(pl.multiple_of(bi * blk, blk), blk)], zsem)

    @pl.when(j == 0)
    def _():
        zblk[...] = jnp.zeros_like(zblk)

        def per_expert(e, carry):
            lo, hi = zlo_ref[e], zhi_ref[e]

            def start(r, c):
                row_copy(zblk.at[pl.ds(0, 1)], r, zsem).start()
                return c

            def wait(r, c):
                row_copy(zblk.at[pl.ds(0, 1)], r, zsem).wait()
                return c

            lax.fori_loop(lo, hi, start, 0)
            lax.fori_loop(lo, hi, wait, 0)
            return carry

        lax.fori_loop(0, MOE_EXPERTS, per_expert, 0)

        def tail_start(bi, c):
            blk_copy(bi).start()
            return c

        def tail_wait(bi, c):
            blk_copy(bi).wait()
            return c

        lax.fori_loop(nu_ref[0], n_blocks, tail_start, 0)
        lax.fori_loop(nu_ref[0], n_blocks, tail_wait, 0)

    def body(r, c):
        for kk in range(MOE_TOP_K):
            row_copy(h_ref.at[pl.ds(r, 1)], dest_ref[0, MOE_TOP_K * r + kk], sem).start()
        return c

    lax.fori_loop(0, tm, body, 0)
    for _ in range(MOE_TOP_K):
        pltpu.make_async_copy(h_ref, xs_ref.at[pl.ds(0, tm)], sem).wait()


def _expert_kernel(be_ref, nu_ref, x_ref, w1_ref, w3_ref, w2_ref, y_ref):
    j = pl.program_id(0)

    @pl.when(j < nu_ref[0])
    def _():
        x = x_ref[...].astype(BF16)
        a = jnp.dot(x, w1_ref[...], preferred_element_type=F32)
        b = jnp.dot(x, w3_ref[...], preferred_element_type=F32)
        y_ref[...] = jnp.dot((_silu(a) * b).astype(BF16), w2_ref[...], preferred_element_type=F32)

    @pl.when(j >= nu_ref[0])
    def _():
        y_ref[...] = jnp.zeros_like(y_ref)


def _combine_kernel(dest_ref, x_ref, gate_ref, mod_ref, fg_ref, yb_ref, xo_ref, ybuf, sem, *, tm, final):
    def body(r, c):
        for kk in range(MOE_TOP_K):
            pltpu.make_async_copy(yb_ref.at[pl.ds(dest_ref[0, MOE_TOP_K * r + kk], 1)],
                                  ybuf.at[kk, pl.ds(r, 1)], sem).start()
        return c

    lax.fori_loop(0, tm, body, 0)
    for kk in range(MOE_TOP_K):
        pltpu.make_async_copy(yb_ref.at[pl.ds(0, tm)], ybuf.at[kk], sem).wait()
    gate = gate_ref[...]
    y = ybuf[0] * gate[:, 0:1] + ybuf[1] * gate[:, 1:2]
    x = x_ref[...] + mod_ref[5:6, :] * y
    if final:
        x = x * lax.rsqrt(jnp.mean(x * x, axis=-1, keepdims=True) + EPS) * fg_ref[...]
    xo_ref[...] = x


def _moe(xall, h, ids, gates, counts, mod_i, w1, w3, w2, final_gain, dims, final):
    bsz, ctx_len, seq, d, tm = dims
    ltot = ctx_len + seq
    nt, nct = ltot // tm, ctx_len // tm
    n_tiles = bsz * nt
    rows = n_tiles * tm
    e = MOE_EXPERTS
    blk = MOE_BLOCK
    a = rows * MOE_TOP_K
    n_blocks = -(-(a + e * (blk - 1)) // blk)
    cnt = counts[0, MOE_GROUPS:MOE_GROUPS + e].astype(I32)
    padded = (cnt + blk - 1) // blk * blk
    pad_end = jnp.cumsum(padded)
    pad_start = pad_end - padded
    dest = (pad_start[ids[:, :MOE_TOP_K]] + ids[:, MOE_TOP_K:2 * MOE_TOP_K]).astype(I32)
    dest = dest.reshape(n_tiles, 1, tm * MOE_TOP_K)
    n_used = (pad_end[-1] // blk).astype(I32).reshape(1)
    blk_expert = jnp.minimum(jnp.searchsorted(pad_end, jnp.arange(n_blocks, dtype=I32) * blk, side='right'),
                             e - 1).astype(I32)
    smem_dest = pl.BlockSpec((None, 1, tm * MOE_TOP_K), lambda j, *_: (j, 0, 0), memory_space=pltpu.SMEM)
    xs = pl.pallas_call(
        functools.partial(_dispatch_kernel, tm=tm, blk=blk, n_blocks=n_blocks),
        grid_spec=pltpu.PrefetchScalarGridSpec(
            num_scalar_prefetch=3, grid=(n_tiles,),
            in_specs=[smem_dest, pl.BlockSpec((tm, d), lambda j, *_: (j, 0))],
            out_specs=pl.BlockSpec(memory_space=pl.ANY),
            scratch_shapes=[pltpu.VMEM((blk, d), F32), pltpu.SemaphoreType.DMA,
                            pltpu.SemaphoreType.DMA]),
        out_shape=jax.ShapeDtypeStruct((n_blocks * blk, d), F32),
        compiler_params=_cparams("arbitrary"),
        name="moe_dispatch",
    )((pad_start + cnt).astype(I32), pad_end.astype(I32), n_used, dest, h)

    def xmap(j, be, nu):
        return (jnp.minimum(j, nu[0] - 1), 0)

    def wmap(j, be, nu):
        return (be[jnp.minimum(j, nu[0] - 1)], 0, 0)

    f = w1.shape[-1]
    yb = pl.pallas_call(
        _expert_kernel,
        grid_spec=pltpu.PrefetchScalarGridSpec(
            num_scalar_prefetch=2, grid=(n_blocks,),
            in_specs=[pl.BlockSpec((blk, d), xmap),
                      pl.BlockSpec((None, d, f), wmap),
                      pl.BlockSpec((None, d, f), wmap),
                      pl.BlockSpec((None, f, d), wmap)],
            out_specs=pl.BlockSpec((blk, d), lambda j, be, nu: (j, 0))),
        out_shape=jax.ShapeDtypeStruct((n_blocks * blk, d), F32),
        compiler_params=_cparams("arbitrary"),
        name="moe_experts",
    )(blk_expert, n_used, xs, w1.astype(BF16), w3.astype(BF16), w2.astype(BF16))

    mrow = _mod_row_map(nt, nct, bsz)
    out = pl.pallas_call(
        functools.partial(_combine_kernel, tm=tm, final=final),
        grid=(n_tiles,),
        in_specs=[pl.BlockSpec((None, 1, tm * MOE_TOP_K), lambda j: (j, 0, 0), memory_space=pltpu.SMEM),
                  pl.BlockSpec((tm, d), lambda j: (j, 0)),
                  pl.BlockSpec((tm, LANES), lambda j: (j, 0)),
                  pl.BlockSpec((None, 6, d), lambda j: (mrow(j), 0, 0)),
                  pl.BlockSpec((1, d), lambda j: (0, 0)),
                  pl.BlockSpec(memory_space=pl.ANY)],
        out_specs=pl.BlockSpec((tm, d), lambda j: (j, 0)),
        out_shape=jax.ShapeDtypeStruct((rows, d), F32),
        scratch_shapes=[pltpu.VMEM((MOE_TOP_K, tm, d), F32), pltpu.SemaphoreType.DMA],
        input_output_aliases={1: 0},
        compiler_params=_cparams("arbitrary"),
        name="moe_combine",
    )(dest, xall, gates, mod_i, final_gain.reshape(1, d), yb)
    return out


def kernel(x, c, ctx, c_ctx, w_mod, b_mod, norm_mix, norm_ffn, gdn_w_in, gdn_conv, gdn_a_log, gdn_dt_bias, gdn_norm, gdn_w_out, diff_w_qkv, diff_lambda, diff_norm, diff_w_out, mla_w_down, mla_q_norm, mla_kv_norm, mla_w_uq, mla_w_ukv, mla_w_out, moe_w_group, moe_b_group, moe_w_expert, moe_b_expert, moe_w1, moe_w3, moe_w2, final_norm):
    bsz, seq, d = x.shape
    ctx_len = ctx.shape[1]
    depth = w_mod.shape[0]
    tm = _row_tile(ctx_len)
    assert d % LANES == 0 and d // HEADS == LANES
    assert ctx_len % tm == 0 and seq % tm == 0 and ctx_len % GDN_CHUNK == 0 and seq % GDN_CHUNK == 0
    dims = (bsz, ctx_len, seq, d, tm)
    ltot = ctx_len + seq
    xall = jnp.concatenate([ctx, x], axis=1).reshape(bsz * ltot, d)
    mod = _mod_vectors(c, c_ctx, w_mod, b_mod)
    tables = _rope_tables(seq, ctx_len)
    for i in range(depth):
        kind, j = i % N_MIXERS, i // N_MIXERS
        if kind == 0:
            mixer_out = _gdn_mixer(xall, mod[i], norm_mix[i], gdn_w_in[j], gdn_conv[j], gdn_a_log[j],
                                   gdn_dt_bias[j], dims)
            w_out, o_gain = gdn_w_out[j], gdn_norm[j]
        elif kind == 1:
            lam_init = 0.8 - 0.6 * math.exp(-0.3 * i)
            mixer_out = _diff_mixer(xall, mod[i], norm_mix[i], diff_w_qkv[j], diff_lambda[j], diff_norm[j],
                                    dims, tables, lam_init)
            w_out, o_gain = diff_w_out[j], None
        else:
            mixer_out = _mla_mixer(xall, mod[i], norm_mix[i], mla_w_down[j], mla_q_norm[j], mla_kv_norm[j],
                                   mla_w_uq[j], mla_w_ukv[j], dims, tables)
            w_out, o_gain = mla_w_out[j], None
        xall, h, ids, gates, counts = _post_mixer(kind, mixer_out, xall, mod[i], w_out, norm_ffn[i],
                                                  moe_w_group[i], moe_b_group[i], moe_w_expert[i],
                                                  moe_b_expert[i], dims, o_gain)
        xall = _moe(xall, h, ids, gates, counts, mod[i], moe_w1[i], moe_w3[i], moe_w2[i], final_norm, dims,
                    final=(i == depth - 1))
    return xall.reshape(bsz, ltot, d)[:, ctx_len:]
```

```python
import functools
import math

import jax
import jax.numpy as jnp
from jax import lax
from jax.experimental import pallas as pl
from jax.experimental.pallas import tpu as pltpu

F32 = jnp.float32
BF16 = jnp.bfloat16
I32 = jnp.int32
HIGHEST = lax.Precision.HIGHEST

LANES = 128
SUBLANES = 8
VMEM_LIMIT = 56 * 1024 * 1024

EPS = 1e-6
GRID_W = 64
ROPE_THETA = 10000.0
N_MIXERS = 3
HEADS = 8
GDN_CONV = 5
GDN_CHUNK = 64
DIFF_SUBLN_EPS = 1e-5
MLA_NOPE = 128
MLA_ROPE = 64
MLA_V = 128
MOE_GROUPS = 4
MOE_PER_GROUP = 8
MOE_EXPERTS = MOE_GROUPS * MOE_PER_GROUP
MOE_TOP_K = 2
MOE_BLOCK = 256
NEG = -1e30


def _cparams(*sem):
    return pltpu.CompilerParams(dimension_semantics=sem, vmem_limit_bytes=VMEM_LIMIT)


def _row_tile(ctx_len):
    return 256 if ctx_len % 256 == 0 else 128


def _mod_row_map(nt, nct, bsz):
    def f(j):
        return jnp.where(j % nt < nct, bsz, j // nt)
    return f


def _norm_mod(x, gain, shift, scale, eps=EPS):
    var = jnp.mean(x * x, axis=-1, keepdims=True)
    y = x * lax.rsqrt(var + eps) * gain
    return y * (1.0 + scale) + shift


def _silu(x):
    return x * jax.nn.sigmoid(x)


def _mod_kernel(c_ref, w_ref, b_ref, o_ref):
    s = _silu(c_ref[...])
    o_ref[...] = jnp.dot(s, w_ref[...], precision=HIGHEST, preferred_element_type=F32) + b_ref[...]


def _mod_vectors(c, c_ctx, w_mod, b_mod):
    depth, d, n = w_mod.shape
    bsz = c.shape[0]
    rows = -(-(bsz + 1) // SUBLANES) * SUBLANES
    cc = jnp.zeros((rows, d), F32).at[:bsz].set(c).at[bsz].set(c_ctx)
    tn = 512
    out = pl.pallas_call(
        _mod_kernel,
        grid=(depth, n // tn),
        in_specs=[pl.BlockSpec((rows, d), lambda i, j: (0, 0)),
                  pl.BlockSpec((None, d, tn), lambda i, j: (i, 0, j)),
                  pl.BlockSpec((None, 1, tn), lambda i, j: (i, 0, j))],
        out_specs=pl.BlockSpec((None, rows, tn), lambda i, j: (i, 0, j)),
        out_shape=jax.ShapeDtypeStruct((depth, rows, n), F32),
        compiler_params=_cparams("parallel", "parallel"),
        name="mod_vectors",
    )(cc, w_mod, b_mod.reshape(depth, 1, n))
    return out.reshape(depth, rows, 6, d)


def _rope_tables(seq, ctx_len):
    quarter = 16
    inv_freq = ROPE_THETA ** (-jnp.arange(quarter, dtype=F32) / quarter)
    t = jnp.arange(seq)
    row = (t // GRID_W).astype(F32)[:, None] * inv_freq
    col = (t % GRID_W).astype(F32)[:, None] * inv_freq
    cos = jnp.concatenate([jnp.cos(row), jnp.cos(row), jnp.cos(col), jnp.cos(col)], axis=1)
    sin = jnp.concatenate([-jnp.sin(row), jnp.sin(row), -jnp.sin(col), jnp.sin(col)], axis=1)
    cos = jnp.concatenate([jnp.ones((ctx_len, 64), F32), cos], axis=0)
    sin = jnp.concatenate([jnp.zeros((ctx_len, 64), F32), sin], axis=0)
    return jnp.tile(cos, (1, 2)), jnp.tile(sin, (1, 2))


def _rope128(blk, cos, sin):
    lane = lax.broadcasted_iota(I32, blk.shape, 1)
    first = (lane % 32) < 16
    partner = jnp.where(first, pltpu.roll(blk, LANES - 16, 1), pltpu.roll(blk, 16, 1))
    return blk * cos + partner * sin


def _diff_proj_kernel(x_ref, mod_ref, g_ref, w_ref, cos_ref, sin_ref, o_ref, vt_ref, *, d, q_scale):
    h = _norm_mod(x_ref[...], g_ref[...], mod_ref[0:1, :], mod_ref[1:2, :])
    p = jnp.dot(h.astype(BF16), w_ref[...], preferred_element_type=F32)
    cos = cos_ref[...]
    sin = sin_ref[...]
    nqk = 2 * d // LANES
    for cb in range(nqk):
        r = _rope128(p[:, cb * LANES:(cb + 1) * LANES], cos, sin)
        if cb < nqk // 2:
            r = r * q_scale
        o_ref[:, cb * LANES:(cb + 1) * LANES] = r.astype(BF16)
    vt_ref[...] = p[:, 2 * d:].T.astype(BF16)


def _pick_tk(n):
    for cand in (768, 512, 384, 256, 128):
        if n % cand == 0:
            return cand
    raise ValueError(n)


def _flash_t(qs, k_ref, vt_ref, m_ref, l_ref, acc_ref, nsteps, tk):
    for s in range(len(qs)):
        m_ref[s] = jnp.full(m_ref.shape[1:], NEG, F32)
        l_ref[s] = jnp.zeros(l_ref.shape[1:], F32)
        acc_ref[s] = jnp.zeros(acc_ref.shape[1:], F32)
    nt_dims = (((1,), (1,)), ((), ()))

    def step(i, c):
        off = pl.multiple_of(i * tk, tk)
        kc = k_ref[pl.ds(off, tk), :]
        vtc = vt_ref[:, pl.ds(off, tk)]
        sts = [lax.dot_general(kc, q, nt_dims, preferred_element_type=F32) for q in qs]
        alphas, ps = [], []
        for s, st in enumerate(sts):
            m_old = m_ref[s]
            m_new = jnp.maximum(m_old, jnp.max(st, axis=0, keepdims=True))
            alpha = jnp.exp2(m_old - m_new)
            p = jnp.exp2(st - m_new)
            l_ref[s] = alpha * l_ref[s] + jnp.sum(p, axis=0, keepdims=True)
            m_ref[s] = m_new
            alphas.append(alpha)
            ps.append(p.astype(BF16))
        pvs = [jnp.dot(vtc, p, preferred_element_type=F32) for p in ps]
        for s in range(len(qs)):
            acc_ref[s] = alphas[s] * acc_ref[s] + pvs[s]
        return c

    lax.fori_loop(0, nsteps, step, 0)


def _flash_ctx_or_all(qs, k_ref, vt_ref, m_ref, l_ref, acc_ref, nct, ctx_len, ltot):
    i = pl.program_id(2)
    tk_c, tk_l = _pick_tk(ctx_len), _pick_tk(ltot)

    @pl.when(i < nct)
    def _():
        _flash_t(qs, k_ref, vt_ref, m_ref, l_ref, acc_ref, ctx_len // tk_c, tk_c)

    @pl.when(i >= nct)
    def _():
        _flash_t(qs, k_ref, vt_ref, m_ref, l_ref, acc_ref, ltot // tk_l, tk_l)


def _diff_attn_kernel(lam_ref, gain_ref, q_ref, k_ref, vt_ref, o_ref, m_ref, l_ref, acc_ref, *, nct,
                      ctx_len, ltot, lam_init):
    q = q_ref[...]
    lane = lax.broadcasted_iota(I32, q.shape, 1)
    half = q.shape[1] // 2
    zero = jnp.zeros_like(q)
    qs = [jnp.where(lane < half, q, zero), jnp.where(lane >= half, q, zero)]
    _flash_ctx_or_all(qs, k_ref, vt_ref, m_ref, l_ref, acc_ref, nct, ctx_len, ltot)
    lv = lam_ref[...]
    lam = (jnp.exp(jnp.sum(lv[0:1] * lv[1:2], keepdims=True))
           - jnp.exp(jnp.sum(lv[2:3] * lv[3:4], keepdims=True)) + lam_init)
    o = acc_ref[0] / l_ref[0] - lam * (acc_ref[1] / l_ref[1])
    var = jnp.mean(o * o, axis=0, keepdims=True)
    o = o * lax.rsqrt(var + DIFF_SUBLN_EPS) * gain_ref[...] * (1.0 - lam_init)
    o_ref[...] = o.T.astype(BF16)


def _diff_mixer(xall, mod_i, norm_gain, w_qkv, lam_vec, sub_gain, dims, tables, lam_init):
    bsz, ctx_len, seq, d, tm = dims
    ltot = ctx_len + seq
    nt, nct = ltot // tm, ctx_len // tm
    n_tiles = bsz * nt
    dh = d // HEADS // 2
    cos, sin = tables
    mrow = _mod_row_map(nt, nct, bsz)
    rows = n_tiles * tm
    qk, vt = pl.pallas_call(
        functools.partial(_diff_proj_kernel, d=d, q_scale=dh ** -0.5 * math.log2(math.e)),
        grid=(n_tiles,),
        in_specs=[pl.BlockSpec((tm, d), lambda j: (j, 0)),
                  pl.BlockSpec((None, 6, d), lambda j: (mrow(j), 0, 0)),
                  pl.BlockSpec((1, d), lambda j: (0, 0)),
                  pl.BlockSpec((d, 3 * d), lambda j: (0, 0)),
                  pl.BlockSpec((tm, LANES), lambda j: (j % nt, 0)),
                  pl.BlockSpec((tm, LANES), lambda j: (j % nt, 0))],
        out_specs=[pl.BlockSpec((tm, 2 * d), lambda j: (j, 0)),
                   pl.BlockSpec((d, tm), lambda j: (0, j))],
        out_shape=[jax.ShapeDtypeStruct((rows, 2 * d), BF16), jax.ShapeDtypeStruct((d, rows), BF16)],
        compiler_params=_cparams("parallel"),
        name="diff_proj",
    )(xall, mod_i, norm_gain.reshape(1, d), w_qkv.astype(BF16), cos, sin)
    hw = 2 * dh
    o = pl.pallas_call(
        functools.partial(_diff_attn_kernel, nct=nct, ctx_len=ctx_len, ltot=ltot, lam_init=lam_init),
        grid=(bsz, HEADS, nt),
        in_specs=[pl.BlockSpec((4, dh), lambda b, h, i: (0, 0)),
                  pl.BlockSpec((hw, 1), lambda b, h, i: (0, 0)),
                  pl.BlockSpec((tm, hw), lambda b, h, i: (b * nt + i, h)),
                  pl.BlockSpec((ltot, hw), lambda b, h, i: (b, HEADS + h)),
                  pl.BlockSpec((hw, ltot), lambda b, h, i: (h, b))],
        out_specs=pl.BlockSpec((tm, hw), lambda b, h, i: (b * nt + i, h)),
        out_shape=jax.ShapeDtypeStruct((rows, d), BF16),
        scratch_shapes=[pltpu.VMEM((2, 1, tm), F32), pltpu.VMEM((2, 1, tm), F32),
                        pltpu.VMEM((2, hw, tm), F32)],
        compiler_params=_cparams("parallel", "parallel", "arbitrary"),
        name="diff_attn",
    )(lam_vec, sub_gain.reshape(hw, 1), qk, qk, vt)
    return (o,)


def _mla_proj_kernel(x_ref, mod_ref, g_ref, wd_ref, qg_ref, kvg_ref, wq_ref, wkv_ref, cos_ref, sin_ref,
                     q_ref, k_ref, v_ref, *, q_lora, kv_lora, scale):
    h = _norm_mod(x_ref[...], g_ref[...], mod_ref[0:1, :], mod_ref[1:2, :])
    p = jnp.dot(h.astype(BF16), wd_ref[...], preferred_element_type=F32)
    cq = p[:, :q_lora]
    cq = cq * lax.rsqrt(jnp.mean(cq * cq, axis=-1, keepdims=True) + EPS) * qg_ref[...]
    ckv = p[:, q_lora:q_lora + kv_lora]
    ckv = ckv * lax.rsqrt(jnp.mean(ckv * ckv, axis=-1, keepdims=True) + EPS) * kvg_ref[...]
    cos = cos_ref[...]
    sin = sin_ref[...]
    kr = _rope128(p[:, q_lora + kv_lora:], cos, sin).astype(BF16)
    q = jnp.dot(cq.astype(BF16), wq_ref[...], preferred_element_type=F32)
    kv = jnp.dot(ckv.astype(BF16), wkv_ref[...], preferred_element_type=F32)
    hq = MLA_NOPE + LANES
    for hh in range(HEADS):
        q_ref[:, hh * hq:hh * hq + MLA_NOPE] = (q[:, hh * hq:hh * hq + MLA_NOPE] * scale).astype(BF16)
        qr = _rope128(q[:, hh * hq + MLA_NOPE:(hh + 1) * hq], cos, sin) * scale
        q_ref[:, hh * hq + MLA_NOPE:(hh + 1) * hq] = qr.astype(BF16)
        k_ref[:, hh * hq:hh * hq + MLA_NOPE] = kv[:, hh * MLA_NOPE:(hh + 1) * MLA_NOPE].astype(BF16)
        k_ref[:, hh * hq + MLA_NOPE:(hh + 1) * hq] = kr
    v_ref[...] = kv[:, HEADS * MLA_NOPE:].T.astype(BF16)


def _mla_attn_kernel(q_ref, k_ref, vt_ref, o_ref, m_ref, l_ref, acc_ref, *, nct, ctx_len, ltot):
    _flash_ctx_or_all([q_ref[...]], k_ref, vt_ref, m_ref, l_ref, acc_ref, nct, ctx_len, ltot)
    o_ref[...] = (acc_ref[0] / l_ref[0]).T.astype(BF16)


def _mla_mixer(xall, mod_i, norm_gain, w_down, q_gain, kv_gain, w_uq, w_ukv, dims, tables):
    bsz, ctx_len, seq, d, tm = dims
    ltot = ctx_len + seq
    nt, nct = ltot // tm, ctx_len // tm
    n_tiles = bsz * nt
    rows = n_tiles * tm
    q_lora, kv_lora = q_gain.shape[0], kv_gain.shape[0]
    cos, sin = tables
    mrow = _mod_row_map(nt, nct, bsz)
    hq = MLA_NOPE + LANES
    wd = jnp.pad(w_down, ((0, 0), (0, LANES - MLA_ROPE))).astype(BF16)
    nd = wd.shape[1]
    wq = jnp.pad(w_uq.reshape(q_lora, HEADS, MLA_NOPE + MLA_ROPE),
                 ((0, 0), (0, 0), (0, LANES - MLA_ROPE))).reshape(q_lora, HEADS * hq).astype(BF16)
    wkv = w_ukv.reshape(kv_lora, HEADS, MLA_NOPE + MLA_V)
    wkv = jnp.concatenate([wkv[:, :, :MLA_NOPE].reshape(kv_lora, HEADS * MLA_NOPE),
                           wkv[:, :, MLA_NOPE:].reshape(kv_lora, HEADS * MLA_V)], axis=1).astype(BF16)
    scale = (MLA_NOPE + MLA_ROPE) ** -0.5 * math.log2(math.e)
    const = lambda j: (0, 0)
    q, k, v = pl.pallas_call(
        functools.partial(_mla_proj_kernel, q_lora=q_lora, kv_lora=kv_lora, scale=scale),
        grid=(n_tiles,),
        in_specs=[pl.BlockSpec((tm, d), lambda j: (j, 0)),
                  pl.BlockSpec((None, 6, d), lambda j: (mrow(j), 0, 0)),
                  pl.BlockSpec((1, d), const),
                  pl.BlockSpec((d, nd), const),
                  pl.BlockSpec((1, q_lora), const),
                  pl.BlockSpec((1, kv_lora), const),
                  pl.BlockSpec((q_lora, HEADS * hq), const),
                  pl.BlockSpec((kv_lora, HEADS * (MLA_NOPE + MLA_V)), const),
                  pl.BlockSpec((tm, LANES), lambda j: (j % nt, 0)),
                  pl.BlockSpec((tm, LANES), lambda j: (j % nt, 0))],
        out_specs=[pl.BlockSpec((tm, HEADS * hq), lambda j: (j, 0)),
                   pl.BlockSpec((tm, HEADS * hq), lambda j: (j, 0)),
                   pl.BlockSpec((HEADS * MLA_V, tm), lambda j: (0, j))],
        out_shape=[jax.ShapeDtypeStruct((rows, HEADS * hq), BF16),
                   jax.ShapeDtypeStruct((rows, HEADS * hq), BF16),
                   jax.ShapeDtypeStruct((HEADS * MLA_V, rows), BF16)],
        compiler_params=_cparams("parallel"),
        name="mla_proj",
    )(xall, mod_i, norm_gain.reshape(1, d), wd, q_gain.reshape(1, q_lora), kv_gain.reshape(1, kv_lora),
      wq, wkv, cos, sin)
    o = pl.pallas_call(
        functools.partial(_mla_attn_kernel, nct=nct, ctx_len=ctx_len, ltot=ltot),
        grid=(bsz, HEADS, nt),
        in_specs=[pl.BlockSpec((tm, hq), lambda b, h, i: (b * nt + i, h)),
                  pl.BlockSpec((ltot, hq), lambda b, h, i: (b, h)),
                  pl.BlockSpec((MLA_V, ltot), lambda b, h, i: (h, b))],
        out_specs=pl.BlockSpec((tm, MLA_V), lambda b, h, i: (b * nt + i, h)),
        out_shape=jax.ShapeDtypeStruct((rows, HEADS * MLA_V), BF16),
        scratch_shapes=[pltpu.VMEM((1, 1, tm), F32), pltpu.VMEM((1, 1, tm), F32),
                        pltpu.VMEM((1, MLA_V, tm), F32)],
        compiler_params=_cparams("parallel", "parallel", "arbitrary"),
        name="mla_attn",
    )(q, k, v)
    return (o,)


def _gdn_proj_kernel(xp_ref, x_ref, xn_ref, mod_ref, g_ref, w_ref, cw_ref, alog_ref, dtb_ref,
                     q_ref, k_ref, v_ref, z_ref, gb_ref, pbuf, *, d, nt, nct, tm, dk):
    j = pl.program_id(0)
    r = j % nt
    first = jnp.logical_or(r == 0, r == nct)
    last = jnp.logical_or(r == nct - 1, r == nt - 1)
    halo = SUBLANES
    xe = jnp.concatenate([xp_ref[...], x_ref[...], xn_ref[...]], axis=0)
    h = _norm_mod(xe, g_ref[...], mod_ref[0:1, :], mod_ref[1:2, :])
    rid = lax.broadcasted_iota(I32, (tm + 2 * halo, 1), 0)
    keep = jnp.logical_and(jnp.logical_or(rid >= halo, jnp.logical_not(first)),
                           jnp.logical_or(rid < tm + halo, jnp.logical_not(last)))
    h = jnp.where(keep, h, 0.0)
    pbuf[...] = jnp.dot(h.astype(BF16), w_ref[...], preferred_element_type=F32)
    half = GDN_CONV // 2

    def conv_block(c0):
        acc = None
        for t in range(GDN_CONV):
            term = pbuf[pl.ds(halo - half + t, tm), pl.ds(c0, LANES)] * cw_ref[t:t + 1, pl.ds(c0, LANES)]
            acc = term if acc is None else acc + term
        return _silu(acc)

    for hh in range(3 * d // LANES):
        c0 = hh * LANES
        blk = conv_block(c0)
        if hh < 2 * d // LANES:
            blk = blk * lax.rsqrt(jnp.sum(blk * blk, axis=-1, keepdims=True) + EPS)
        if hh < d // LANES:
            q_ref[:, c0:c0 + LANES] = (blk * dk ** -0.5).astype(BF16)
        elif hh < 2 * d // LANES:
            k_ref[:, c0 - d:c0 - d + LANES] = blk.astype(BF16)
        else:
            v_ref[:, c0 - 2 * d:c0 - 2 * d + LANES] = blk.astype(BF16)
    z_ref[...] = pbuf[halo:halo + tm, 3 * d:4 * d].astype(BF16)
    ab = pbuf[halo:halo + tm, 4 * d:4 * d + LANES]
    lane = lax.broadcasted_iota(I32, ab.shape, 1)
    is_a = (lane % 16) < 8
    g = -jnp.exp(alog_ref[...]) * jax.nn.softplus(ab + dtb_ref[...])
    gb_ref[...] = jnp.where(is_a, g, jax.nn.sigmoid(ab))


TRI_BASE = 16


def _mm(a, b):
    return jnp.dot(a.astype(BF16), b.astype(BF16), preferred_element_type=F32)


def _tri_inverse_many(lms, ri, ci):
    n = lms[0].shape[0]

    def same(s):
        shift = int(math.log2(s))
        return (ri >> shift) == (ci >> shift)

    eye = jnp.where(ri == ci, 1.0, 0.0)
    base = same(TRI_BASE)
    ms = [jnp.where(base, lm, 0.0) for lm in lms]
    ps = [eye - m for m in ms]
    for _ in range(int(math.log2(TRI_BASE)) - 1):
        ms = [_mm(m, m) for m in ms]
        ps = [p + _mm(p, m) for p, m in zip(ps, ms)]
    s = TRI_BASE
    while s < n:
        band = jnp.logical_and(same(2 * s), jnp.logical_not(same(s)))
        ts = [_mm(p, jnp.where(band, lm, 0.0)) for p, lm in zip(ps, lms)]
        ps = [p - _mm(t, p) for p, t in zip(ps, ts)]
        s *= 2
    return ps


def _gdn_chunks(probs, ri, ci):
    c = probs[0][0].shape[0]
    dv = probs[0][2].shape[1]
    nt_dims = (((1,), (1,)), ((), ()))
    tn_dims = (((0,), (0,)), ((), ()))
    incl = {False: ri >= ci, True: ri <= ci}
    strict = {False: ri > ci, True: ri < ci}
    decays, kbs, rhss, qgs, kdecs, glasts, sbs = [], [], [], [], [], [], []
    for q, k, v, gc, gct, beta, state, upper in probs:
        decays.append(jnp.exp(jnp.where(incl[upper], gc - gct, NEG)))
        kf = k.astype(F32)
        kb = kf * beta
        eg = jnp.exp(gc)
        g_last = gc[0:1, :] if upper else gc[c - 1:c, :]
        kbs.append(kb.astype(BF16))
        rhss.append(jnp.concatenate([v.astype(F32) * beta, kb * eg], axis=1).astype(BF16))
        qgs.append((q.astype(F32) * eg).astype(BF16))
        kdecs.append((kf * jnp.exp(g_last - gc)).astype(BF16))
        glasts.append(g_last)
        sbs.append(state.astype(BF16))
    kks = [lax.dot_general(kb, p[1], nt_dims, preferred_element_type=F32) for kb, p in zip(kbs, probs)]
    qks = [lax.dot_general(p[0], p[1], nt_dims, preferred_element_type=F32) for p in probs]
    lowers = [jnp.where(strict[p[7]], kk * dec, 0.0) for kk, dec, p in zip(kks, decays, probs)]
    intras = [(qk * dec).astype(BF16) for qk, dec in zip(qks, decays)]
    tinvs = _tri_inverse_many(lowers, ri, ci)
    uws = [jnp.dot(t.astype(BF16), r, preferred_element_type=F32) for t, r in zip(tinvs, rhss)]
    wss = [jnp.dot(uw[:, dv:].astype(BF16), sb, preferred_element_type=F32) for uw, sb in zip(uws, sbs)]
    o1s = [jnp.dot(qg, sb, preferred_element_type=F32) for qg, sb in zip(qgs, sbs)]
    v_news = [(uw[:, :dv] - ws).astype(BF16) for uw, ws in zip(uws, wss)]
    o2s = [jnp.dot(a, vn, preferred_element_type=F32) for a, vn in zip(intras, v_news)]
    upds = [lax.dot_general(kd, vn, tn_dims, preferred_element_type=F32) for kd, vn in zip(kdecs, v_news)]
    outs = [o1 + o2 for o1, o2 in zip(o1s, o2s)]
    states = [p[6] * jnp.exp(gl) + upd for p, gl, upd in zip(probs, glasts, upds)]
    return outs, states


def _gdn_scan_kernel(qf_ref, kf_ref, vf_ref, gf_ref, qb_ref, kb_ref, vb_ref, gbk_ref,
                     of_ref, ob_ref, sf, sb, *, dk):
    s = pl.program_id(1)

    @pl.when(s == 0)
    def _():
        sf[...] = jnp.zeros_like(sf)
        sb[...] = jnp.zeros_like(sb)

    c = qf_ref.shape[0]
    ri = lax.broadcasted_iota(I32, (c, c), 0)
    ci = lax.broadcasted_iota(I32, (c, c), 1)
    tri_l = (ri >= ci).astype(F32)
    tri_u = (ri <= ci).astype(F32)
    probs, sinks = [], []
    for upper, (q_ref, k_ref, v_ref, g_ref, o_ref, st) in enumerate(
            ((qf_ref, kf_ref, vf_ref, gf_ref, of_ref, sf), (qb_ref, kb_ref, vb_ref, gbk_ref, ob_ref, sb))):
        gbv = g_ref[...]
        csum = jnp.dot(tri_u if upper else tri_l, gbv, precision=HIGHEST, preferred_element_type=F32)
        csum_t = csum.T
        base = 16 * upper
        for hh in range(HEADS):
            sl = slice(hh * dk, (hh + 1) * dk)
            probs.append((q_ref[:, sl], k_ref[:, sl], v_ref[:, sl],
                          csum[:, base + hh:base + hh + 1], csum_t[base + hh:base + hh + 1, :],
                          gbv[:, base + 8 + hh:base + 9 + hh], st[hh], bool(upper)))
            sinks.append((o_ref, st, hh, sl))
    outs, states = _gdn_chunks(probs, ri, ci)
    for (o_ref, st, hh, sl), o, new_state in zip(sinks, outs, states):
        o_ref[:, sl] = o.astype(BF16)
        st[hh] = new_state


def _gdn_mixer(xall, mod_i, norm_gain, w_in, conv_w, a_log, dt_bias, dims):
    bsz, ctx_len, seq, d, tm = dims
    ltot = ctx_len + seq
    nt, nct = ltot // tm, ctx_len // tm
    n_tiles = bsz * nt
    rows = n_tiles * tm
    dk = d // HEADS
    n_in = w_in.shape[1]
    n_pad = -(-n_in // LANES) * LANES
    wp = jnp.pad(w_in, ((0, 0), (0, n_pad - n_in))).astype(BF16)
    zeros8 = jnp.zeros((2, HEADS), F32)
    lay = lambda t: jnp.pad(jnp.concatenate([t, zeros8], axis=1).reshape(1, 4 * HEADS),
                            ((0, 0), (0, LANES - 4 * HEADS)))
    mrow = _mod_row_map(nt, nct, bsz)
    hb = tm // SUBLANES
    last_hblk = rows // SUBLANES - 1
    const = lambda j: (0, 0)
    q, k, v, z, gb = pl.pallas_call(
        functools.partial(_gdn_proj_kernel, d=d, nt=nt, nct=nct, tm=tm, dk=dk),
        grid=(n_tiles,),
        in_specs=[pl.BlockSpec((SUBLANES, d), lambda j: (jnp.maximum(j * hb - 1, 0), 0)),
                  pl.BlockSpec((tm, d), lambda j: (j, 0)),
                  pl.BlockSpec((SUBLANES, d), lambda j: (jnp.minimum((j + 1) * hb, last_hblk), 0)),
                  pl.BlockSpec((None, 6, d), lambda j: (mrow(j), 0, 0)),
                  pl.BlockSpec((1, d), const),
                  pl.BlockSpec((d, n_pad), const),
                  pl.BlockSpec((GDN_CONV, 3 * d), const),
                  pl.BlockSpec((1, LANES), const),
                  pl.BlockSpec((1, LANES), const)],
        out_specs=[pl.BlockSpec((tm, d), lambda j: (j, 0))] * 4 + [pl.BlockSpec((tm, LANES), lambda j: (j, 0))],
        out_shape=[jax.ShapeDtypeStruct((rows, d), BF16)] * 4 + [jax.ShapeDtypeStruct((rows, LANES), F32)],
        scratch_shapes=[pltpu.VMEM((tm + 2 * SUBLANES, n_pad), F32)],
        compiler_params=_cparams("parallel"),
        name="gdn_proj",
    )(xall, xall, xall, mod_i, norm_gain.reshape(1, d), wp, conv_w, lay(a_log), lay(dt_bias))
    c = GDN_CHUNK
    ncl, ncc = ltot // c, ctx_len // c

    def fwd(b, s):
        return (b * ncl + s, 0)

    def bwd(b, s):
        return (b * ncl + jnp.where(s < ncc, ncc - 1 - s, ncl + ncc - 1 - s), 0)

    blk = lambda m: pl.BlockSpec((c, d), m)
    gblk = lambda m: pl.BlockSpec((c, LANES), m)
    o_f, o_b = pl.pallas_call(
        functools.partial(_gdn_scan_kernel, dk=dk),
        grid=(bsz, ncl),
        in_specs=[blk(fwd), blk(fwd), blk(fwd), gblk(fwd), blk(bwd), blk(bwd), blk(bwd), gblk(bwd)],
        out_specs=[blk(fwd), blk(bwd)],
        out_shape=[jax.ShapeDtypeStruct((rows, d), BF16)] * 2,
        scratch_shapes=[pltpu.VMEM((HEADS, dk, dk), F32), pltpu.VMEM((HEADS, dk, dk), F32)],
        compiler_params=_cparams("parallel", "arbitrary"),
        name="gdn_scan",
    )(q, k, v, gb, q, k, v, gb)
    return (o_f, o_b, z)


def _split_bf16(x):
    hi = x.astype(BF16)
    lo = (x - hi.astype(F32)).astype(BF16)
    return hi, lo


def _post_kernel(*refs, kind, d, tm, dk):
    if kind == 0:
        of_ref, ob_ref, z_ref, og_ref = refs[:4]
        refs = refs[4:]
    else:
        o_ref = refs[0]
        refs = refs[1:]
    (x_ref, mod_ref, wo_ref, g_ref, wrh_ref, wrl_ref, br_ref,
     xo_ref, h_ref, ids_ref, gate_ref, cnt_ref, base) = refs
    j = pl.program_id(0)

    @pl.when(j == 0)
    def _():
        base[...] = jnp.zeros_like(base)

    if kind == 0:
        parts = []
        for hh in range(d // dk):
            sl = slice(hh * dk, (hh + 1) * dk)
            o = of_ref[:, sl].astype(F32) + ob_ref[:, sl].astype(F32)
            o = o * lax.rsqrt(jnp.mean(o * o, axis=-1, keepdims=True) + EPS) * og_ref[...]
            parts.append((o * _silu(z_ref[:, sl].astype(F32))).astype(BF16))
        o_in = jnp.concatenate(parts, axis=1)
    else:
        o_in = o_ref[...]
    mod = mod_ref[...]
    x = x_ref[...] + mod[2:3, :] * jnp.dot(o_in, wo_ref[...], preferred_element_type=F32)
    xo_ref[...] = x
    h = _norm_mod(x, g_ref[...], mod[3:4, :], mod[4:5, :])
    h_ref[...] = h
    hi, lo = _split_bf16(h)
    logits = (jnp.dot(hi, wrh_ref[...], preferred_element_type=F32)
              + jnp.dot(lo, wrh_ref[...], preferred_element_type=F32)
              + jnp.dot(hi, wrl_ref[...], preferred_element_type=F32)) + br_ref[...]
    lane = lax.broadcasted_iota(I32, logits.shape, 1)
    big = jnp.int32(1 << 20)
    is_g = lane < MOE_GROUPS
    gl = jnp.where(is_g, logits, NEG)
    gmax = jnp.max(gl, axis=-1, keepdims=True)
    gsel = jnp.min(jnp.where(gl == gmax, lane, big), axis=-1, keepdims=True)
    p_group = 1.0 / jnp.sum(jnp.where(is_g, jnp.exp(gl - gmax), 0.0), axis=-1, keepdims=True)
    in_grp = jnp.logical_and(lane >= MOE_GROUPS + gsel * MOE_PER_GROUP,
                             lane < MOE_GROUPS + (gsel + 1) * MOE_PER_GROUP)
    el = jnp.where(in_grp, logits, NEG)
    v0 = jnp.max(el, axis=-1, keepdims=True)
    i0 = jnp.min(jnp.where(el == v0, lane, big), axis=-1, keepdims=True)
    el1 = jnp.where(lane == i0, NEG, el)
    v1 = jnp.max(el1, axis=-1, keepdims=True)
    i1 = jnp.min(jnp.where(el1 == v1, lane, big), axis=-1, keepdims=True)
    e1 = jnp.exp(v1 - v0)
    w0 = p_group / (1.0 + e1)
    w1 = p_group * e1 / (1.0 + e1)
    oh0 = lane == i0
    oh1 = lane == i1
    onehot = jnp.where(jnp.logical_or(oh0, oh1), 1.0, 0.0)
    ri = lax.broadcasted_iota(I32, (tm, tm), 0)
    ci = lax.broadcasted_iota(I32, (tm, tm), 1)
    tri = jnp.where(ri > ci, 1.0, 0.0).astype(BF16)
    before = base[...] + jnp.dot(tri, onehot.astype(BF16), preferred_element_type=F32)
    r0 = jnp.sum(jnp.where(oh0, before, 0.0), axis=-1, keepdims=True)
    r1 = jnp.sum(jnp.where(oh1, before, 0.0), axis=-1, keepdims=True)
    new_base = base[...] + jnp.sum(onehot, axis=0, keepdims=True)
    base[...] = new_base
    cnt_ref[...] = new_base
    e0 = i0 - MOE_GROUPS
    e1i = i1 - MOE_GROUPS
    ids_ref[...] = jnp.where(lane == 0, e0, jnp.where(lane == 1, e1i, jnp.where(
        lane == 2, r0.astype(I32), jnp.where(lane == 3, r1.astype(I32), 0))))
    gate_ref[...] = jnp.where(lane == 0, w0, jnp.where(lane == 1, w1, 0.0))


def _post_mixer(kind, mixer_out, xall, mod_i, w_out, ffn_gain, w_group, b_group, w_expert, b_expert,
                dims, o_gain=None):
    bsz, ctx_len, seq, d, tm = dims
    ltot = ctx_len + seq
    nt, nct = ltot // tm, ctx_len // tm
    n_tiles = bsz * nt
    rows = n_tiles * tm
    dk = d // HEADS
    mrow = _mod_row_map(nt, nct, bsz)
    wr = jnp.pad(jnp.concatenate([w_group, w_expert], axis=1),
                 ((0, 0), (0, LANES - MOE_GROUPS - MOE_EXPERTS)))
    wr_hi = wr.astype(BF16)
    wr_lo = (wr - wr_hi.astype(F32)).astype(BF16)
    br = jnp.pad(jnp.concatenate([b_group, b_expert]), (0, LANES - MOE_GROUPS - MOE_EXPERTS)).reshape(1, LANES)
    const = lambda j: (0, 0)
    row = lambda j: (j, 0)
    lead_specs = [pl.BlockSpec((tm, d), row)] * len(mixer_out)
    lead_args = list(mixer_out)
    if kind == 0:
        lead_specs.append(pl.BlockSpec((1, dk), const))
        lead_args.append(o_gain.reshape(1, dk))
    n_lead = len(lead_args)
    outs = pl.pallas_call(
        functools.partial(_post_kernel, kind=kind, d=d, tm=tm, dk=dk),
        grid=(n_tiles,),
        in_specs=lead_specs + [pl.BlockSpec((tm, d), row),
                               pl.BlockSpec((None, 6, d), lambda j: (mrow(j), 0, 0)),
                               pl.BlockSpec((w_out.shape[0], d), const),
                               pl.BlockSpec((1, d), const),
                               pl.BlockSpec((d, LANES), const),
                               pl.BlockSpec((d, LANES), const),
                               pl.BlockSpec((1, LANES), const)],
        out_specs=[pl.BlockSpec((tm, d), row), pl.BlockSpec((tm, d), row),
                   pl.BlockSpec((tm, LANES), row), pl.BlockSpec((tm, LANES), row),
                   pl.BlockSpec((1, LANES), const)],
        out_shape=[jax.ShapeDtypeStruct((rows, d), F32), jax.ShapeDtypeStruct((rows, d), F32),
                   jax.ShapeDtypeStruct((rows, LANES), I32), jax.ShapeDtypeStruct((rows, LANES), F32),
                   jax.ShapeDtypeStruct((1, LANES), F32)],
        scratch_shapes=[pltpu.VMEM((1, LANES), F32)],
        input_output_aliases={n_lead: 0},
        compiler_params=_cparams("arbitrary"),
        name="post_mixer",
    )(*lead_args, xall, mod_i, w_out.astype(BF16), ffn_gain.reshape(1, d), wr_hi, wr_lo, br)
    return outs


def _dispatch_kernel(zlo_ref, zhi_ref, nu_ref, dest_ref, h_ref, xs_ref, zblk, sem, zsem, *, tm, blk, n_blocks):
    j = pl.program_id(0)

    def row_copy(src, dst_row, s):
        return pltpu.make_async_copy(src, xs_ref.at[pl.ds(dst_row, 1)], s)

    def blk_copy(bi):
        return pltpu.make_async_copy(zblk, xs_ref.at[pl.ds(pl.multiple_of(bi * blk, blk), blk)], zsem)

    @pl.when(j == 0)
    def _():
        zblk[...] = jnp.zeros_like(zblk)

        def per_expert(e, carry):
            lo, hi = zlo_ref[e], zhi_ref[e]

            def start(r, c):
                row_copy(zblk.at[pl.ds(0, 1)], r, zsem).start()
                return c

            def wait(r, c):
                row_copy(zblk.at[pl.ds(0, 1)], r, zsem).wait()
                return c

            lax.fori_loop(lo, hi, start, 0)
            lax.fori_loop(lo, hi, wait, 0)
            return carry

        lax.fori_loop(0, MOE_EXPERTS, per_expert, 0)

        def tail_start(bi, c):
            blk_copy(bi).start()
            return c

        def tail_wait(bi, c):
            blk_copy(bi).wait()
            return c

        lax.fori_loop(nu_ref[0], n_blocks, tail_start, 0)
        lax.fori_loop(nu_ref[0], n_blocks, tail_wait, 0)

    def body(r, c):
        for kk in range(MOE_TOP_K):
            row_copy(h_ref.at[pl.ds(r, 1)], dest_ref[0, MOE_TOP_K * r + kk], sem).start()
        return c

    lax.fori_loop(0, tm, body, 0, unroll=8)
    for _ in range(MOE_TOP_K):
        pltpu.make_async_copy(h_ref, xs_ref.at[pl.ds(0, tm)], sem).wait()


def _expert_kernel(be_ref, nu_ref, x_ref, w1_ref, w3_ref, w2_ref, y_ref):
    j = pl.program_id(0)

    @pl.when(j < nu_ref[0])
    def _():
        x = x_ref[...].astype(BF16)
        a = jnp.dot(x, w1_ref[...], preferred_element_type=F32)
        b = jnp.dot(x, w3_ref[...], preferred_element_type=F32)
        y_ref[...] = jnp.dot((_silu(a) * b).astype(BF16), w2_ref[...], preferred_element_type=F32)

    @pl.when(j >= nu_ref[0])
    def _():
        y_ref[...] = jnp.zeros_like(y_ref)


def _combine_kernel(dest_ref, x_ref, gate_ref, mod_ref, fg_ref, yb_ref, xo_ref, ybuf, sem, *, tm, final):
    def body(r, c):
        for kk in range(MOE_TOP_K):
            pltpu.make_async_copy(yb_ref.at[pl.ds(dest_ref[0, MOE_TOP_K * r + kk], 1)],
                                  ybuf.at[kk, pl.ds(r, 1)], sem).start()
        return c

    lax.fori_loop(0, tm, body, 0, unroll=8)
    for kk in range(MOE_TOP_K):
        pltpu.make_async_copy(yb_ref.at[pl.ds(0, tm)], ybuf.at[kk], sem).wait()
    gate = gate_ref[...]
    y = ybuf[0] * gate[:, 0:1] + ybuf[1] * gate[:, 1:2]
    x = x_ref[...] + mod_ref[5:6, :] * y
    if final:
        x = x * lax.rsqrt(jnp.mean(x * x, axis=-1, keepdims=True) + EPS) * fg_ref[...]
    xo_ref[...] = x


def _moe(xall, h, ids, gates, counts, mod_i, w1, w3, w2, final_gain, dims, final):
    bsz, ctx_len, seq, d, tm = dims
    ltot = ctx_len + seq
    nt, nct = ltot // tm, ctx_len // tm
    n_tiles = bsz * nt
    rows = n_tiles * tm
    e = MOE_EXPERTS
    blk = MOE_BLOCK
    a = rows * MOE_TOP_K
    n_blocks = -(-(a + e * (blk - 1)) // blk)
    cnt = counts[0, MOE_GROUPS:MOE_GROUPS + e].astype(I32)
    padded = (cnt + blk - 1) // blk * blk
    pad_end = jnp.cumsum(padded)
    pad_start = pad_end - padded
    dest = (pad_start[ids[:, :MOE_TOP_K]] + ids[:, MOE_TOP_K:2 * MOE_TOP_K]).astype(I32)
    dest = dest.reshape(n_tiles, 1, tm * MOE_TOP_K)
    n_used = (pad_end[-1] // blk).astype(I32).reshape(1)
    blk_expert = jnp.minimum(jnp.searchsorted(pad_end, jnp.arange(n_blocks, dtype=I32) * blk, side='right'),
                             e - 1).astype(I32)
    smem_dest = pl.BlockSpec((None, 1, tm * MOE_TOP_K), lambda j, *_: (j, 0, 0), memory_space=pltpu.SMEM)
    xs = pl.pallas_call(
        functools.partial(_dispatch_kernel, tm=tm, blk=blk, n_blocks=n_blocks),
        grid_spec=pltpu.PrefetchScalarGridSpec(
            num_scalar_prefetch=3, grid=(n_tiles,),
            in_specs=[smem_dest, pl.BlockSpec((tm, d), lambda j, *_: (j, 0))],
            out_specs=pl.BlockSpec(memory_space=pl.ANY),
            scratch_shapes=[pltpu.VMEM((blk, d), F32), pltpu.SemaphoreType.DMA,
                            pltpu.SemaphoreType.DMA]),
        out_shape=jax.ShapeDtypeStruct((n_blocks * blk, d), F32),
        compiler_params=_cparams("arbitrary"),
        name="moe_dispatch",
    )((pad_start + cnt).astype(I32), pad_end.astype(I32), n_used, dest, h)

    def xmap(j, be, nu):
        return (jnp.minimum(j, nu[0] - 1), 0)

    def wmap(j, be, nu):
        return (be[jnp.minimum(j, nu[0] - 1)], 0, 0)

    f = w1.shape[-1]
    yb = pl.pallas_call(
        _expert_kernel,
        grid_spec=pltpu.PrefetchScalarGridSpec(
            num_scalar_prefetch=2, grid=(n_blocks,),
            in_specs=[pl.BlockSpec((blk, d), xmap),
                      pl.BlockSpec((None, d, f), wmap),
                      pl.BlockSpec((None, d, f), wmap),
                      pl.BlockSpec((None, f, d), wmap)],
            out_specs=pl.BlockSpec((blk, d), lambda j, be, nu: (j, 0))),
        out_shape=jax.ShapeDtypeStruct((n_blocks * blk, d), F32),
        compiler_params=_cparams("arbitrary"),
        name="moe_experts",
    )(blk_expert, n_used, xs, w1.astype(BF16), w3.astype(BF16), w2.astype(BF16))

    mrow = _mod_row_map(nt, nct, bsz)
    out = pl.pallas_call(
        functools.partial(_combine_kernel, tm=tm, final=final),
        grid=(n_tiles,),
        in_specs=[pl.BlockSpec((None, 1, tm * MOE_TOP_K), lambda j: (j, 0, 0), memory_space=pltpu.SMEM),
                  pl.BlockSpec((tm, d), lambda j: (j, 0)),
                  pl.BlockSpec((tm, LANES), lambda j: (j, 0)),
                  pl.BlockSpec((None, 6, d), lambda j: (mrow(j), 0, 0)),
                  pl.BlockSpec((1, d), lambda j: (0, 0)),
                  pl.BlockSpec(memory_space=pl.ANY)],
        out_specs=pl.BlockSpec((tm, d), lambda j: (j, 0)),
        out_shape=jax.ShapeDtypeStruct((rows, d), F32),
        scratch_shapes=[pltpu.VMEM((MOE_TOP_K, tm, d), F32), pltpu.SemaphoreType.DMA],
        input_output_aliases={1: 0},
        compiler_params=_cparams("arbitrary"),
        name="moe_combine",
    )(dest, xall, gates, mod_i, final_gain.reshape(1, d), yb)
    return out


def kernel(x, c, ctx, c_ctx, w_mod, b_mod, norm_mix, norm_ffn, gdn_w_in, gdn_conv, gdn_a_log, gdn_dt_bias, gdn_norm, gdn_w_out, diff_w_qkv, diff_lambda, diff_norm, diff_w_out, mla_w_down, mla_q_norm, mla_kv_norm, mla_w_uq, mla_w_ukv, mla_w_out, moe_w_group, moe_b_group, moe_w_expert, moe_b_expert, moe_w1, moe_w3, moe_w2, final_norm):
    bsz, seq, d = x.shape
    ctx_len = ctx.shape[1]
    depth = w_mod.shape[0]
    tm = _row_tile(ctx_len)
    assert d % LANES == 0 and d // HEADS == LANES
    assert ctx_len % tm == 0 and seq % tm == 0 and ctx_len % GDN_CHUNK == 0 and seq % GDN_CHUNK == 0
    dims = (bsz, ctx_len, seq, d, tm)
    ltot = ctx_len + seq
    xall = jnp.concatenate([ctx, x], axis=1).reshape(bsz * ltot, d)
    mod = _mod_vectors(c, c_ctx, w_mod, b_mod)
    tables = _rope_tables(seq, ctx_len)
    for i in range(depth):
        kind, j = i % N_MIXERS, i // N_MIXERS
        if kind == 0:
            mixer_out = _gdn_mixer(xall, mod[i], norm_mix[i], gdn_w_in[j], gdn_conv[j], gdn_a_log[j],
                                   gdn_dt_bias[j], dims)
            w_out, o_gain = gdn_w_out[j], gdn_norm[j]
        elif kind == 1:
            lam_init = 0.8 - 0.6 * math.exp(-0.3 * i)
            mixer_out = _diff_mixer(xall, mod[i], norm_mix[i], diff_w_qkv[j], diff_lambda[j], diff_norm[j],
                                    dims, tables, lam_init)
            w_out, o_gain = diff_w_out[j], None
        else:
            mixer_out = _mla_mixer(xall, mod[i], norm_mix[i], mla_w_down[j], mla_q_norm[j], mla_kv_norm[j],
                                   mla_w_uq[j], mla_w_ukv[j], dims, tables)
            w_out, o_gain = mla_w_out[j], None
        xall, h, ids, gates, counts = _post_mixer(kind, mixer_out, xall, mod[i], w_out, norm_ffn[i],
                                                  moe_w_group[i], moe_b_group[i], moe_w_expert[i],
                                                  moe_b_expert[i], dims, o_gain)
        xall = _moe(xall, h, ids, gates, counts, mod[i], moe_w1[i], moe_w3[i], moe_w2[i], final_norm, dims,
                    final=(i == depth - 1))
    return xall.reshape(bsz, ltot, d)[:, ctx_len:]
```

```python
import functools
import math

import jax
import jax.numpy as jnp
from jax import lax
from jax.experimental import pallas as pl
from jax.experimental.pallas import tpu as pltpu

F32 = jnp.float32
BF16 = jnp.bfloat16
I32 = jnp.int32
HIGHEST = lax.Precision.HIGHEST

LANES = 128
SUBLANES = 8
VMEM_LIMIT = 56 * 1024 * 1024

EPS = 1e-6
GRID_W = 64
ROPE_THETA = 10000.0
N_MIXERS = 3
HEADS = 8
GDN_CONV = 5
GDN_CHUNK = 64
DIFF_SUBLN_EPS = 1e-5
MLA_NOPE = 128
MLA_ROPE = 64
MLA_V = 128
MOE_GROUPS = 4
MOE_PER_GROUP = 8
MOE_EXPERTS = MOE_GROUPS * MOE_PER_GROUP
MOE_TOP_K = 2
MOE_BLOCK = 256
NEG = -1e30


def _cparams(*sem):
    return pltpu.CompilerParams(dimension_semantics=sem, vmem_limit_bytes=VMEM_LIMIT)


def _row_tile(ctx_len):
    return 256 if ctx_len % 256 == 0 else 128


def _mod_row_map(nt, nct, bsz):
    def f(j):
        return jnp.where(j % nt < nct, bsz, j // nt)
    return f


def _norm_mod(x, gain, shift, scale, eps=EPS):
    var = jnp.mean(x * x, axis=-1, keepdims=True)
    y = x * lax.rsqrt(var + eps) * gain
    return y * (1.0 + scale) + shift


def _silu(x):
    return x * jax.nn.sigmoid(x)


def _mod_kernel(c_ref, w_ref, b_ref, o_ref):
    s = _silu(c_ref[...])
    o_ref[...] = jnp.dot(s, w_ref[...], precision=HIGHEST, preferred_element_type=F32) + b_ref[...]


def _mod_vectors(c, c_ctx, w_mod, b_mod):
    depth, d, n = w_mod.shape
    bsz = c.shape[0]
    rows = -(-(bsz + 1) // SUBLANES) * SUBLANES
    cc = jnp.zeros((rows, d), F32).at[:bsz].set(c).at[bsz].set(c_ctx)
    tn = 512
    out = pl.pallas_call(
        _mod_kernel,
        grid=(depth, n // tn),
        in_specs=[pl.BlockSpec((rows, d), lambda i, j: (0, 0)),
                  pl.BlockSpec((None, d, tn), lambda i, j: (i, 0, j)),
                  pl.BlockSpec((None, 1, tn), lambda i, j: (i, 0, j))],
        out_specs=pl.BlockSpec((None, rows, tn), lambda i, j: (i, 0, j)),
        out_shape=jax.ShapeDtypeStruct((depth, rows, n), F32),
        compiler_params=_cparams("parallel", "parallel"),
        name="mod_vectors",
    )(cc, w_mod, b_mod.reshape(depth, 1, n))
    return out.reshape(depth, rows, 6, d)


def _rope_tables(seq, ctx_len):
    quarter = 16
    inv_freq = ROPE_THETA ** (-jnp.arange(quarter, dtype=F32) / quarter)
    t = jnp.arange(seq)
    row = (t // GRID_W).astype(F32)[:, None] * inv_freq
    col = (t % GRID_W).astype(F32)[:, None] * inv_freq
    cos = jnp.concatenate([jnp.cos(row), jnp.cos(row), jnp.cos(col), jnp.cos(col)], axis=1)
    sin = jnp.concatenate([-jnp.sin(row), jnp.sin(row), -jnp.sin(col), jnp.sin(col)], axis=1)
    cos = jnp.concatenate([jnp.ones((ctx_len, 64), F32), cos], axis=0)
    sin = jnp.concatenate([jnp.zeros((ctx_len, 64), F32), sin], axis=0)
    return jnp.tile(cos, (1, 2)), jnp.tile(sin, (1, 2))


def _rope128(blk, cos, sin):
    lane = lax.broadcasted_iota(I32, blk.shape, 1)
    first = (lane % 32) < 16
    partner = jnp.where(first, pltpu.roll(blk, LANES - 16, 1), pltpu.roll(blk, 16, 1))
    return blk * cos + partner * sin


def _diff_proj_kernel(x_ref, mod_ref, g_ref, w_ref, cos_ref, sin_ref, o_ref, vt_ref, *, d, q_scale):
    h = _norm_mod(x_ref[...], g_ref[...], mod_ref[0:1, :], mod_ref[1:2, :])
    p = jnp.dot(h.astype(BF16), w_ref[...], preferred_element_type=F32)
    cos = cos_ref[...]
    sin = sin_ref[...]
    nqk = 2 * d // LANES
    for cb in range(nqk):
        r = _rope128(p[:, cb * LANES:(cb + 1) * LANES], cos, sin)
        if cb < nqk // 2:
            r = r * q_scale
        o_ref[:, cb * LANES:(cb + 1) * LANES] = r.astype(BF16)
    vt_ref[...] = p[:, 2 * d:].T.astype(BF16)


def _pick_tk(n):
    for cand in (768, 512, 384, 256, 128):
        if n % cand == 0:
            return cand
    raise ValueError(n)


def _flash_t(streams, k_ref, vt_ref, m_ref, l_ref, acc_ref, nsteps, tk):
    for s in range(len(streams)):
        m_ref[s] = jnp.full(m_ref.shape[1:], NEG, F32)
        l_ref[s] = jnp.zeros(l_ref.shape[1:], F32)
        acc_ref[s] = jnp.zeros(acc_ref.shape[1:], F32)
    nt_dims = (((1,), (1,)), ((), ()))

    def scores(i):
        return [lax.dot_general(k_ref[i * tk:(i + 1) * tk, kc], q, nt_dims, preferred_element_type=F32)
                for q, kc, _ in streams]

    sts = scores(0)
    for i in range(nsteps):
        nxt = scores(i + 1) if i + 1 < nsteps else None
        alphas, ps = [], []
        for s, st in enumerate(sts):
            m_old = m_ref[s]
            m_new = jnp.maximum(m_old, jnp.max(st, axis=0, keepdims=True))
            alpha = jnp.exp2(m_old - m_new)
            p = jnp.exp2(st - m_new)
            l_ref[s] = alpha * l_ref[s] + jnp.sum(p, axis=0, keepdims=True)
            m_ref[s] = m_new
            alphas.append(alpha)
            ps.append(p.astype(BF16))
        pvs = [jnp.dot(vt_ref[vr, i * tk:(i + 1) * tk], p, preferred_element_type=F32)
               for (_, _, vr), p in zip(streams, ps)]
        for s in range(len(streams)):
            acc_ref[s] = alphas[s] * acc_ref[s] + pvs[s]
        sts = nxt


def _flash_ctx_or_all(streams, k_ref, vt_ref, m_ref, l_ref, acc_ref, nct, ctx_len, ltot):
    i = pl.program_id(2)
    tk_c, tk_l = _pick_tk(ctx_len), _pick_tk(ltot)

    @pl.when(i < nct)
    def _():
        _flash_t(streams, k_ref, vt_ref, m_ref, l_ref, acc_ref, ctx_len // tk_c, tk_c)

    @pl.when(i >= nct)
    def _():
        _flash_t(streams, k_ref, vt_ref, m_ref, l_ref, acc_ref, ltot // tk_l, tk_l)


ATTN_HEADS_PER_STEP = 2


def _diff_attn_kernel(lam_ref, gain_ref, q_ref, k_ref, vt_ref, o_ref, m_ref, l_ref, acc_ref, *, nct,
                      ctx_len, ltot, lam_init, hw):
    streams = []
    for g in range(ATTN_HEADS_PER_STEP):
        cols = slice(g * hw, (g + 1) * hw)
        q = q_ref[:, cols]
        lane = lax.broadcasted_iota(I32, q.shape, 1)
        zero = jnp.zeros_like(q)
        streams.append((jnp.where(lane < hw // 2, q, zero), cols, cols))
        streams.append((jnp.where(lane >= hw // 2, q, zero), cols, cols))
    _flash_ctx_or_all(streams, k_ref, vt_ref, m_ref, l_ref, acc_ref, nct, ctx_len, ltot)
    lv = lam_ref[...]
    lam = (jnp.exp(jnp.sum(lv[0:1] * lv[1:2], keepdims=True))
           - jnp.exp(jnp.sum(lv[2:3] * lv[3:4], keepdims=True)) + lam_init)
    for g in range(ATTN_HEADS_PER_STEP):
        o = acc_ref[2 * g] / l_ref[2 * g] - lam * (acc_ref[2 * g + 1] / l_ref[2 * g + 1])
        var = jnp.mean(o * o, axis=0, keepdims=True)
        o = o * lax.rsqrt(var + DIFF_SUBLN_EPS) * gain_ref[...] * (1.0 - lam_init)
        o_ref[:, g * hw:(g + 1) * hw] = o.T.astype(BF16)


def _diff_mixer(xall, mod_i, norm_gain, w_qkv, lam_vec, sub_gain, dims, tables, lam_init):
    bsz, ctx_len, seq, d, tm = dims
    ltot = ctx_len + seq
    nt, nct = ltot // tm, ctx_len // tm
    n_tiles = bsz * nt
    dh = d // HEADS // 2
    cos, sin = tables
    mrow = _mod_row_map(nt, nct, bsz)
    rows = n_tiles * tm
    qk, vt = pl.pallas_call(
        functools.partial(_diff_proj_kernel, d=d, q_scale=dh ** -0.5 * math.log2(math.e)),
        grid=(n_tiles,),
        in_specs=[pl.BlockSpec((tm, d), lambda j: (j, 0)),
                  pl.BlockSpec((None, 6, d), lambda j: (mrow(j), 0, 0)),
                  pl.BlockSpec((1, d), lambda j: (0, 0)),
                  pl.BlockSpec((d, 3 * d), lambda j: (0, 0)),
                  pl.BlockSpec((tm, LANES), lambda j: (j % nt, 0)),
                  pl.BlockSpec((tm, LANES), lambda j: (j % nt, 0))],
        out_specs=[pl.BlockSpec((tm, 2 * d), lambda j: (j, 0)),
                   pl.BlockSpec((d, tm), lambda j: (0, j))],
        out_shape=[jax.ShapeDtypeStruct((rows, 2 * d), BF16), jax.ShapeDtypeStruct((d, rows), BF16)],
        compiler_params=_cparams("parallel"),
        name="diff_proj",
    )(xall, mod_i, norm_gain.reshape(1, d), w_qkv.astype(BF16), cos, sin)
    hw = 2 * dh
    hps = ATTN_HEADS_PER_STEP
    hg = HEADS // hps
    o = pl.pallas_call(
        functools.partial(_diff_attn_kernel, nct=nct, ctx_len=ctx_len, ltot=ltot, lam_init=lam_init, hw=hw),
        grid=(bsz, hg, nt),
        in_specs=[pl.BlockSpec((4, dh), lambda b, h, i: (0, 0)),
                  pl.BlockSpec((hw, 1), lambda b, h, i: (0, 0)),
                  pl.BlockSpec((tm, hps * hw), lambda b, h, i: (b * nt + i, h)),
                  pl.BlockSpec((ltot, hps * hw), lambda b, h, i: (b, hg + h)),
                  pl.BlockSpec((hps * hw, ltot), lambda b, h, i: (h, b))],
        out_specs=pl.BlockSpec((tm, hps * hw), lambda b, h, i: (b * nt + i, h)),
        out_shape=jax.ShapeDtypeStruct((rows, d), BF16),
        scratch_shapes=[pltpu.VMEM((2 * hps, 1, tm), F32), pltpu.VMEM((2 * hps, 1, tm), F32),
                        pltpu.VMEM((2 * hps, hw, tm), F32)],
        compiler_params=_cparams("parallel", "parallel", "arbitrary"),
        name="diff_attn",
    )(lam_vec, sub_gain.reshape(hw, 1), qk, qk, vt)
    return (o,)


def _mla_proj_kernel(x_ref, mod_ref, g_ref, wd_ref, qg_ref, kvg_ref, wq_ref, wkv_ref, cos_ref, sin_ref,
                     q_ref, k_ref, v_ref, *, q_lora, kv_lora, scale):
    h = _norm_mod(x_ref[...], g_ref[...], mod_ref[0:1, :], mod_ref[1:2, :])
    p = jnp.dot(h.astype(BF16), wd_ref[...], preferred_element_type=F32)
    cq = p[:, :q_lora]
    cq = cq * lax.rsqrt(jnp.mean(cq * cq, axis=-1, keepdims=True) + EPS) * qg_ref[...]
    ckv = p[:, q_lora:q_lora + kv_lora]
    ckv = ckv * lax.rsqrt(jnp.mean(ckv * ckv, axis=-1, keepdims=True) + EPS) * kvg_ref[...]
    cos = cos_ref[...]
    sin = sin_ref[...]
    kr = _rope128(p[:, q_lora + kv_lora:], cos, sin).astype(BF16)
    q = jnp.dot(cq.astype(BF16), wq_ref[...], preferred_element_type=F32)
    kv = jnp.dot(ckv.astype(BF16), wkv_ref[...], preferred_element_type=F32)
    hq = MLA_NOPE + LANES
    for hh in range(HEADS):
        q_ref[:, hh * hq:hh * hq + MLA_NOPE] = (q[:, hh * hq:hh * hq + MLA_NOPE] * scale).astype(BF16)
        qr = _rope128(q[:, hh * hq + MLA_NOPE:(hh + 1) * hq], cos, sin) * scale
        q_ref[:, hh * hq + MLA_NOPE:(hh + 1) * hq] = qr.astype(BF16)
        k_ref[:, hh * hq:hh * hq + MLA_NOPE] = kv[:, hh * MLA_NOPE:(hh + 1) * MLA_NOPE].astype(BF16)
        k_ref[:, hh * hq + MLA_NOPE:(hh + 1) * hq] = kr
    v_ref[...] = kv[:, HEADS * MLA_NOPE:].T.astype(BF16)


def _mla_attn_kernel(q_ref, k_ref, vt_ref, o_ref, m_ref, l_ref, acc_ref, *, nct, ctx_len, ltot, hq):
    streams = [(q_ref[:, g * hq:(g + 1) * hq], slice(g * hq, (g + 1) * hq), slice(g * MLA_V, (g + 1) * MLA_V))
               for g in range(ATTN_HEADS_PER_STEP)]
    _flash_ctx_or_all(streams, k_ref, vt_ref, m_ref, l_ref, acc_ref, nct, ctx_len, ltot)
    for g in range(ATTN_HEADS_PER_STEP):
        o_ref[:, g * MLA_V:(g + 1) * MLA_V] = (acc_ref[g] / l_ref[g]).T.astype(BF16)


def _mla_mixer(xall, mod_i, norm_gain, w_down, q_gain, kv_gain, w_uq, w_ukv, dims, tables):
    bsz, ctx_len, seq, d, tm = dims
    ltot = ctx_len + seq
    nt, nct = ltot // tm, ctx_len // tm
    n_tiles = bsz * nt
    rows = n_tiles * tm
    q_lora, kv_lora = q_gain.shape[0], kv_gain.shape[0]
    cos, sin = tables
    mrow = _mod_row_map(nt, nct, bsz)
    hq = MLA_NOPE + LANES
    wd = jnp.pad(w_down, ((0, 0), (0, LANES - MLA_ROPE))).astype(BF16)
    nd = wd.shape[1]
    wq = jnp.pad(w_uq.reshape(q_lora, HEADS, MLA_NOPE + MLA_ROPE),
                 ((0, 0), (0, 0), (0, LANES - MLA_ROPE))).reshape(q_lora, HEADS * hq).astype(BF16)
    wkv = w_ukv.reshape(kv_lora, HEADS, MLA_NOPE + MLA_V)
    wkv = jnp.concatenate([wkv[:, :, :MLA_NOPE].reshape(kv_lora, HEADS * MLA_NOPE),
                           wkv[:, :, MLA_NOPE:].reshape(kv_lora, HEADS * MLA_V)], axis=1).astype(BF16)
    scale = (MLA_NOPE + MLA_ROPE) ** -0.5 * math.log2(math.e)
    const = lambda j: (0, 0)
    q, k, v = pl.pallas_call(
        functools.partial(_mla_proj_kernel, q_lora=q_lora, kv_lora=kv_lora, scale=scale),
        grid=(n_tiles,),
        in_specs=[pl.BlockSpec((tm, d), lambda j: (j, 0)),
                  pl.BlockSpec((None, 6, d), lambda j: (mrow(j), 0, 0)),
                  pl.BlockSpec((1, d), const),
                  pl.BlockSpec((d, nd), const),
                  pl.BlockSpec((1, q_lora), const),
                  pl.BlockSpec((1, kv_lora), const),
                  pl.BlockSpec((q_lora, HEADS * hq), const),
                  pl.BlockSpec((kv_lora, HEADS * (MLA_NOPE + MLA_V)), const),
                  pl.BlockSpec((tm, LANES), lambda j: (j % nt, 0)),
                  pl.BlockSpec((tm, LANES), lambda j: (j % nt, 0))],
        out_specs=[pl.BlockSpec((tm, HEADS * hq), lambda j: (j, 0)),
                   pl.BlockSpec((tm, HEADS * hq), lambda j: (j, 0)),
                   pl.BlockSpec((HEADS * MLA_V, tm), lambda j: (0, j))],
        out_shape=[jax.ShapeDtypeStruct((rows, HEADS * hq), BF16),
                   jax.ShapeDtypeStruct((rows, HEADS * hq), BF16),
                   jax.ShapeDtypeStruct((HEADS * MLA_V, rows), BF16)],
        compiler_params=_cparams("parallel"),
        name="mla_proj",
    )(xall, mod_i, norm_gain.reshape(1, d), wd, q_gain.reshape(1, q_lora), kv_gain.reshape(1, kv_lora),
      wq, wkv, cos, sin)
    hps = ATTN_HEADS_PER_STEP
    o = pl.pallas_call(
        functools.partial(_mla_attn_kernel, nct=nct, ctx_len=ctx_len, ltot=ltot, hq=hq),
        grid=(bsz, HEADS // hps, nt),
        in_specs=[pl.BlockSpec((tm, hps * hq), lambda b, h, i: (b * nt + i, h)),
                  pl.BlockSpec((ltot, hps * hq), lambda b, h, i: (b, h)),
                  pl.BlockSpec((hps * MLA_V, ltot), lambda b, h, i: (h, b))],
        out_specs=pl.BlockSpec((tm, hps * MLA_V), lambda b, h, i: (b * nt + i, h)),
        out_shape=jax.ShapeDtypeStruct((rows, HEADS * MLA_V), BF16),
        scratch_shapes=[pltpu.VMEM((hps, 1, tm), F32), pltpu.VMEM((hps, 1, tm), F32),
                        pltpu.VMEM((hps, MLA_V, tm), F32)],
        compiler_params=_cparams("parallel", "parallel", "arbitrary"),
        name="mla_attn",
    )(q, k, v)
    return (o,)


def _gdn_proj_kernel(xp_ref, x_ref, xn_ref, mod_ref, g_ref, w_ref, cw_ref, alog_ref, dtb_ref,
                     q_ref, k_ref, v_ref, z_ref, gb_ref, pbuf, *, d, nt, nct, tm, dk):
    j = pl.program_id(0)
    r = j % nt
    first = jnp.logical_or(r == 0, r == nct)
    last = jnp.logical_or(r == nct - 1, r == nt - 1)
    halo = SUBLANES
    xe = jnp.concatenate([xp_ref[...], x_ref[...], xn_ref[...]], axis=0)
    h = _norm_mod(xe, g_ref[...], mod_ref[0:1, :], mod_ref[1:2, :])
    rid = lax.broadcasted_iota(I32, (tm + 2 * halo, 1), 0)
    keep = jnp.logical_and(jnp.logical_or(rid >= halo, jnp.logical_not(first)),
                           jnp.logical_or(rid < tm + halo, jnp.logical_not(last)))
    h = jnp.where(keep, h, 0.0)
    pbuf[...] = jnp.dot(h.astype(BF16), w_ref[...], preferred_element_type=F32)
    half = GDN_CONV // 2

    def conv_block(c0):
        acc = None
        for t in range(GDN_CONV):
            term = pbuf[pl.ds(halo - half + t, tm), pl.ds(c0, LANES)] * cw_ref[t:t + 1, pl.ds(c0, LANES)]
            acc = term if acc is None else acc + term
        return _silu(acc)

    for hh in range(3 * d // LANES):
        c0 = hh * LANES
        blk = conv_block(c0)
        if hh < 2 * d // LANES:
            blk = blk * lax.rsqrt(jnp.sum(blk * blk, axis=-1, keepdims=True) + EPS)
        if hh < d // LANES:
            q_ref[:, c0:c0 + LANES] = (blk * dk ** -0.5).astype(BF16)
        elif hh < 2 * d // LANES:
            k_ref[:, c0 - d:c0 - d + LANES] = blk.astype(BF16)
        else:
            v_ref[:, c0 - 2 * d:c0 - 2 * d + LANES] = blk.astype(BF16)
    z_ref[...] = pbuf[halo:halo + tm, 3 * d:4 * d].astype(BF16)
    ab = pbuf[halo:halo + tm, 4 * d:4 * d + LANES]
    lane = lax.broadcasted_iota(I32, ab.shape, 1)
    is_a = (lane % 16) < 8
    g = -jnp.exp(alog_ref[...]) * jax.nn.softplus(ab + dtb_ref[...])
    gb_ref[...] = jnp.where(is_a, g, jax.nn.sigmoid(ab))


TRI_BASE = 16


def _mm(a, b):
    return jnp.dot(a.astype(BF16), b.astype(BF16), preferred_element_type=F32)


def _tri_inverse_many(lms, ri, ci):
    n = lms[0].shape[0]

    def same(s):
        shift = int(math.log2(s))
        return (ri >> shift) == (ci >> shift)

    eye = jnp.where(ri == ci, 1.0, 0.0)
    base = same(TRI_BASE)
    ms = [jnp.where(base, lm, 0.0) for lm in lms]
    ps = [eye - m for m in ms]
    for _ in range(int(math.log2(TRI_BASE)) - 1):
        ms = [_mm(m, m) for m in ms]
        ps = [p + _mm(p, m) for p, m in zip(ps, ms)]
    s = TRI_BASE
    while s < n:
        band = jnp.logical_and(same(2 * s), jnp.logical_not(same(s)))
        ts = [_mm(p, jnp.where(band, lm, 0.0)) for p, lm in zip(ps, lms)]
        ps = [p - _mm(t, p) for p, t in zip(ps, ts)]
        s *= 2
    return ps


def _gdn_chunks(probs, ri, ci):
    c = probs[0][0].shape[0]
    dv = probs[0][2].shape[1]
    nt_dims = (((1,), (1,)), ((), ()))
    tn_dims = (((0,), (0,)), ((), ()))
    incl = {False: ri >= ci, True: ri <= ci}
    strict = {False: ri > ci, True: ri < ci}
    decays, kbs, rhss, qgs, kdecs, glasts, sbs = [], [], [], [], [], [], []
    for q, k, v, gc, gct, beta, state, upper in probs:
        decays.append(jnp.exp(jnp.where(incl[upper], gc - gct, NEG)))
        kf = k.astype(F32)
        kb = kf * beta
        eg = jnp.exp(gc)
        g_last = gc[0:1, :] if upper else gc[c - 1:c, :]
        kbs.append(kb.astype(BF16))
        rhss.append(jnp.concatenate([v.astype(F32) * beta, kb * eg], axis=1).astype(BF16))
        qgs.append((q.astype(F32) * eg).astype(BF16))
        kdecs.append((kf * jnp.exp(g_last - gc)).astype(BF16))
        glasts.append(g_last)
        sbs.append(state.astype(BF16))
    kks = [lax.dot_general(kb, p[1], nt_dims, preferred_element_type=F32) for kb, p in zip(kbs, probs)]
    qks = [lax.dot_general(p[0], p[1], nt_dims, preferred_element_type=F32) for p in probs]
    lowers = [jnp.where(strict[p[7]], kk * dec, 0.0) for kk, dec, p in zip(kks, decays, probs)]
    intras = [(qk * dec).astype(BF16) for qk, dec in zip(qks, decays)]
    tinvs = _tri_inverse_many(lowers, ri, ci)
    uws = [jnp.dot(t.astype(BF16), r, preferred_element_type=F32) for t, r in zip(tinvs, rhss)]
    wss = [jnp.dot(uw[:, dv:].astype(BF16), sb, preferred_element_type=F32) for uw, sb in zip(uws, sbs)]
    o1s = [jnp.dot(qg, sb, preferred_element_type=F32) for qg, sb in zip(qgs, sbs)]
    v_news = [(uw[:, :dv] - ws).astype(BF16) for uw, ws in zip(uws, wss)]
    o2s = [jnp.dot(a, vn, preferred_element_type=F32) for a, vn in zip(intras, v_news)]
    upds = [lax.dot_general(kd, vn, tn_dims, preferred_element_type=F32) for kd, vn in zip(kdecs, v_news)]
    outs = [o1 + o2 for o1, o2 in zip(o1s, o2s)]
    states = [p[6] * jnp.exp(gl) + upd for p, gl, upd in zip(probs, glasts, upds)]
    return outs, states


def _gdn_scan_kernel(qf_ref, kf_ref, vf_ref, gf_ref, qb_ref, kb_ref, vb_ref, gbk_ref,
                     of_ref, ob_ref, sf, sb, *, dk):
    s = pl.program_id(1)

    @pl.when(s == 0)
    def _():
        sf[...] = jnp.zeros_like(sf)
        sb[...] = jnp.zeros_like(sb)

    c = qf_ref.shape[0]
    ri = lax.broadcasted_iota(I32, (c, c), 0)
    ci = lax.broadcasted_iota(I32, (c, c), 1)
    tri_l = (ri >= ci).astype(F32)
    tri_u = (ri <= ci).astype(F32)
    probs, sinks = [], []
    for upper, (q_ref, k_ref, v_ref, g_ref, o_ref, st) in enumerate(
            ((qf_ref, kf_ref, vf_ref, gf_ref, of_ref, sf), (qb_ref, kb_ref, vb_ref, gbk_ref, ob_ref, sb))):
        gbv = g_ref[...]
        csum = jnp.dot(tri_u if upper else tri_l, gbv, precision=HIGHEST, preferred_element_type=F32)
        csum_t = csum.T
        base = 16 * upper
        for hh in range(HEADS):
            sl = slice(hh * dk, (hh + 1) * dk)
            probs.append((q_ref[:, sl], k_ref[:, sl], v_ref[:, sl],
                          csum[:, base + hh:base + hh + 1], csum_t[base + hh:base + hh + 1, :],
                          gbv[:, base + 8 + hh:base + 9 + hh], st[hh], bool(upper)))
            sinks.append((o_ref, st, hh, sl))
    outs, states = _gdn_chunks(probs, ri, ci)
    for (o_ref, st, hh, sl), o, new_state in zip(sinks, outs, states):
        o_ref[:, sl] = o.astype(BF16)
        st[hh] = new_state


def _gdn_mixer(xall, mod_i, norm_gain, w_in, conv_w, a_log, dt_bias, dims):
    bsz, ctx_len, seq, d, tm = dims
    ltot = ctx_len + seq
    nt, nct = ltot // tm, ctx_len // tm
    n_tiles = bsz * nt
    rows = n_tiles * tm
    dk = d // HEADS
    n_in = w_in.shape[1]
    n_pad = -(-n_in // LANES) * LANES
    wp = jnp.pad(w_in, ((0, 0), (0, n_pad - n_in))).astype(BF16)
    zeros8 = jnp.zeros((2, HEADS), F32)
    lay = lambda t: jnp.pad(jnp.concatenate([t, zeros8], axis=1).reshape(1, 4 * HEADS),
                            ((0, 0), (0, LANES - 4 * HEADS)))
    mrow = _mod_row_map(nt, nct, bsz)
    hb = tm // SUBLANES
    last_hblk = rows // SUBLANES - 1
    const = lambda j: (0, 0)
    q, k, v, z, gb = pl.pallas_call(
        functools.partial(_gdn_proj_kernel, d=d, nt=nt, nct=nct, tm=tm, dk=dk),
        grid=(n_tiles,),
        in_specs=[pl.BlockSpec((SUBLANES, d), lambda j: (jnp.maximum(j * hb - 1, 0), 0)),
                  pl.BlockSpec((tm, d), lambda j: (j, 0)),
                  pl.BlockSpec((SUBLANES, d), lambda j: (jnp.minimum((j + 1) * hb, last_hblk), 0)),
                  pl.BlockSpec((None, 6, d), lambda j: (mrow(j), 0, 0)),
                  pl.BlockSpec((1, d), const),
                  pl.BlockSpec((d, n_pad), const),
                  pl.BlockSpec((GDN_CONV, 3 * d), const),
                  pl.BlockSpec((1, LANES), const),
                  pl.BlockSpec((1, LANES), const)],
        out_specs=[pl.BlockSpec((tm, d), lambda j: (j, 0))] * 4 + [pl.BlockSpec((tm, LANES), lambda j: (j, 0))],
        out_shape=[jax.ShapeDtypeStruct((rows, d), BF16)] * 4 + [jax.ShapeDtypeStruct((rows, LANES), F32)],
        scratch_shapes=[pltpu.VMEM((tm + 2 * SUBLANES, n_pad), F32)],
        compiler_params=_cparams("parallel"),
        name="gdn_proj",
    )(xall, xall, xall, mod_i, norm_gain.reshape(1, d), wp, conv_w, lay(a_log), lay(dt_bias))
    c = GDN_CHUNK
    ncl, ncc = ltot // c, ctx_len // c

    def fwd(b, s):
        return (b * ncl + s, 0)

    def bwd(b, s):
        return (b * ncl + jnp.where(s < ncc, ncc - 1 - s, ncl + ncc - 1 - s), 0)

    blk = lambda m: pl.BlockSpec((c, d), m)
    gblk = lambda m: pl.BlockSpec((c, LANES), m)
    o_f, o_b = pl.pallas_call(
        functools.partial(_gdn_scan_kernel, dk=dk),
        grid=(bsz, ncl),
        in_specs=[blk(fwd), blk(fwd), blk(fwd), gblk(fwd), blk(bwd), blk(bwd), blk(bwd), gblk(bwd)],
        out_specs=[blk(fwd), blk(bwd)],
        out_shape=[jax.ShapeDtypeStruct((rows, d), BF16)] * 2,
        scratch_shapes=[pltpu.VMEM((HEADS, dk, dk), F32), pltpu.VMEM((HEADS, dk, dk), F32)],
        compiler_params=_cparams("parallel", "arbitrary"),
        name="gdn_scan",
    )(q, k, v, gb, q, k, v, gb)
    return (o_f, o_b, z)


def _split_bf16(x):
    hi = x.astype(BF16)
    lo = (x - hi.astype(F32)).astype(BF16)
    return hi, lo


def _post_kernel(*refs, kind, d, tm, dk):
    if kind == 0:
        of_ref, ob_ref, z_ref, og_ref = refs[:4]
        refs = refs[4:]
    else:
        o_ref = refs[0]
        refs = refs[1:]
    (x_ref, mod_ref, wo_ref, g_ref, wrh_ref, wrl_ref, br_ref,
     xo_ref, h_ref, ids_ref, gate_ref, cnt_ref, base) = refs
    j = pl.program_id(0)

    @pl.when(j == 0)
    def _():
        base[...] = jnp.zeros_like(base)

    if kind == 0:
        parts = []
        for hh in range(d // dk):
            sl = slice(hh * dk, (hh + 1) * dk)
            o = of_ref[:, sl].astype(F32) + ob_ref[:, sl].astype(F32)
            o = o * lax.rsqrt(jnp.mean(o * o, axis=-1, keepdims=True) + EPS) * og_ref[...]
            parts.append((o * _silu(z_ref[:, sl].astype(F32))).astype(BF16))
        o_in = jnp.concatenate(parts, axis=1)
    else:
        o_in = o_ref[...]
    mod = mod_ref[...]
    x = x_ref[...] + mod[2:3, :] * jnp.dot(o_in, wo_ref[...], preferred_element_type=F32)
    xo_ref[...] = x
    h = _norm_mod(x, g_ref[...], mod[3:4, :], mod[4:5, :])
    h_ref[...] = h
    hi, lo = _split_bf16(h)
    logits = (jnp.dot(hi, wrh_ref[...], preferred_element_type=F32)
              + jnp.dot(lo, wrh_ref[...], preferred_element_type=F32)
              + jnp.dot(hi, wrl_ref[...], preferred_element_type=F32)) + br_ref[...]
    lane = lax.broadcasted_iota(I32, logits.shape, 1)
    big = jnp.int32(1 << 20)
    is_g = lane < MOE_GROUPS
    gl = jnp.where(is_g, logits, NEG)
    gmax = jnp.max(gl, axis=-1, keepdims=True)
    gsel = jnp.min(jnp.where(gl == gmax, lane, big), axis=-1, keepdims=True)
    p_group = 1.0 / jnp.sum(jnp.where(is_g, jnp.exp(gl - gmax), 0.0), axis=-1, keepdims=True)
    in_grp = jnp.logical_and(lane >= MOE_GROUPS + gsel * MOE_PER_GROUP,
                             lane < MOE_GROUPS + (gsel + 1) * MOE_PER_GROUP)
    el = jnp.where(in_grp, logits, NEG)
    v0 = jnp.max(el, axis=-1, keepdims=True)
    i0 = jnp.min(jnp.where(el == v0, lane, big), axis=-1, keepdims=True)
    el1 = jnp.where(lane == i0, NEG, el)
    v1 = jnp.max(el1, axis=-1, keepdims=True)
    i1 = jnp.min(jnp.where(el1 == v1, lane, big), axis=-1, keepdims=True)
    e1 = jnp.exp(v1 - v0)
    w0 = p_group / (1.0 + e1)
    w1 = p_group * e1 / (1.0 + e1)
    oh0 = lane == i0
    oh1 = lane == i1
    onehot = jnp.where(jnp.logical_or(oh0, oh1), 1.0, 0.0)
    ri = lax.broadcasted_iota(I32, (tm, tm), 0)
    ci = lax.broadcasted_iota(I32, (tm, tm), 1)
    tri = jnp.where(ri > ci, 1.0, 0.0).astype(BF16)
    before = base[...] + jnp.dot(tri, onehot.astype(BF16), preferred_element_type=F32)
    r0 = jnp.sum(jnp.where(oh0, before, 0.0), axis=-1, keepdims=True)
    r1 = jnp.sum(jnp.where(oh1, before, 0.0), axis=-1, keepdims=True)
    new_base = base[...] + jnp.sum(onehot, axis=0, keepdims=True)
    base[...] = new_base
    cnt_ref[...] = new_base
    e0 = i0 - MOE_GROUPS
    e1i = i1 - MOE_GROUPS
    ids = jnp.where(lane == 0, e0, jnp.where(lane == 1, e1i, jnp.where(
        lane == 2, r0.astype(I32), jnp.where(lane == 3, r1.astype(I32), 0))))
    ids_ref[...] = ids.T[0:SUBLANES, :]
    gate_ref[...] = jnp.where(lane == 0, w0, jnp.where(lane == 1, w1, 0.0))


def _post_mixer(kind, mixer_out, xall, mod_i, w_out, ffn_gain, w_group, b_group, w_expert, b_expert,
                dims, o_gain=None):
    bsz, ctx_len, seq, d, tm = dims
    ltot = ctx_len + seq
    nt, nct = ltot // tm, ctx_len // tm
    n_tiles = bsz * nt
    rows = n_tiles * tm
    dk = d // HEADS
    mrow = _mod_row_map(nt, nct, bsz)
    wr = jnp.pad(jnp.concatenate([w_group, w_expert], axis=1),
                 ((0, 0), (0, LANES - MOE_GROUPS - MOE_EXPERTS)))
    wr_hi = wr.astype(BF16)
    wr_lo = (wr - wr_hi.astype(F32)).astype(BF16)
    br = jnp.pad(jnp.concatenate([b_group, b_expert]), (0, LANES - MOE_GROUPS - MOE_EXPERTS)).reshape(1, LANES)
    const = lambda j: (0, 0)
    row = lambda j: (j, 0)
    lead_specs = [pl.BlockSpec((tm, d), row)] * len(mixer_out)
    lead_args = list(mixer_out)
    if kind == 0:
        lead_specs.append(pl.BlockSpec((1, dk), const))
        lead_args.append(o_gain.reshape(1, dk))
    n_lead = len(lead_args)
    outs = pl.pallas_call(
        functools.partial(_post_kernel, kind=kind, d=d, tm=tm, dk=dk),
        grid=(n_tiles,),
        in_specs=lead_specs + [pl.BlockSpec((tm, d), row),
                               pl.BlockSpec((None, 6, d), lambda j: (mrow(j), 0, 0)),
                               pl.BlockSpec((w_out.shape[0], d), const),
                               pl.BlockSpec((1, d), const),
                               pl.BlockSpec((d, LANES), const),
                               pl.BlockSpec((d, LANES), const),
                               pl.BlockSpec((1, LANES), const)],
        out_specs=[pl.BlockSpec((tm, d), row), pl.BlockSpec((tm, d), row),
                   pl.BlockSpec((SUBLANES, tm), lambda j: (0, j)), pl.BlockSpec((tm, LANES), row),
                   pl.BlockSpec((1, LANES), const)],
        out_shape=[jax.ShapeDtypeStruct((rows, d), F32), jax.ShapeDtypeStruct((rows, d), F32),
                   jax.ShapeDtypeStruct((SUBLANES, rows), I32), jax.ShapeDtypeStruct((rows, LANES), F32),
                   jax.ShapeDtypeStruct((1, LANES), F32)],
        scratch_shapes=[pltpu.VMEM((1, LANES), F32)],
        input_output_aliases={n_lead: 0},
        compiler_params=_cparams("arbitrary"),
        name="post_mixer",
    )(*lead_args, xall, mod_i, w_out.astype(BF16), ffn_gain.reshape(1, d), wr_hi, wr_lo, br)
    return outs


def _dispatch_kernel(zlo_ref, zhi_ref, nu_ref, dest_ref, h_ref, xs_ref, zblk, sem, zsem, *, tm, blk, n_blocks):
    j = pl.program_id(0)

    def row_copy(src, dst_row, s):
        return pltpu.make_async_copy(src, xs_ref.at[pl.ds(dst_row, 1)], s)

    def blk_copy(bi):
        return pltpu.make_async_copy(zblk, xs_ref.at[pl.ds(pl.multiple_of(bi * blk, blk), blk)], zsem)

    @pl.when(j == 0)
    def _():
        zblk[...] = jnp.zeros_like(zblk)

        def per_expert(e, carry):
            lo, hi = zlo_ref[e], zhi_ref[e]

            def start(r, c):
                row_copy(zblk.at[pl.ds(0, 1)], r, zsem).start()
                return c

            def wait(r, c):
                row_copy(zblk.at[pl.ds(0, 1)], r, zsem).wait()
                return c

            lax.fori_loop(lo, hi, start, 0)
            lax.fori_loop(lo, hi, wait, 0)
            return carry

        lax.fori_loop(0, MOE_EXPERTS, per_expert, 0)

        def tail_start(bi, c):
            blk_copy(bi).start()
            return c

        def tail_wait(bi, c):
            blk_copy(bi).wait()
            return c

        lax.fori_loop(nu_ref[0], n_blocks, tail_start, 0)
        lax.fori_loop(nu_ref[0], n_blocks, tail_wait, 0)

    def body(r, c):
        for kk in range(MOE_TOP_K):
            row_copy(h_ref.at[pl.ds(r, 1)], dest_ref[0, kk * tm + r], sem).start()
        return c

    lax.fori_loop(0, tm, body, 0, unroll=8)
    for _ in range(MOE_TOP_K):
        pltpu.make_async_copy(h_ref, xs_ref.at[pl.ds(0, tm)], sem).wait()


def _expert_kernel(be_ref, nu_ref, x_ref, w1_ref, w3_ref, w2_ref, y_ref):
    j = pl.program_id(0)

    @pl.when(j < nu_ref[0])
    def _():
        x = x_ref[...].astype(BF16)
        a = jnp.dot(x, w1_ref[...], preferred_element_type=F32)
        b = jnp.dot(x, w3_ref[...], preferred_element_type=F32)
        y_ref[...] = jnp.dot((_silu(a) * b).astype(BF16), w2_ref[...], preferred_element_type=F32)

    @pl.when(j >= nu_ref[0])
    def _():
        y_ref[...] = jnp.zeros_like(y_ref)


def _combine_kernel(dest_ref, x_ref, gate_ref, mod_ref, fg_ref, yb_ref, xo_ref, ybuf, sem, *, tm, final):
    def body(r, c):
        for kk in range(MOE_TOP_K):
            pltpu.make_async_copy(yb_ref.at[pl.ds(dest_ref[0, kk * tm + r], 1)],
                                  ybuf.at[kk, pl.ds(r, 1)], sem).start()
        return c

    lax.fori_loop(0, tm, body, 0, unroll=8)
    for kk in range(MOE_TOP_K):
        pltpu.make_async_copy(yb_ref.at[pl.ds(0, tm)], ybuf.at[kk], sem).wait()
    gate = gate_ref[...]
    y = ybuf[0] * gate[:, 0:1] + ybuf[1] * gate[:, 1:2]
    x = x_ref[...] + mod_ref[5:6, :] * y
    if final:
        x = x * lax.rsqrt(jnp.mean(x * x, axis=-1, keepdims=True) + EPS) * fg_ref[...]
    xo_ref[...] = x


def _moe(xall, h, ids, gates, counts, mod_i, w1, w3, w2, final_gain, dims, final):
    bsz, ctx_len, seq, d, tm = dims
    ltot = ctx_len + seq
    nt, nct = ltot // tm, ctx_len // tm
    n_tiles = bsz * nt
    rows = n_tiles * tm
    e = MOE_EXPERTS
    blk = MOE_BLOCK
    a = rows * MOE_TOP_K
    n_blocks = -(-(a + e * (blk - 1)) // blk)
    cnt = counts[0, MOE_GROUPS:MOE_GROUPS + e].astype(I32)
    padded = (cnt + blk - 1) // blk * blk
    pad_end = jnp.cumsum(padded)
    pad_start = pad_end - padded
    dest = (pad_start[ids[:MOE_TOP_K]] + ids[MOE_TOP_K:2 * MOE_TOP_K]).astype(I32)
    dest = dest.reshape(MOE_TOP_K, n_tiles, tm).transpose(1, 0, 2).reshape(n_tiles, 1, tm * MOE_TOP_K)
    n_used = (pad_end[-1] // blk).astype(I32).reshape(1)
    blk_first = jnp.arange(n_blocks, dtype=I32) * blk
    blk_expert = jnp.minimum(jnp.sum((pad_end[None, :] <= blk_first[:, None]).astype(I32), axis=1), e - 1)
    smem_dest = pl.BlockSpec((None, 1, tm * MOE_TOP_K), lambda j, *_: (j, 0, 0), memory_space=pltpu.SMEM)
    xs = pl.pallas_call(
        functools.partial(_dispatch_kernel, tm=tm, blk=blk, n_blocks=n_blocks),
        grid_spec=pltpu.PrefetchScalarGridSpec(
            num_scalar_prefetch=3, grid=(n_tiles,),
            in_specs=[smem_dest, pl.BlockSpec((tm, d), lambda j, *_: (j, 0))],
            out_specs=pl.BlockSpec(memory_space=pl.ANY),
            scratch_shapes=[pltpu.VMEM((blk, d), F32), pltpu.SemaphoreType.DMA,
                            pltpu.SemaphoreType.DMA]),
        out_shape=jax.ShapeDtypeStruct((n_blocks * blk, d), F32),
        compiler_params=_cparams("arbitrary"),
        name="moe_dispatch",
    )((pad_start + cnt).astype(I32), pad_end.astype(I32), n_used, dest, h)

    def xmap(j, be, nu):
        return (jnp.minimum(j, nu[0] - 1), 0)

    def wmap(j, be, nu):
        return (be[jnp.minimum(j, nu[0] - 1)], 0, 0)

    f = w1.shape[-1]
    yb = pl.pallas_call(
        _expert_kernel,
        grid_spec=pltpu.PrefetchScalarGridSpec(
            num_scalar_prefetch=2, grid=(n_blocks,),
            in_specs=[pl.BlockSpec((blk, d), xmap),
                      pl.BlockSpec((None, d, f), wmap),
                      pl.BlockSpec((None, d, f), wmap),
                      pl.BlockSpec((None, f, d), wmap)],
            out_specs=pl.BlockSpec((blk, d), lambda j, be, nu: (j, 0))),
        out_shape=jax.ShapeDtypeStruct((n_blocks * blk, d), F32),
        compiler_params=_cparams("arbitrary"),
        name="moe_experts",
    )(blk_expert, n_used, xs, w1.astype(BF16), w3.astype(BF16), w2.astype(BF16))

    if final:
        nlt = seq // tm
        grid = (bsz, nlt)
        tile = lambda b, i: b * nt + nct + i
        mod_map = lambda b, i: (b, 0, 0)
        out_map = lambda b, i: (b * nlt + i, 0)
        out_rows, aliases = bsz * seq, {}
    else:
        mrow = _mod_row_map(nt, nct, bsz)
        grid = (n_tiles,)
        tile = lambda j: j
        mod_map = lambda j: (mrow(j), 0, 0)
        out_map = lambda j: (j, 0)
        out_rows, aliases = rows, {1: 0}
    out = pl.pallas_call(
        functools.partial(_combine_kernel, tm=tm, final=final),
        grid=grid,
        in_specs=[pl.BlockSpec((None, 1, tm * MOE_TOP_K), lambda *g: (tile(*g), 0, 0), memory_space=pltpu.SMEM),
                  pl.BlockSpec((tm, d), lambda *g: (tile(*g), 0)),
                  pl.BlockSpec((tm, LANES), lambda *g: (tile(*g), 0)),
                  pl.BlockSpec((None, 6, d), mod_map),
                  pl.BlockSpec((1, d), lambda *g: (0, 0)),
                  pl.BlockSpec(memory_space=pl.ANY)],
        out_specs=pl.BlockSpec((tm, d), out_map),
        out_shape=jax.ShapeDtypeStruct((out_rows, d), F32),
        scratch_shapes=[pltpu.VMEM((MOE_TOP_K, tm, d), F32), pltpu.SemaphoreType.DMA],
        input_output_aliases=aliases,
        compiler_params=_cparams(*(("arbitrary",) * len(grid))),
        name="moe_combine",
    )(dest, xall, gates, mod_i, final_gain.reshape(1, d), yb)
    return out


def kernel(x, c, ctx, c_ctx, w_mod, b_mod, norm_mix, norm_ffn, gdn_w_in, gdn_conv, gdn_a_log, gdn_dt_bias, gdn_norm, gdn_w_out, diff_w_qkv, diff_lambda, diff_norm, diff_w_out, mla_w_down, mla_q_norm, mla_kv_norm, mla_w_uq, mla_w_ukv, mla_w_out, moe_w_group, moe_b_group, moe_w_expert, moe_b_expert, moe_w1, moe_w3, moe_w2, final_norm):
    bsz, seq, d = x.shape
    ctx_len = ctx.shape[1]
    depth = w_mod.shape[0]
    tm = _row_tile(ctx_len)
    assert d % LANES == 0 and d // HEADS == LANES
    assert ctx_len % tm == 0 and seq % tm == 0 and ctx_len % GDN_CHUNK == 0 and seq % GDN_CHUNK == 0
    dims = (bsz, ctx_len, seq, d, tm)
    ltot = ctx_len + seq
    xall = jnp.concatenate([ctx, x], axis=1).reshape(bsz * ltot, d)
    mod = _mod_vectors(c, c_ctx, w_mod, b_mod)
    tables = _rope_tables(seq, ctx_len)
    for i in range(depth):
        kind, j = i % N_MIXERS, i // N_MIXERS
        if kind == 0:
            mixer_out = _gdn_mixer(xall, mod[i], norm_mix[i], gdn_w_in[j], gdn_conv[j], gdn_a_log[j],
                                   gdn_dt_bias[j], dims)
            w_out, o_gain = gdn_w_out[j], gdn_norm[j]
        elif kind == 1:
            lam_init = 0.8 - 0.6 * math.exp(-0.3 * i)
            mixer_out = _diff_mixer(xall, mod[i], norm_mix[i], diff_w_qkv[j], diff_lambda[j], diff_norm[j],
                                    dims, tables, lam_init)
            w_out, o_gain = diff_w_out[j], None
        else:
            mixer_out = _mla_mixer(xall, mod[i], norm_mix[i], mla_w_down[j], mla_q_norm[j], mla_kv_norm[j],
                                   mla_w_uq[j], mla_w_ukv[j], dims, tables)
            w_out, o_gain = mla_w_out[j], None
        xall, h, ids, gates, counts = _post_mixer(kind, mixer_out, xall, mod[i], w_out, norm_ffn[i],
                                                  moe_w_group[i], moe_b_group[i], moe_w_expert[i],
                                                  moe_b_expert[i], dims, o_gain)
        xall = _moe(xall, h, ids, gates, counts, mod[i], moe_w1[i], moe_w3[i], moe_w2[i], final_norm, dims,
                    final=(i == depth - 1))
    return xall.reshape(bsz, seq, d)
```

```python
import functools
import math

import jax
import jax.numpy as jnp
from jax import lax
from jax.experimental import pallas as pl
from jax.experimental.pallas import tpu as pltpu

F32 = jnp.float32
BF16 = jnp.bfloat16
I32 = jnp.int32
HIGHEST = lax.Precision.HIGHEST

LANES = 128
SUBLANES = 8
VMEM_LIMIT = 56 * 1024 * 1024

EPS = 1e-6
GRID_W = 64
ROPE_THETA = 10000.0
N_MIXERS = 3
HEADS = 8
GDN_CONV = 5
GDN_CHUNK = 64
DIFF_SUBLN_EPS = 1e-5
MLA_NOPE = 128
MLA_ROPE = 64
MLA_V = 128
MOE_GROUPS = 4
MOE_PER_GROUP = 8
MOE_EXPERTS = MOE_GROUPS * MOE_PER_GROUP
MOE_TOP_K = 2
MOE_PAIRS = MOE_PER_GROUP * (MOE_PER_GROUP - 1) // 2
MOE_CLASSES = MOE_GROUPS * MOE_PAIRS
MOE_BLOCK = 256
NEG = -1e30


def _cparams(*sem):
    return pltpu.CompilerParams(dimension_semantics=sem, vmem_limit_bytes=VMEM_LIMIT)


def _row_tile(ctx_len):
    return 256 if ctx_len % 256 == 0 else 128


def _mod_row_map(nt, nct, bsz):
    def f(j):
        return jnp.where(j % nt < nct, bsz, j // nt)
    return f


def _norm_mod(x, gain, shift, scale, eps=EPS):
    var = jnp.mean(x * x, axis=-1, keepdims=True)
    y = x * lax.rsqrt(var + eps) * gain
    return y * (1.0 + scale) + shift


def _silu(x):
    return x * jax.nn.sigmoid(x)


def _mod_kernel(c_ref, w_ref, b_ref, o_ref):
    s = _silu(c_ref[...])
    o_ref[...] = jnp.dot(s, w_ref[...], precision=HIGHEST, preferred_element_type=F32) + b_ref[...]


def _mod_vectors(c, c_ctx, w_mod, b_mod):
    depth, d, n = w_mod.shape
    bsz = c.shape[0]
    rows = -(-(bsz + 1) // SUBLANES) * SUBLANES
    cc = jnp.zeros((rows, d), F32).at[:bsz].set(c).at[bsz].set(c_ctx)
    tn = 512
    out = pl.pallas_call(
        _mod_kernel,
        grid=(depth, n // tn),
        in_specs=[pl.BlockSpec((rows, d), lambda i, j: (0, 0)),
                  pl.BlockSpec((None, d, tn), lambda i, j: (i, 0, j)),
                  pl.BlockSpec((None, 1, tn), lambda i, j: (i, 0, j))],
        out_specs=pl.BlockSpec((None, rows, tn), lambda i, j: (i, 0, j)),
        out_shape=jax.ShapeDtypeStruct((depth, rows, n), F32),
        compiler_params=_cparams("parallel", "parallel"),
        name="mod_vectors",
    )(cc, w_mod, b_mod.reshape(depth, 1, n))
    return out.reshape(depth, rows, 6, d)


def _rope_tables(seq, ctx_len):
    quarter = 16
    inv_freq = ROPE_THETA ** (-jnp.arange(quarter, dtype=F32) / quarter)
    t = jnp.arange(seq)
    row = (t // GRID_W).astype(F32)[:, None] * inv_freq
    col = (t % GRID_W).astype(F32)[:, None] * inv_freq
    cos = jnp.concatenate([jnp.cos(row), jnp.cos(row), jnp.cos(col), jnp.cos(col)], axis=1)
    sin = jnp.concatenate([-jnp.sin(row), jnp.sin(row), -jnp.sin(col), jnp.sin(col)], axis=1)
    cos = jnp.concatenate([jnp.ones((ctx_len, 64), F32), cos], axis=0)
    sin = jnp.concatenate([jnp.zeros((ctx_len, 64), F32), sin], axis=0)
    return jnp.tile(cos, (1, 2)), jnp.tile(sin, (1, 2))


def _rope128(blk, cos, sin):
    lane = lax.broadcasted_iota(I32, blk.shape, 1)
    first = (lane % 32) < 16
    partner = jnp.where(first, pltpu.roll(blk, LANES - 16, 1), pltpu.roll(blk, 16, 1))
    return blk * cos + partner * sin


def _diff_proj_kernel(x_ref, mod_ref, g_ref, w_ref, cos_ref, sin_ref, o_ref, vt_ref, *, d, q_scale):
    h = _norm_mod(x_ref[...], g_ref[...], mod_ref[0:1, :], mod_ref[1:2, :])
    p = jnp.dot(h.astype(BF16), w_ref[...], preferred_element_type=F32)
    cos = cos_ref[...]
    sin = sin_ref[...]
    nqk = 2 * d // LANES
    for cb in range(nqk):
        r = _rope128(p[:, cb * LANES:(cb + 1) * LANES], cos, sin)
        if cb < nqk // 2:
            r = r * q_scale
        o_ref[:, cb * LANES:(cb + 1) * LANES] = r.astype(BF16)
    vt_ref[...] = p[:, 2 * d:].T.astype(BF16)


def _pick_tk(n):
    for cand in (768, 512, 384, 256, 128):
        if n % cand == 0:
            return cand
    raise ValueError(n)


def _flash_t(streams, k_ref, vt_ref, m_ref, l_ref, acc_ref, nsteps, tk):
    for s in range(len(streams)):
        m_ref[s] = jnp.full(m_ref.shape[1:], NEG, F32)
        l_ref[s] = jnp.zeros(l_ref.shape[1:], F32)
        acc_ref[s] = jnp.zeros(acc_ref.shape[1:], F32)
    nt_dims = (((1,), (1,)), ((), ()))

    def scores(i):
        return [lax.dot_general(k_ref[i * tk:(i + 1) * tk, kc], q, nt_dims, preferred_element_type=F32)
                for q, kc, _ in streams]

    sts = scores(0)
    for i in range(nsteps):
        nxt = scores(i + 1) if i + 1 < nsteps else None
        alphas, ps = [], []
        for s, st in enumerate(sts):
            m_old = m_ref[s]
            m_new = jnp.maximum(m_old, jnp.max(st, axis=0, keepdims=True))
            alpha = jnp.exp2(m_old - m_new)
            p = jnp.exp2(st - m_new)
            l_ref[s] = alpha * l_ref[s] + jnp.sum(p, axis=0, keepdims=True)
            m_ref[s] = m_new
            alphas.append(alpha)
            ps.append(p.astype(BF16))
        pvs = [jnp.dot(vt_ref[vr, i * tk:(i + 1) * tk], p, preferred_element_type=F32)
               for (_, _, vr), p in zip(streams, ps)]
        for s in range(len(streams)):
            acc_ref[s] = alphas[s] * acc_ref[s] + pvs[s]
        sts = nxt


def _flash_ctx_or_all(streams, k_ref, vt_ref, m_ref, l_ref, acc_ref, nct, ctx_len, ltot):
    i = pl.program_id(2)
    tk_c, tk_l = _pick_tk(ctx_len), _pick_tk(ltot)

    @pl.when(i < nct)
    def _():
        _flash_t(streams, k_ref, vt_ref, m_ref, l_ref, acc_ref, ctx_len // tk_c, tk_c)

    @pl.when(i >= nct)
    def _():
        _flash_t(streams, k_ref, vt_ref, m_ref, l_ref, acc_ref, ltot // tk_l, tk_l)


ATTN_HEADS_PER_STEP = 2


def _diff_attn_kernel(lam_ref, gain_ref, q_ref, k_ref, vt_ref, o_ref, m_ref, l_ref, acc_ref, *, nct,
                      ctx_len, ltot, lam_init, hw):
    streams = []
    for g in range(ATTN_HEADS_PER_STEP):
        cols = slice(g * hw, (g + 1) * hw)
        q = q_ref[:, cols]
        lane = lax.broadcasted_iota(I32, q.shape, 1)
        zero = jnp.zeros_like(q)
        streams.append((jnp.where(lane < hw // 2, q, zero), cols, cols))
        streams.append((jnp.where(lane >= hw // 2, q, zero), cols, cols))
    _flash_ctx_or_all(streams, k_ref, vt_ref, m_ref, l_ref, acc_ref, nct, ctx_len, ltot)
    lv = lam_ref[...]
    lam = (jnp.exp(jnp.sum(lv[0:1] * lv[1:2], keepdims=True))
           - jnp.exp(jnp.sum(lv[2:3] * lv[3:4], keepdims=True)) + lam_init)
    for g in range(ATTN_HEADS_PER_STEP):
        o = acc_ref[2 * g] / l_ref[2 * g] - lam * (acc_ref[2 * g + 1] / l_ref[2 * g + 1])
        var = jnp.mean(o * o, axis=0, keepdims=True)
        o = o * lax.rsqrt(var + DIFF_SUBLN_EPS) * gain_ref[...] * (1.0 - lam_init)
        o_ref[:, g * hw:(g + 1) * hw] = o.T.astype(BF16)


def _diff_mixer(xall, mod_i, norm_gain, w_qkv, lam_vec, sub_gain, dims, tables, lam_init):
    bsz, ctx_len, seq, d, tm = dims
    ltot = ctx_len + seq
    nt, nct = ltot // tm, ctx_len // tm
    n_tiles = bsz * nt
    dh = d // HEADS // 2
    cos, sin = tables
    mrow = _mod_row_map(nt, nct, bsz)
    rows = n_tiles * tm
    qk, vt = pl.pallas_call(
        functools.partial(_diff_proj_kernel, d=d, q_scale=dh ** -0.5 * math.log2(math.e)),
        grid=(n_tiles,),
        in_specs=[pl.BlockSpec((tm, d), lambda j: (j, 0)),
                  pl.BlockSpec((None, 6, d), lambda j: (mrow(j), 0, 0)),
                  pl.BlockSpec((1, d), lambda j: (0, 0)),
                  pl.BlockSpec((d, 3 * d), lambda j: (0, 0)),
                  pl.BlockSpec((tm, LANES), lambda j: (j % nt, 0)),
                  pl.BlockSpec((tm, LANES), lambda j: (j % nt, 0))],
        out_specs=[pl.BlockSpec((tm, 2 * d), lambda j: (j, 0)),
                   pl.BlockSpec((d, tm), lambda j: (0, j))],
        out_shape=[jax.ShapeDtypeStruct((rows, 2 * d), BF16), jax.ShapeDtypeStruct((d, rows), BF16)],
        compiler_params=_cparams("parallel"),
        name="diff_proj",
    )(xall, mod_i, norm_gain.reshape(1, d), w_qkv.astype(BF16), cos, sin)
    hw = 2 * dh
    hps = ATTN_HEADS_PER_STEP
    hg = HEADS // hps
    o = pl.pallas_call(
        functools.partial(_diff_attn_kernel, nct=nct, ctx_len=ctx_len, ltot=ltot, lam_init=lam_init, hw=hw),
        grid=(bsz, hg, nt),
        in_specs=[pl.BlockSpec((4, dh), lambda b, h, i: (0, 0)),
                  pl.BlockSpec((hw, 1), lambda b, h, i: (0, 0)),
                  pl.BlockSpec((tm, hps * hw), lambda b, h, i: (b * nt + i, h)),
                  pl.BlockSpec((ltot, hps * hw), lambda b, h, i: (b, hg + h)),
                  pl.BlockSpec((hps * hw, ltot), lambda b, h, i: (h, b))],
        out_specs=pl.BlockSpec((tm, hps * hw), lambda b, h, i: (b * nt + i, h)),
        out_shape=jax.ShapeDtypeStruct((rows, d), BF16),
        scratch_shapes=[pltpu.VMEM((2 * hps, 1, tm), F32), pltpu.VMEM((2 * hps, 1, tm), F32),
                        pltpu.VMEM((2 * hps, hw, tm), F32)],
        compiler_params=_cparams("parallel", "parallel", "arbitrary"),
        name="diff_attn",
    )(lam_vec, sub_gain.reshape(hw, 1), qk, qk, vt)
    return (o,)


def _mla_proj_kernel(x_ref, mod_ref, g_ref, wd_ref, qg_ref, kvg_ref, wq_ref, wkv_ref, cos_ref, sin_ref,
                     q_ref, k_ref, v_ref, *, q_lora, kv_lora, scale):
    h = _norm_mod(x_ref[...], g_ref[...], mod_ref[0:1, :], mod_ref[1:2, :])
    p = jnp.dot(h.astype(BF16), wd_ref[...], preferred_element_type=F32)
    cq = p[:, :q_lora]
    cq = cq * lax.rsqrt(jnp.mean(cq * cq, axis=-1, keepdims=True) + EPS) * qg_ref[...]
    ckv = p[:, q_lora:q_lora + kv_lora]
    ckv = ckv * lax.rsqrt(jnp.mean(ckv * ckv, axis=-1, keepdims=True) + EPS) * kvg_ref[...]
    cos = cos_ref[...]
    sin = sin_ref[...]
    kr = _rope128(p[:, q_lora + kv_lora:], cos, sin).astype(BF16)
    q = jnp.dot(cq.astype(BF16), wq_ref[...], preferred_element_type=F32)
    kv = jnp.dot(ckv.astype(BF16), wkv_ref[...], preferred_element_type=F32)
    hq = MLA_NOPE + LANES
    for hh in range(HEADS):
        q_ref[:, hh * hq:hh * hq + MLA_NOPE] = (q[:, hh * hq:hh * hq + MLA_NOPE] * scale).astype(BF16)
        qr = _rope128(q[:, hh * hq + MLA_NOPE:(hh + 1) * hq], cos, sin) * scale
        q_ref[:, hh * hq + MLA_NOPE:(hh + 1) * hq] = qr.astype(BF16)
        k_ref[:, hh * hq:hh * hq + MLA_NOPE] = kv[:, hh * MLA_NOPE:(hh + 1) * MLA_NOPE].astype(BF16)
        k_ref[:, hh * hq + MLA_NOPE:(hh + 1) * hq] = kr
    v_ref[...] = kv[:, HEADS * MLA_NOPE:].T.astype(BF16)


def _mla_attn_kernel(q_ref, k_ref, vt_ref, o_ref, m_ref, l_ref, acc_ref, *, nct, ctx_len, ltot, hq):
    streams = [(q_ref[:, g * hq:(g + 1) * hq], slice(g * hq, (g + 1) * hq), slice(g * MLA_V, (g + 1) * MLA_V))
               for g in range(ATTN_HEADS_PER_STEP)]
    _flash_ctx_or_all(streams, k_ref, vt_ref, m_ref, l_ref, acc_ref, nct, ctx_len, ltot)
    for g in range(ATTN_HEADS_PER_STEP):
        o_ref[:, g * MLA_V:(g + 1) * MLA_V] = (acc_ref[g] / l_ref[g]).T.astype(BF16)


def _mla_mixer(xall, mod_i, norm_gain, w_down, q_gain, kv_gain, w_uq, w_ukv, dims, tables):
    bsz, ctx_len, seq, d, tm = dims
    ltot = ctx_len + seq
    nt, nct = ltot // tm, ctx_len // tm
    n_tiles = bsz * nt
    rows = n_tiles * tm
    q_lora, kv_lora = q_gain.shape[0], kv_gain.shape[0]
    cos, sin = tables
    mrow = _mod_row_map(nt, nct, bsz)
    hq = MLA_NOPE + LANES
    wd = jnp.pad(w_down, ((0, 0), (0, LANES - MLA_ROPE))).astype(BF16)
    nd = wd.shape[1]
    wq = jnp.pad(w_uq.reshape(q_lora, HEADS, MLA_NOPE + MLA_ROPE),
                 ((0, 0), (0, 0), (0, LANES - MLA_ROPE))).reshape(q_lora, HEADS * hq).astype(BF16)
    wkv = w_ukv.reshape(kv_lora, HEADS, MLA_NOPE + MLA_V)
    wkv = jnp.concatenate([wkv[:, :, :MLA_NOPE].reshape(kv_lora, HEADS * MLA_NOPE),
                           wkv[:, :, MLA_NOPE:].reshape(kv_lora, HEADS * MLA_V)], axis=1).astype(BF16)
    scale = (MLA_NOPE + MLA_ROPE) ** -0.5 * math.log2(math.e)
    const = lambda j: (0, 0)
    q, k, v = pl.pallas_call(
        functools.partial(_mla_proj_kernel, q_lora=q_lora, kv_lora=kv_lora, scale=scale),
        grid=(n_tiles,),
        in_specs=[pl.BlockSpec((tm, d), lambda j: (j, 0)),
                  pl.BlockSpec((None, 6, d), lambda j: (mrow(j), 0, 0)),
                  pl.BlockSpec((1, d), const),
                  pl.BlockSpec((d, nd), const),
                  pl.BlockSpec((1, q_lora), const),
                  pl.BlockSpec((1, kv_lora), const),
                  pl.BlockSpec((q_lora, HEADS * hq), const),
                  pl.BlockSpec((kv_lora, HEADS * (MLA_NOPE + MLA_V)), const),
                  pl.BlockSpec((tm, LANES), lambda j: (j % nt, 0)),
                  pl.BlockSpec((tm, LANES), lambda j: (j % nt, 0))],
        out_specs=[pl.BlockSpec((tm, HEADS * hq), lambda j: (j, 0)),
                   pl.BlockSpec((tm, HEADS * hq), lambda j: (j, 0)),
                   pl.BlockSpec((HEADS * MLA_V, tm), lambda j: (0, j))],
        out_shape=[jax.ShapeDtypeStruct((rows, HEADS * hq), BF16),
                   jax.ShapeDtypeStruct((rows, HEADS * hq), BF16),
                   jax.ShapeDtypeStruct((HEADS * MLA_V, rows), BF16)],
        compiler_params=_cparams("parallel"),
        name="mla_proj",
    )(xall, mod_i, norm_gain.reshape(1, d), wd, q_gain.reshape(1, q_lora), kv_gain.reshape(1, kv_lora),
      wq, wkv, cos, sin)
    hps = ATTN_HEADS_PER_STEP
    o = pl.pallas_call(
        functools.partial(_mla_attn_kernel, nct=nct, ctx_len=ctx_len, ltot=ltot, hq=hq),
        grid=(bsz, HEADS // hps, nt),
        in_specs=[pl.BlockSpec((tm, hps * hq), lambda b, h, i: (b * nt + i, h)),
                  pl.BlockSpec((ltot, hps * hq), lambda b, h, i: (b, h)),
                  pl.BlockSpec((hps * MLA_V, ltot), lambda b, h, i: (h, b))],
        out_specs=pl.BlockSpec((tm, hps * MLA_V), lambda b, h, i: (b * nt + i, h)),
        out_shape=jax.ShapeDtypeStruct((rows, HEADS * MLA_V), BF16),
        scratch_shapes=[pltpu.VMEM((hps, 1, tm), F32), pltpu.VMEM((hps, 1, tm), F32),
                        pltpu.VMEM((hps, MLA_V, tm), F32)],
        compiler_params=_cparams("parallel", "parallel", "arbitrary"),
        name="mla_attn",
    )(q, k, v)
    return (o,)


def _gdn_proj_kernel(xp_ref, x_ref, xn_ref, mod_ref, g_ref, w_ref, cw_ref, alog_ref, dtb_ref,
                     q_ref, k_ref, v_ref, z_ref, gb_ref, pbuf, *, d, nt, nct, tm, dk):
    j = pl.program_id(0)
    r = j % nt
    first = jnp.logical_or(r == 0, r == nct)
    last = jnp.logical_or(r == nct - 1, r == nt - 1)
    halo = SUBLANES
    xe = jnp.concatenate([xp_ref[...], x_ref[...], xn_ref[...]], axis=0)
    h = _norm_mod(xe, g_ref[...], mod_ref[0:1, :], mod_ref[1:2, :])
    rid = lax.broadcasted_iota(I32, (tm + 2 * halo, 1), 0)
    keep = jnp.logical_and(jnp.logical_or(rid >= halo, jnp.logical_not(first)),
                           jnp.logical_or(rid < tm + halo, jnp.logical_not(last)))
    h = jnp.where(keep, h, 0.0)
    pbuf[...] = jnp.dot(h.astype(BF16), w_ref[...], preferred_element_type=F32)
    half = GDN_CONV // 2

    def conv_block(c0):
        acc = None
        for t in range(GDN_CONV):
            term = pbuf[pl.ds(halo - half + t, tm), pl.ds(c0, LANES)] * cw_ref[t:t + 1, pl.ds(c0, LANES)]
            acc = term if acc is None else acc + term
        return _silu(acc)

    for hh in range(3 * d // LANES):
        c0 = hh * LANES
        blk = conv_block(c0)
        if hh < 2 * d // LANES:
            blk = blk * lax.rsqrt(jnp.sum(blk * blk, axis=-1, keepdims=True) + EPS)
        if hh < d // LANES:
            q_ref[:, c0:c0 + LANES] = (blk * dk ** -0.5).astype(BF16)
        elif hh < 2 * d // LANES:
            k_ref[:, c0 - d:c0 - d + LANES] = blk.astype(BF16)
        else:
            v_ref[:, c0 - 2 * d:c0 - 2 * d + LANES] = blk.astype(BF16)
    z_ref[...] = pbuf[halo:halo + tm, 3 * d:4 * d].astype(BF16)
    ab = pbuf[halo:halo + tm, 4 * d:4 * d + LANES]
    lane = lax.broadcasted_iota(I32, ab.shape, 1)
    is_a = (lane % 16) < 8
    g = -jnp.exp(alog_ref[...]) * jax.nn.softplus(ab + dtb_ref[...])
    gb_ref[...] = jnp.where(is_a, g, jax.nn.sigmoid(ab))


TRI_BASE = 16


def _mm(a, b):
    return jnp.dot(a.astype(BF16), b.astype(BF16), preferred_element_type=F32)


def _tri_inverse_many(lms, ri, ci):
    n = lms[0].shape[0]

    def same(s):
        shift = int(math.log2(s))
        return (ri >> shift) == (ci >> shift)

    eye = jnp.where(ri == ci, 1.0, 0.0)
    base = same(TRI_BASE)
    ms = [jnp.where(base, lm, 0.0) for lm in lms]
    ps = [eye - m for m in ms]
    for _ in range(int(math.log2(TRI_BASE)) - 1):
        ms = [_mm(m, m) for m in ms]
        ps = [p + _mm(p, m) for p, m in zip(ps, ms)]
    s = TRI_BASE
    while s < n:
        band = jnp.logical_and(same(2 * s), jnp.logical_not(same(s)))
        ts = [_mm(p, jnp.where(band, lm, 0.0)) for p, lm in zip(ps, lms)]
        ps = [p - _mm(t, p) for p, t in zip(ps, ts)]
        s *= 2
    return ps


def _gdn_prep(probs, ri, ci):
    c = probs[0][0].shape[0]
    nt_dims = (((1,), (1,)), ((), ()))
    incl = {False: ri >= ci, True: ri <= ci}
    strict = {False: ri > ci, True: ri < ci}
    decays, kbs, rhss, qgs, kdecs, glasts = [], [], [], [], [], []
    for q, k, v, gc, gct, beta, upper in probs:
        decays.append(jnp.exp(jnp.where(incl[upper], gc - gct, NEG)))
        kf = k.astype(F32)
        kb = kf * beta
        eg = jnp.exp(gc)
        g_last = gc[0:1, :] if upper else gc[c - 1:c, :]
        kbs.append(kb.astype(BF16))
        rhss.append(jnp.concatenate([v.astype(F32) * beta, kb * eg], axis=1).astype(BF16))
        qgs.append((q.astype(F32) * eg).astype(BF16))
        kdecs.append((kf * jnp.exp(g_last - gc)).astype(BF16))
        glasts.append(g_last)
    kks = [lax.dot_general(kb, p[1], nt_dims, preferred_element_type=F32) for kb, p in zip(kbs, probs)]
    qks = [lax.dot_general(p[0], p[1], nt_dims, preferred_element_type=F32) for p in probs]
    lowers = [jnp.where(strict[p[6]], kk * dec, 0.0) for kk, dec, p in zip(kks, decays, probs)]
    intras = [(qk * dec).astype(BF16) for qk, dec in zip(qks, decays)]
    tinvs = _tri_inverse_many(lowers, ri, ci)
    uws = [jnp.dot(t.astype(BF16), r, preferred_element_type=F32) for t, r in zip(tinvs, rhss)]
    return list(zip(uws, qgs, intras, kdecs, glasts))


def _gdn_advance(preps, states, dv):
    tn_dims = (((0,), (0,)), ((), ()))
    sbs = [st.astype(BF16) for st in states]
    wss = [jnp.dot(p[0][:, dv:].astype(BF16), sb, preferred_element_type=F32) for p, sb in zip(preps, sbs)]
    o1s = [jnp.dot(p[1], sb, preferred_element_type=F32) for p, sb in zip(preps, sbs)]
    v_news = [(p[0][:, :dv] - ws).astype(BF16) for p, ws in zip(preps, wss)]
    o2s = [jnp.dot(p[2], vn, preferred_element_type=F32) for p, vn in zip(preps, v_news)]
    upds = [lax.dot_general(p[3], vn, tn_dims, preferred_element_type=F32) for p, vn in zip(preps, v_news)]
    outs = [o1 + o2 for o1, o2 in zip(o1s, o2s)]
    new_states = [st * jnp.exp(p[4]) + upd for st, p, upd in zip(states, preps, upds)]
    return outs, new_states


GDN_CHUNKS_PER_STEP = 2


def _gdn_scan_kernel(qf_ref, kf_ref, vf_ref, gf_ref, qb_ref, kb_ref, vb_ref, gbk_ref,
                     of_ref, ob_ref, sf, sb, *, dk):
    s = pl.program_id(1)

    @pl.when(s == 0)
    def _():
        sf[...] = jnp.zeros_like(sf)
        sb[...] = jnp.zeros_like(sb)

    c = GDN_CHUNK
    cps = qf_ref.shape[0] // c
    ri = lax.broadcasted_iota(I32, (c, c), 0)
    ci = lax.broadcasted_iota(I32, (c, c), 1)
    tri_l = (ri >= ci).astype(F32)
    tri_u = (ri <= ci).astype(F32)
    dirs = ((qf_ref, kf_ref, vf_ref, gf_ref, of_ref, sf), (qb_ref, kb_ref, vb_ref, gbk_ref, ob_ref, sb))
    probs, sinks = [], []
    for t in range(cps):
        for upper, (q_ref, k_ref, v_ref, g_ref, o_ref, st) in enumerate(dirs):
            j = cps - 1 - t if upper else t
            rows = slice(j * c, (j + 1) * c)
            gbv = g_ref[rows, :]
            csum = jnp.dot(tri_u if upper else tri_l, gbv, precision=HIGHEST, preferred_element_type=F32)
            csum_t = csum.T
            base = 16 * upper
            for hh in range(HEADS):
                sl = slice(hh * dk, (hh + 1) * dk)
                probs.append((q_ref[rows, sl], k_ref[rows, sl], v_ref[rows, sl],
                              csum[:, base + hh:base + hh + 1], csum_t[base + hh:base + hh + 1, :],
                              gbv[:, base + 8 + hh:base + 9 + hh], bool(upper)))
                sinks.append((o_ref, rows, sl))
    preps = _gdn_prep(probs, ri, ci)
    per = 2 * HEADS
    states = [st[hh] for (_, _, _, _, _, st) in dirs for hh in range(HEADS)]
    for t in range(cps):
        outs, states = _gdn_advance(preps[t * per:(t + 1) * per], states, dk)
        for (o_ref, rows, sl), o in zip(sinks[t * per:(t + 1) * per], outs):
            o_ref[rows, sl] = o.astype(BF16)
    for i, (_, _, _, _, _, st) in enumerate(dirs):
        for hh in range(HEADS):
            st[hh] = states[i * HEADS + hh]


def _gdn_mixer(xall, mod_i, norm_gain, w_in, conv_w, a_log, dt_bias, dims):
    bsz, ctx_len, seq, d, tm = dims
    ltot = ctx_len + seq
    nt, nct = ltot // tm, ctx_len // tm
    n_tiles = bsz * nt
    rows = n_tiles * tm
    dk = d // HEADS
    n_in = w_in.shape[1]
    n_pad = -(-n_in // LANES) * LANES
    wp = jnp.pad(w_in, ((0, 0), (0, n_pad - n_in))).astype(BF16)
    zeros8 = jnp.zeros((2, HEADS), F32)
    lay = lambda t: jnp.pad(jnp.concatenate([t, zeros8], axis=1).reshape(1, 4 * HEADS),
                            ((0, 0), (0, LANES - 4 * HEADS)))
    mrow = _mod_row_map(nt, nct, bsz)
    hb = tm // SUBLANES
    last_hblk = rows // SUBLANES - 1
    const = lambda j: (0, 0)
    q, k, v, z, gb = pl.pallas_call(
        functools.partial(_gdn_proj_kernel, d=d, nt=nt, nct=nct, tm=tm, dk=dk),
        grid=(n_tiles,),
        in_specs=[pl.BlockSpec((SUBLANES, d), lambda j: (jnp.maximum(j * hb - 1, 0), 0)),
                  pl.BlockSpec((tm, d), lambda j: (j, 0)),
                  pl.BlockSpec((SUBLANES, d), lambda j: (jnp.minimum((j + 1) * hb, last_hblk), 0)),
                  pl.BlockSpec((None, 6, d), lambda j: (mrow(j), 0, 0)),
                  pl.BlockSpec((1, d), const),
                  pl.BlockSpec((d, n_pad), const),
                  pl.BlockSpec((GDN_CONV, 3 * d), const),
                  pl.BlockSpec((1, LANES), const),
                  pl.BlockSpec((1, LANES), const)],
        out_specs=[pl.BlockSpec((tm, d), lambda j: (j, 0))] * 4 + [pl.BlockSpec((tm, LANES), lambda j: (j, 0))],
        out_shape=[jax.ShapeDtypeStruct((rows, d), BF16)] * 4 + [jax.ShapeDtypeStruct((rows, LANES), F32)],
        scratch_shapes=[pltpu.VMEM((tm + 2 * SUBLANES, n_pad), F32)],
        compiler_params=_cparams("parallel"),
        name="gdn_proj",
    )(xall, xall, xall, mod_i, norm_gain.reshape(1, d), wp, conv_w, lay(a_log), lay(dt_bias))
    c = GDN_CHUNK * GDN_CHUNKS_PER_STEP
    assert ctx_len % c == 0 and ltot % c == 0
    ncl, ncc = ltot // c, ctx_len // c

    def fwd(b, s):
        return (b * ncl + s, 0)

    def bwd(b, s):
        return (b * ncl + jnp.where(s < ncc, ncc - 1 - s, ncl + ncc - 1 - s), 0)

    blk = lambda m: pl.BlockSpec((c, d), m)
    gblk = lambda m: pl.BlockSpec((c, LANES), m)
    o_f, o_b = pl.pallas_call(
        functools.partial(_gdn_scan_kernel, dk=dk),
        grid=(bsz, ncl),
        in_specs=[blk(fwd), blk(fwd), blk(fwd), gblk(fwd), blk(bwd), blk(bwd), blk(bwd), gblk(bwd)],
        out_specs=[blk(fwd), blk(bwd)],
        out_shape=[jax.ShapeDtypeStruct((rows, d), BF16)] * 2,
        scratch_shapes=[pltpu.VMEM((HEADS, dk, dk), F32), pltpu.VMEM((HEADS, dk, dk), F32)],
        compiler_params=_cparams("parallel", "arbitrary"),
        name="gdn_scan",
    )(q, k, v, gb, q, k, v, gb)
    return (o_f, o_b, z)


def _split_bf16(x):
    hi = x.astype(BF16)
    lo = (x - hi.astype(F32)).astype(BF16)
    return hi, lo


def _post_kernel(*refs, kind, d, tm, dk):
    if kind == 0:
        of_ref, ob_ref, z_ref, og_ref = refs[:4]
        refs = refs[4:]
    else:
        o_ref = refs[0]
        refs = refs[1:]
    (x_ref, mod_ref, wo_ref, g_ref, wrh_ref, wrl_ref, br_ref,
     xo_ref, h_ref, ids_ref, cnt_ref, base) = refs
    j = pl.program_id(0)

    @pl.when(j == 0)
    def _():
        base[...] = jnp.zeros_like(base)

    if kind == 0:
        parts = []
        for hh in range(d // dk):
            sl = slice(hh * dk, (hh + 1) * dk)
            o = of_ref[:, sl].astype(F32) + ob_ref[:, sl].astype(F32)
            o = o * lax.rsqrt(jnp.mean(o * o, axis=-1, keepdims=True) + EPS) * og_ref[...]
            parts.append((o * _silu(z_ref[:, sl].astype(F32))).astype(BF16))
        o_in = jnp.concatenate(parts, axis=1)
    else:
        o_in = o_ref[...]
    mod = mod_ref[...]
    x = x_ref[...] + mod[2:3, :] * jnp.dot(o_in, wo_ref[...], preferred_element_type=F32)
    xo_ref[...] = x
    h = _norm_mod(x, g_ref[...], mod[3:4, :], mod[4:5, :])
    h_ref[:, :d] = h
    hi, lo = _split_bf16(h)
    logits = (jnp.dot(hi, wrh_ref[...], preferred_element_type=F32)
              + jnp.dot(lo, wrh_ref[...], preferred_element_type=F32)
              + jnp.dot(hi, wrl_ref[...], preferred_element_type=F32)) + br_ref[...]
    lane = lax.broadcasted_iota(I32, logits.shape, 1)
    big = jnp.int32(1 << 20)
    is_g = lane < MOE_GROUPS
    gl = jnp.where(is_g, logits, NEG)
    gmax = jnp.max(gl, axis=-1, keepdims=True)
    gsel = jnp.min(jnp.where(gl == gmax, lane, big), axis=-1, keepdims=True)
    p_group = 1.0 / jnp.sum(jnp.where(is_g, jnp.exp(gl - gmax), 0.0), axis=-1, keepdims=True)
    in_grp = jnp.logical_and(lane >= MOE_GROUPS + gsel * MOE_PER_GROUP,
                             lane < MOE_GROUPS + (gsel + 1) * MOE_PER_GROUP)
    el = jnp.where(in_grp, logits, NEG)
    v0 = jnp.max(el, axis=-1, keepdims=True)
    i0 = jnp.min(jnp.where(el == v0, lane, big), axis=-1, keepdims=True)
    el1 = jnp.where(lane == i0, NEG, el)
    v1 = jnp.max(el1, axis=-1, keepdims=True)
    i1 = jnp.min(jnp.where(el1 == v1, lane, big), axis=-1, keepdims=True)
    e1 = jnp.exp(v1 - v0)
    w0 = p_group / (1.0 + e1)
    w1 = p_group * e1 / (1.0 + e1)
    a0 = i0 - MOE_GROUPS - gsel * MOE_PER_GROUP
    a1 = i1 - MOE_GROUPS - gsel * MOE_PER_GROUP
    lo_e = jnp.minimum(a0, a1)
    hi_e = jnp.maximum(a0, a1)
    lof = lo_e.astype(F32)
    pair = (lof * MOE_PER_GROUP - lof * (lof + 1.0) * 0.5).astype(I32) + (hi_e - lo_e - 1)
    cls = gsel * MOE_PAIRS + pair
    first_lo = a0 < a1
    g_lo = jnp.where(first_lo, w0, w1)
    g_hi = jnp.where(first_lo, w1, w0)
    oh = lane == cls
    onehot = jnp.where(oh, 1.0, 0.0)
    ri = lax.broadcasted_iota(I32, (tm, tm), 0)
    ci = lax.broadcasted_iota(I32, (tm, tm), 1)
    tri = jnp.where(ri > ci, 1.0, 0.0).astype(BF16)
    before = base[...] + jnp.dot(tri, onehot.astype(BF16), preferred_element_type=F32)
    rank = jnp.sum(jnp.where(oh, before, 0.0), axis=-1, keepdims=True)
    new_base = base[...] + jnp.sum(onehot, axis=0, keepdims=True)
    base[...] = new_base
    cnt_ref[...] = new_base
    ids = jnp.where(lane == 0, cls, jnp.where(lane == 1, rank.astype(I32), 0))
    ids_ref[...] = ids.T[0:SUBLANES, :]
    h_ref[:, d:] = jnp.where(lane == 0, g_lo, jnp.where(lane == 1, g_hi, 0.0))


def _post_mixer(kind, mixer_out, xall, mod_i, w_out, ffn_gain, w_group, b_group, w_expert, b_expert,
                dims, o_gain=None):
    bsz, ctx_len, seq, d, tm = dims
    ltot = ctx_len + seq
    nt, nct = ltot // tm, ctx_len // tm
    n_tiles = bsz * nt
    rows = n_tiles * tm
    dk = d // HEADS
    mrow = _mod_row_map(nt, nct, bsz)
    wr = jnp.pad(jnp.concatenate([w_group, w_expert], axis=1),
                 ((0, 0), (0, LANES - MOE_GROUPS - MOE_EXPERTS)))
    wr_hi = wr.astype(BF16)
    wr_lo = (wr - wr_hi.astype(F32)).astype(BF16)
    br = jnp.pad(jnp.concatenate([b_group, b_expert]), (0, LANES - MOE_GROUPS - MOE_EXPERTS)).reshape(1, LANES)
    const = lambda j: (0, 0)
    row = lambda j: (j, 0)
    lead_specs = [pl.BlockSpec((tm, d), row)] * len(mixer_out)
    lead_args = list(mixer_out)
    if kind == 0:
        lead_specs.append(pl.BlockSpec((1, dk), const))
        lead_args.append(o_gain.reshape(1, dk))
    n_lead = len(lead_args)
    outs = pl.pallas_call(
        functools.partial(_post_kernel, kind=kind, d=d, tm=tm, dk=dk),
        grid=(n_tiles,),
        in_specs=lead_specs + [pl.BlockSpec((tm, d), row),
                               pl.BlockSpec((None, 6, d), lambda j: (mrow(j), 0, 0)),
                               pl.BlockSpec((w_out.shape[0], d), const),
                               pl.BlockSpec((1, d), const),
                               pl.BlockSpec((d, LANES), const),
                               pl.BlockSpec((d, LANES), const),
                               pl.BlockSpec((1, LANES), const)],
        out_specs=[pl.BlockSpec((tm, d), row), pl.BlockSpec((tm, d + LANES), row),
                   pl.BlockSpec((SUBLANES, tm), lambda j: (0, j)),
                   pl.BlockSpec((1, LANES), const)],
        out_shape=[jax.ShapeDtypeStruct((rows, d), F32), jax.ShapeDtypeStruct((rows, d + LANES), F32),
                   jax.ShapeDtypeStruct((SUBLANES, rows), I32),
                   jax.ShapeDtypeStruct((1, LANES), F32)],
        scratch_shapes=[pltpu.VMEM((1, LANES), F32)],
        input_output_aliases={n_lead: 0},
        compiler_params=_cparams("arbitrary"),
        name="post_mixer",
    )(*lead_args, xall, mod_i, w_out.astype(BF16), ffn_gain.reshape(1, d), wr_hi, wr_lo, br)
    return outs


def _dispatch_kernel(zlo_ref, zhi_ref, nu_ref, dest_ref, h_ref, xs_ref, zblk, sem, zsem, *, tm, blk, n_blocks):
    j = pl.program_id(0)

    def row_copy(src, dst_row, s):
        return pltpu.make_async_copy(src, xs_ref.at[pl.ds(dst_row, 1)], s)

    def blk_copy(bi):
        return pltpu.make_async_copy(zblk, xs_ref.at[pl.ds(pl.multiple_of(bi * blk, blk), blk)], zsem)

    @pl.when(j == 0)
    def _():
        zblk[...] = jnp.zeros_like(zblk)

        def per_class(e, carry):
            lo, hi = zlo_ref[e], zhi_ref[e]

            def start(r, c):
                row_copy(zblk.at[pl.ds(0, 1)], r, zsem).start()
                return c

            def wait(r, c):
                row_copy(zblk.at[pl.ds(0, 1)], r, zsem).wait()
                return c

            lax.fori_loop(lo, hi, start, 0)
            lax.fori_loop(lo, hi, wait, 0)
            return carry

        lax.fori_loop(0, MOE_CLASSES, per_class, 0)

        def tail_start(bi, c):
            blk_copy(bi).start()
            return c

        def tail_wait(bi, c):
            blk_copy(bi).wait()
            return c

        lax.fori_loop(nu_ref[0], n_blocks, tail_start, 0)
        lax.fori_loop(nu_ref[0], n_blocks, tail_wait, 0)

    def body(r, c):
        row_copy(h_ref.at[pl.ds(r, 1)], dest_ref[0, r], sem).start()
        return c

    lax.fori_loop(0, tm, body, 0, unroll=8)
    pltpu.make_async_copy(h_ref, xs_ref.at[pl.ds(0, tm)], sem).wait()


def _expert_kernel(ea_ref, eb_ref, nu_ref, x_ref, w1a_ref, w3a_ref, w2a_ref, w1b_ref, w3b_ref, w2b_ref,
                   y_ref, *, d):
    j = pl.program_id(0)

    @pl.when(j < nu_ref[0])
    def _():
        x = x_ref[:, :d].astype(BF16)
        gates = x_ref[:, d:]

        def expert(w1_ref, w3_ref, w2_ref):
            a = jnp.dot(x, w1_ref[...], preferred_element_type=F32)
            b = jnp.dot(x, w3_ref[...], preferred_element_type=F32)
            return jnp.dot((_silu(a) * b).astype(BF16), w2_ref[...], preferred_element_type=F32)

        y_ref[...] = (expert(w1a_ref, w3a_ref, w2a_ref) * gates[:, 0:1]
                      + expert(w1b_ref, w3b_ref, w2b_ref) * gates[:, 1:2])

    @pl.when(j >= nu_ref[0])
    def _():
        y_ref[...] = jnp.zeros_like(y_ref)


def _combine_kernel(dest_ref, x_ref, mod_ref, fg_ref, yb_ref, xo_ref, ybuf, sem, *, tm, final):
    def body(r, c):
        pltpu.make_async_copy(yb_ref.at[pl.ds(dest_ref[0, r], 1)], ybuf.at[pl.ds(r, 1)], sem).start()
        return c

    lax.fori_loop(0, tm, body, 0, unroll=8)
    pltpu.make_async_copy(yb_ref.at[pl.ds(0, tm)], ybuf, sem).wait()
    x = x_ref[...] + mod_ref[5:6, :] * ybuf[...]
    if final:
        x = x * lax.rsqrt(jnp.mean(x * x, axis=-1, keepdims=True) + EPS) * fg_ref[...]
    xo_ref[...] = x


def _class_experts():
    lo, hi = [], []
    for g in range(MOE_GROUPS):
        for a in range(MOE_PER_GROUP):
            for b in range(a + 1, MOE_PER_GROUP):
                lo.append(g * MOE_PER_GROUP + a)
                hi.append(g * MOE_PER_GROUP + b)
    return jnp.asarray(lo, I32), jnp.asarray(hi, I32)


def _moe(xall, h, ids, counts, mod_i, w1, w3, w2, final_gain, dims, final):
    bsz, ctx_len, seq, d, tm = dims
    ltot = ctx_len + seq
    nt, nct = ltot // tm, ctx_len // tm
    n_tiles = bsz * nt
    rows = n_tiles * tm
    ncls = MOE_CLASSES
    blk = MOE_BLOCK
    dw = d + LANES
    n_blocks = -(-(rows + ncls * (blk - 1)) // blk)
    cnt = counts[0, :ncls].astype(I32)
    padded = (cnt + blk - 1) // blk * blk
    pad_end = jnp.cumsum(padded)
    pad_start = pad_end - padded
    cls_of_row = ids[0]
    start_of_row = jnp.sum(jnp.where(cls_of_row[:, None] == jnp.arange(ncls, dtype=I32)[None, :],
                                     pad_start[None, :], 0), axis=1)
    dest = (start_of_row + ids[1]).astype(I32).reshape(n_tiles, 1, tm)
    n_used = (pad_end[-1] // blk).astype(I32).reshape(1)
    blk_first = jnp.arange(n_blocks, dtype=I32) * blk
    blk_class = jnp.minimum(jnp.sum((pad_end[None, :] <= blk_first[:, None]).astype(I32), axis=1), ncls - 1)
    cls_lo, cls_hi = _class_experts()
    blk_lo, blk_hi = cls_lo[blk_class], cls_hi[blk_class]
    smem_dest = pl.BlockSpec((None, 1, tm), lambda j, *_: (j, 0, 0), memory_space=pltpu.SMEM)
    xs = pl.pallas_call(
        functools.partial(_dispatch_kernel, tm=tm, blk=blk, n_blocks=n_blocks),
        grid_spec=pltpu.PrefetchScalarGridSpec(
            num_scalar_prefetch=3, grid=(n_tiles,),
            in_specs=[smem_dest, pl.BlockSpec((tm, dw), lambda j, *_: (j, 0))],
            out_specs=pl.BlockSpec(memory_space=pl.ANY),
            scratch_shapes=[pltpu.VMEM((blk, dw), F32), pltpu.SemaphoreType.DMA,
                            pltpu.SemaphoreType.DMA]),
        out_shape=jax.ShapeDtypeStruct((n_blocks * blk, dw), F32),
        compiler_params=_cparams("arbitrary"),
        name="moe_dispatch",
    )((pad_start + cnt).astype(I32), pad_end.astype(I32), n_used, dest, h)

    def xmap(j, ea, eb, nu):
        return (jnp.minimum(j, nu[0] - 1), 0)

    def wmap_lo(j, ea, eb, nu):
        return (ea[jnp.minimum(j, nu[0] - 1)], 0, 0)

    def wmap_hi(j, ea, eb, nu):
        return (eb[jnp.minimum(j, nu[0] - 1)], 0, 0)

    f = w1.shape[-1]
    w1b, w3b, w2b = w1.astype(BF16), w3.astype(BF16), w2.astype(BF16)
    yb = pl.pallas_call(
        functools.partial(_expert_kernel, d=d),
        grid_spec=pltpu.PrefetchScalarGridSpec(
            num_scalar_prefetch=3, grid=(n_blocks,),
            in_specs=[pl.BlockSpec((blk, dw), xmap),
                      pl.BlockSpec((None, d, f), wmap_lo),
                      pl.BlockSpec((None, d, f), wmap_lo),
                      pl.BlockSpec((None, f, d), wmap_lo),
                      pl.BlockSpec((None, d, f), wmap_hi),
                      pl.BlockSpec((None, d, f), wmap_hi),
                      pl.BlockSpec((None, f, d), wmap_hi)],
            out_specs=pl.BlockSpec((blk, d), lambda j, ea, eb, nu: (j, 0))),
        out_shape=jax.ShapeDtypeStruct((n_blocks * blk, d), F32),
        compiler_params=_cparams("arbitrary"),
        name="moe_experts",
    )(blk_lo, blk_hi, n_used, xs, w1b, w3b, w2b, w1b, w3b, w2b)

    if final:
        nlt = seq // tm
        grid = (bsz, nlt)
        tile = lambda b, i: b * nt + nct + i
        mod_map = lambda b, i: (b, 0, 0)
        out_map = lambda b, i: (b * nlt + i, 0)
        out_rows, aliases = bsz * seq, {}
    else:
        mrow = _mod_row_map(nt, nct, bsz)
        grid = (n_tiles,)
        tile = lambda j: j
        mod_map = lambda j: (mrow(j), 0, 0)
        out_map = lambda j: (j, 0)
        out_rows, aliases = rows, {1: 0}
    out = pl.pallas_call(
        functools.partial(_combine_kernel, tm=tm, final=final),
        grid=grid,
        in_specs=[pl.BlockSpec((None, 1, tm), lambda *g: (tile(*g), 0, 0), memory_space=pltpu.SMEM),
                  pl.BlockSpec((tm, d), lambda *g: (tile(*g), 0)),
                  pl.BlockSpec((None, 6, d), mod_map),
                  pl.BlockSpec((1, d), lambda *g: (0, 0)),
                  pl.BlockSpec(memory_space=pl.ANY)],
        out_specs=pl.BlockSpec((tm, d), out_map),
        out_shape=jax.ShapeDtypeStruct((out_rows, d), F32),
        scratch_shapes=[pltpu.VMEM((tm, d), F32), pltpu.SemaphoreType.DMA],
        input_output_aliases=aliases,
        compiler_params=_cparams(*(("arbitrary",) * len(grid))),
        name="moe_combine",
    )(dest, xall, mod_i, final_gain.reshape(1, d), yb)
    return out


def kernel(x, c, ctx, c_ctx, w_mod, b_mod, norm_mix, norm_ffn, gdn_w_in, gdn_conv, gdn_a_log, gdn_dt_bias, gdn_norm, gdn_w_out, diff_w_qkv, diff_lambda, diff_norm, diff_w_out, mla_w_down, mla_q_norm, mla_kv_norm, mla_w_uq, mla_w_ukv, mla_w_out, moe_w_group, moe_b_group, moe_w_expert, moe_b_expert, moe_w1, moe_w3, moe_w2, final_norm):
    bsz, seq, d = x.shape
    ctx_len = ctx.shape[1]
    depth = w_mod.shape[0]
    tm = _row_tile(ctx_len)
    assert d % LANES == 0 and d // HEADS == LANES
    assert ctx_len % tm == 0 and seq % tm == 0 and ctx_len % GDN_CHUNK == 0 and seq % GDN_CHUNK == 0
    dims = (bsz, ctx_len, seq, d, tm)
    ltot = ctx_len + seq
    xall = jnp.concatenate([ctx, x], axis=1).reshape(bsz * ltot, d)
    mod = _mod_vectors(c, c_ctx, w_mod, b_mod)
    tables = _rope_tables(seq, ctx_len)
    for i in range(depth):
        kind, j = i % N_MIXERS, i // N_MIXERS
        if kind == 0:
            mixer_out = _gdn_mixer(xall, mod[i], norm_mix[i], gdn_w_in[j], gdn_conv[j], gdn_a_log[j],
                                   gdn_dt_bias[j], dims)
            w_out, o_gain = gdn_w_out[j], gdn_norm[j]
        elif kind == 1:
            lam_init = 0.8 - 0.6 * math.exp(-0.3 * i)
            mixer_out = _diff_mixer(xall, mod[i], norm_mix[i], diff_w_qkv[j], diff_lambda[j], diff_norm[j],
                                    dims, tables, lam_init)
            w_out, o_gain = diff_w_out[j], None
        else:
            mixer_out = _mla_mixer(xall, mod[i], norm_mix[i], mla_w_down[j], mla_q_norm[j], mla_kv_norm[j],
                                   mla_w_uq[j], mla_w_ukv[j], dims, tables)
            w_out, o_gain = mla_w_out[j], None
        xall, h, ids, counts = _post_mixer(kind, mixer_out, xall, mod[i], w_out, norm_ffn[i],
                                           moe_w_group[i], moe_b_group[i], moe_w_expert[i],
                                           moe_b_expert[i], dims, o_gain)
        xall = _moe(xall, h, ids, counts, mod[i], moe_w1[i], moe_w3[i], moe_w2[i], final_norm, dims,
                    final=(i == depth - 1))
    return xall.reshape(bsz, seq, d)
```

```python
import functools
import math

import jax
import jax.numpy as jnp
from jax import lax
from jax.experimental import pallas as pl
from jax.experimental.pallas import tpu as pltpu

F32 = jnp.float32
BF16 = jnp.bfloat16
I32 = jnp.int32
HIGHEST = lax.Precision.HIGHEST

LANES = 128
SUBLANES = 8
VMEM_LIMIT = 56 * 1024 * 1024

EPS = 1e-6
GRID_W = 64
ROPE_THETA = 10000.0
N_MIXERS = 3
HEADS = 8
GDN_CONV = 5
GDN_CHUNK = 64
DIFF_SUBLN_EPS = 1e-5
MLA_NOPE = 128
MLA_ROPE = 64
MLA_V = 128
MOE_GROUPS = 4
MOE_PER_GROUP = 8
MOE_EXPERTS = MOE_GROUPS * MOE_PER_GROUP
MOE_TOP_K = 2
MOE_PAIRS = MOE_PER_GROUP * (MOE_PER_GROUP - 1) // 2
MOE_CLASSES = MOE_GROUPS * MOE_PAIRS
MOE_BLOCK = 256
NEG = -1e30


def _cparams(*sem):
    return pltpu.CompilerParams(dimension_semantics=sem, vmem_limit_bytes=VMEM_LIMIT)


def _row_tile(ctx_len):
    return 256 if ctx_len % 256 == 0 else 128


def _mod_row_map(nt, nct, bsz):
    def f(j):
        return jnp.where(j % nt < nct, bsz, j // nt)
    return f


def _norm_mod(x, gain, shift, scale, eps=EPS):
    var = jnp.mean(x * x, axis=-1, keepdims=True)
    y = x * lax.rsqrt(var + eps) * gain
    return y * (1.0 + scale) + shift


def _silu(x):
    return x * jax.nn.sigmoid(x)


def _mod_kernel(c_ref, w_ref, b_ref, o_ref):
    s = _silu(c_ref[...])
    o_ref[...] = jnp.dot(s, w_ref[...], precision=HIGHEST, preferred_element_type=F32) + b_ref[...]


def _mod_vectors(c, c_ctx, w_mod, b_mod):
    depth, d, n = w_mod.shape
    bsz = c.shape[0]
    rows = -(-(bsz + 1) // SUBLANES) * SUBLANES
    cc = jnp.zeros((rows, d), F32).at[:bsz].set(c).at[bsz].set(c_ctx)
    tn = 512
    out = pl.pallas_call(
        _mod_kernel,
        grid=(depth, n // tn),
        in_specs=[pl.BlockSpec((rows, d), lambda i, j: (0, 0)),
                  pl.BlockSpec((None, d, tn), lambda i, j: (i, 0, j)),
                  pl.BlockSpec((None, 1, tn), lambda i, j: (i, 0, j))],
        out_specs=pl.BlockSpec((None, rows, tn), lambda i, j: (i, 0, j)),
        out_shape=jax.ShapeDtypeStruct((depth, rows, n), F32),
        compiler_params=_cparams("parallel", "parallel"),
        name="mod_vectors",
    )(cc, w_mod, b_mod.reshape(depth, 1, n))
    return out.reshape(depth, rows, 6, d)


def _rope_tables(seq, ctx_len):
    quarter = 16
    inv_freq = ROPE_THETA ** (-jnp.arange(quarter, dtype=F32) / quarter)
    t = jnp.arange(seq)
    row = (t // GRID_W).astype(F32)[:, None] * inv_freq
    col = (t % GRID_W).astype(F32)[:, None] * inv_freq
    cos = jnp.concatenate([jnp.cos(row), jnp.cos(row), jnp.cos(col), jnp.cos(col)], axis=1)
    sin = jnp.concatenate([-jnp.sin(row), jnp.sin(row), -jnp.sin(col), jnp.sin(col)], axis=1)
    cos = jnp.concatenate([jnp.ones((ctx_len, 64), F32), cos], axis=0)
    sin = jnp.concatenate([jnp.zeros((ctx_len, 64), F32), sin], axis=0)
    return jnp.tile(cos, (1, 2)), jnp.tile(sin, (1, 2))


def _rope128(blk, cos, sin):
    lane = lax.broadcasted_iota(I32, blk.shape, 1)
    first = (lane % 32) < 16
    partner = jnp.where(first, pltpu.roll(blk, LANES - 16, 1), pltpu.roll(blk, 16, 1))
    return blk * cos + partner * sin


def _diff_proj_kernel(x_ref, mod_ref, g_ref, w_ref, cos_ref, sin_ref, o_ref, vt_ref, *, d, q_scale):
    h = _norm_mod(x_ref[...], g_ref[...], mod_ref[0:1, :], mod_ref[1:2, :])
    p = jnp.dot(h.astype(BF16), w_ref[...], preferred_element_type=F32)
    cos = cos_ref[...]
    sin = sin_ref[...]
    nqk = 2 * d // LANES
    for cb in range(nqk):
        r = _rope128(p[:, cb * LANES:(cb + 1) * LANES], cos, sin)
        if cb < nqk // 2:
            r = r * q_scale
        o_ref[:, cb * LANES:(cb + 1) * LANES] = r.astype(BF16)
    vt_ref[...] = p[:, 2 * d:].T.astype(BF16)


def _pick_tk(n):
    for cand in (768, 512, 384, 256, 128):
        if n % cand == 0:
            return cand
    raise ValueError(n)


def _flash_t(streams, k_ref, vt_ref, m_ref, l_ref, acc_ref, nsteps, tk):
    for s in range(len(streams)):
        m_ref[s] = jnp.full(m_ref.shape[1:], NEG, F32)
        l_ref[s] = jnp.zeros(l_ref.shape[1:], F32)
        acc_ref[s] = jnp.zeros(acc_ref.shape[1:], F32)
    nt_dims = (((1,), (1,)), ((), ()))

    def scores(i):
        return [lax.dot_general(k_ref[i * tk:(i + 1) * tk, kc], q, nt_dims, preferred_element_type=F32)
                for q, kc, _ in streams]

    sts = scores(0)
    for i in range(nsteps):
        nxt = scores(i + 1) if i + 1 < nsteps else None
        alphas, ps = [], []
        for s, st in enumerate(sts):
            m_old = m_ref[s]
            m_new = jnp.maximum(m_old, jnp.max(st, axis=0, keepdims=True))
            alpha = jnp.exp2(m_old - m_new)
            p = jnp.exp2(st - m_new)
            l_ref[s] = alpha * l_ref[s] + jnp.sum(p, axis=0, keepdims=True)
            m_ref[s] = m_new
            alphas.append(alpha)
            ps.append(p.astype(BF16))
        pvs = [jnp.dot(vt_ref[vr, i * tk:(i + 1) * tk], p, preferred_element_type=F32)
               for (_, _, vr), p in zip(streams, ps)]
        for s in range(len(streams)):
            acc_ref[s] = alphas[s] * acc_ref[s] + pvs[s]
        sts = nxt


def _flash_ctx_or_all(streams, k_ref, vt_ref, m_ref, l_ref, acc_ref, nct, ctx_len, ltot):
    i = pl.program_id(2)
    tk_c, tk_l = _pick_tk(ctx_len), _pick_tk(ltot)

    @pl.when(i < nct)
    def _():
        _flash_t(streams, k_ref, vt_ref, m_ref, l_ref, acc_ref, ctx_len // tk_c, tk_c)

    @pl.when(i >= nct)
    def _():
        _flash_t(streams, k_ref, vt_ref, m_ref, l_ref, acc_ref, ltot // tk_l, tk_l)


ATTN_HEADS_PER_STEP = 2


def _diff_attn_kernel(lam_ref, gain_ref, q_ref, k_ref, vt_ref, o_ref, m_ref, l_ref, acc_ref, *, nct,
                      ctx_len, ltot, lam_init, hw):
    streams = []
    for g in range(ATTN_HEADS_PER_STEP):
        cols = slice(g * hw, (g + 1) * hw)
        q = q_ref[:, cols]
        lane = lax.broadcasted_iota(I32, q.shape, 1)
        zero = jnp.zeros_like(q)
        streams.append((jnp.where(lane < hw // 2, q, zero), cols, cols))
        streams.append((jnp.where(lane >= hw // 2, q, zero), cols, cols))
    _flash_ctx_or_all(streams, k_ref, vt_ref, m_ref, l_ref, acc_ref, nct, ctx_len, ltot)
    lv = lam_ref[...]
    lam = (jnp.exp(jnp.sum(lv[0:1] * lv[1:2], keepdims=True))
           - jnp.exp(jnp.sum(lv[2:3] * lv[3:4], keepdims=True)) + lam_init)
    for g in range(ATTN_HEADS_PER_STEP):
        o = acc_ref[2 * g] / l_ref[2 * g] - lam * (acc_ref[2 * g + 1] / l_ref[2 * g + 1])
        var = jnp.mean(o * o, axis=0, keepdims=True)
        o = o * lax.rsqrt(var + DIFF_SUBLN_EPS) * gain_ref[...] * (1.0 - lam_init)
        o_ref[:, g * hw:(g + 1) * hw] = o.T.astype(BF16)


def _diff_mixer(xall, mod_i, norm_gain, w_qkv, lam_vec, sub_gain, dims, tables, lam_init):
    bsz, ctx_len, seq, d, tm = dims
    ltot = ctx_len + seq
    nt, nct = ltot // tm, ctx_len // tm
    n_tiles = bsz * nt
    dh = d // HEADS // 2
    cos, sin = tables
    mrow = _mod_row_map(nt, nct, bsz)
    rows = n_tiles * tm
    qk, vt = pl.pallas_call(
        functools.partial(_diff_proj_kernel, d=d, q_scale=dh ** -0.5 * math.log2(math.e)),
        grid=(n_tiles,),
        in_specs=[pl.BlockSpec((tm, d), lambda j: (j, 0)),
                  pl.BlockSpec((None, 6, d), lambda j: (mrow(j), 0, 0)),
                  pl.BlockSpec((1, d), lambda j: (0, 0)),
                  pl.BlockSpec((d, 3 * d), lambda j: (0, 0)),
                  pl.BlockSpec((tm, LANES), lambda j: (j % nt, 0)),
                  pl.BlockSpec((tm, LANES), lambda j: (j % nt, 0))],
        out_specs=[pl.BlockSpec((tm, 2 * d), lambda j: (j, 0)),
                   pl.BlockSpec((d, tm), lambda j: (0, j))],
        out_shape=[jax.ShapeDtypeStruct((rows, 2 * d), BF16), jax.ShapeDtypeStruct((d, rows), BF16)],
        compiler_params=_cparams("parallel"),
        name="diff_proj",
    )(xall, mod_i, norm_gain.reshape(1, d), w_qkv.astype(BF16), cos, sin)
    hw = 2 * dh
    hps = ATTN_HEADS_PER_STEP
    hg = HEADS // hps
    o = pl.pallas_call(
        functools.partial(_diff_attn_kernel, nct=nct, ctx_len=ctx_len, ltot=ltot, lam_init=lam_init, hw=hw),
        grid=(bsz, hg, nt),
        in_specs=[pl.BlockSpec((4, dh), lambda b, h, i: (0, 0)),
                  pl.BlockSpec((hw, 1), lambda b, h, i: (0, 0)),
                  pl.BlockSpec((tm, hps * hw), lambda b, h, i: (b * nt + i, h)),
                  pl.BlockSpec((ltot, hps * hw), lambda b, h, i: (b, hg + h)),
                  pl.BlockSpec((hps * hw, ltot), lambda b, h, i: (h, b))],
        out_specs=pl.BlockSpec((tm, hps * hw), lambda b, h, i: (b * nt + i, h)),
        out_shape=jax.ShapeDtypeStruct((rows, d), BF16),
        scratch_shapes=[pltpu.VMEM((2 * hps, 1, tm), F32), pltpu.VMEM((2 * hps, 1, tm), F32),
                        pltpu.VMEM((2 * hps, hw, tm), F32)],
        compiler_params=_cparams("parallel", "parallel", "arbitrary"),
        name="diff_attn",
    )(lam_vec, sub_gain.reshape(hw, 1), qk, qk, vt)
    return (o,)


def _mla_proj_kernel(x_ref, mod_ref, g_ref, wd_ref, qg_ref, kvg_ref, wq_ref, wkv_ref, cos_ref, sin_ref,
                     q_ref, k_ref, v_ref, *, q_lora, kv_lora, scale):
    h = _norm_mod(x_ref[...], g_ref[...], mod_ref[0:1, :], mod_ref[1:2, :])
    p = jnp.dot(h.astype(BF16), wd_ref[...], preferred_element_type=F32)
    cq = p[:, :q_lora]
    cq = cq * lax.rsqrt(jnp.mean(cq * cq, axis=-1, keepdims=True) + EPS) * qg_ref[...]
    ckv = p[:, q_lora:q_lora + kv_lora]
    ckv = ckv * lax.rsqrt(jnp.mean(ckv * ckv, axis=-1, keepdims=True) + EPS) * kvg_ref[...]
    cos = cos_ref[...]
    sin = sin_ref[...]
    kr = _rope128(p[:, q_lora + kv_lora:], cos, sin).astype(BF16)
    q = jnp.dot(cq.astype(BF16), wq_ref[...], preferred_element_type=F32)
    kv = jnp.dot(ckv.astype(BF16), wkv_ref[...], preferred_element_type=F32)
    hq = MLA_NOPE + LANES
    for hh in range(HEADS):
        q_ref[:, hh * hq:hh * hq + MLA_NOPE] = (q[:, hh * hq:hh * hq + MLA_NOPE] * scale).astype(BF16)
        qr = _rope128(q[:, hh * hq + MLA_NOPE:(hh + 1) * hq], cos, sin) * scale
        q_ref[:, hh * hq + MLA_NOPE:(hh + 1) * hq] = qr.astype(BF16)
        k_ref[:, hh * hq:hh * hq + MLA_NOPE] = kv[:, hh * MLA_NOPE:(hh + 1) * MLA_NOPE].astype(BF16)
        k_ref[:, hh * hq + MLA_NOPE:(hh + 1) * hq] = kr
    v_ref[...] = kv[:, HEADS * MLA_NOPE:].T.astype(BF16)


def _mla_attn_kernel(q_ref, k_ref, vt_ref, o_ref, m_ref, l_ref, acc_ref, *, nct, ctx_len, ltot, hq):
    streams = [(q_ref[:, g * hq:(g + 1) * hq], slice(g * hq, (g + 1) * hq), slice(g * MLA_V, (g + 1) * MLA_V))
               for g in range(ATTN_HEADS_PER_STEP)]
    _flash_ctx_or_all(streams, k_ref, vt_ref, m_ref, l_ref, acc_ref, nct, ctx_len, ltot)
    for g in range(ATTN_HEADS_PER_STEP):
        o_ref[:, g * MLA_V:(g + 1) * MLA_V] = (acc_ref[g] / l_ref[g]).T.astype(BF16)


def _mla_mixer(xall, mod_i, norm_gain, w_down, q_gain, kv_gain, w_uq, w_ukv, dims, tables):
    bsz, ctx_len, seq, d, tm = dims
    ltot = ctx_len + seq
    nt, nct = ltot // tm, ctx_len // tm
    n_tiles = bsz * nt
    rows = n_tiles * tm
    q_lora, kv_lora = q_gain.shape[0], kv_gain.shape[0]
    cos, sin = tables
    mrow = _mod_row_map(nt, nct, bsz)
    hq = MLA_NOPE + LANES
    wd = jnp.pad(w_down, ((0, 0), (0, LANES - MLA_ROPE))).astype(BF16)
    nd = wd.shape[1]
    wq = jnp.pad(w_uq.reshape(q_lora, HEADS, MLA_NOPE + MLA_ROPE),
                 ((0, 0), (0, 0), (0, LANES - MLA_ROPE))).reshape(q_lora, HEADS * hq).astype(BF16)
    wkv = w_ukv.reshape(kv_lora, HEADS, MLA_NOPE + MLA_V)
    wkv = jnp.concatenate([wkv[:, :, :MLA_NOPE].reshape(kv_lora, HEADS * MLA_NOPE),
                           wkv[:, :, MLA_NOPE:].reshape(kv_lora, HEADS * MLA_V)], axis=1).astype(BF16)
    scale = (MLA_NOPE + MLA_ROPE) ** -0.5 * math.log2(math.e)
    const = lambda j: (0, 0)
    q, k, v = pl.pallas_call(
        functools.partial(_mla_proj_kernel, q_lora=q_lora, kv_lora=kv_lora, scale=scale),
        grid=(n_tiles,),
        in_specs=[pl.BlockSpec((tm, d), lambda j: (j, 0)),
                  pl.BlockSpec((None, 6, d), lambda j: (mrow(j), 0, 0)),
                  pl.BlockSpec((1, d), const),
                  pl.BlockSpec((d, nd), const),
                  pl.BlockSpec((1, q_lora), const),
                  pl.BlockSpec((1, kv_lora), const),
                  pl.BlockSpec((q_lora, HEADS * hq), const),
                  pl.BlockSpec((kv_lora, HEADS * (MLA_NOPE + MLA_V)), const),
                  pl.BlockSpec((tm, LANES), lambda j: (j % nt, 0)),
                  pl.BlockSpec((tm, LANES), lambda j: (j % nt, 0))],
        out_specs=[pl.BlockSpec((tm, HEADS * hq), lambda j: (j, 0)),
                   pl.BlockSpec((tm, HEADS * hq), lambda j: (j, 0)),
                   pl.BlockSpec((HEADS * MLA_V, tm), lambda j: (0, j))],
        out_shape=[jax.ShapeDtypeStruct((rows, HEADS * hq), BF16),
                   jax.ShapeDtypeStruct((rows, HEADS * hq), BF16),
                   jax.ShapeDtypeStruct((HEADS * MLA_V, rows), BF16)],
        compiler_params=_cparams("parallel"),
        name="mla_proj",
    )(xall, mod_i, norm_gain.reshape(1, d), wd, q_gain.reshape(1, q_lora), kv_gain.reshape(1, kv_lora),
      wq, wkv, cos, sin)
    hps = ATTN_HEADS_PER_STEP
    o = pl.pallas_call(
        functools.partial(_mla_attn_kernel, nct=nct, ctx_len=ctx_len, ltot=ltot, hq=hq),
        grid=(bsz, HEADS // hps, nt),
        in_specs=[pl.BlockSpec((tm, hps * hq), lambda b, h, i: (b * nt + i, h)),
                  pl.BlockSpec((ltot, hps * hq), lambda b, h, i: (b, h)),
                  pl.BlockSpec((hps * MLA_V, ltot), lambda b, h, i: (h, b))],
        out_specs=pl.BlockSpec((tm, hps * MLA_V), lambda b, h, i: (b * nt + i, h)),
        out_shape=jax.ShapeDtypeStruct((rows, HEADS * MLA_V), BF16),
        scratch_shapes=[pltpu.VMEM((hps, 1, tm), F32), pltpu.VMEM((hps, 1, tm), F32),
                        pltpu.VMEM((hps, MLA_V, tm), F32)],
        compiler_params=_cparams("parallel", "parallel", "arbitrary"),
        name="mla_attn",
    )(q, k, v)
    return (o,)


def _gdn_proj_kernel(xp_ref, x_ref, xn_ref, mod_ref, g_ref, w_ref, cw_ref, alog_ref, dtb_ref,
                     q_ref, k_ref, v_ref, z_ref, gb_ref, pbuf, *, d, nt, nct, tm, dk):
    j = pl.program_id(0)
    r = j % nt
    first = jnp.logical_or(r == 0, r == nct)
    last = jnp.logical_or(r == nct - 1, r == nt - 1)
    halo = SUBLANES
    xe = jnp.concatenate([xp_ref[...], x_ref[...], xn_ref[...]], axis=0)
    h = _norm_mod(xe, g_ref[...], mod_ref[0:1, :], mod_ref[1:2, :])
    rid = lax.broadcasted_iota(I32, (tm + 2 * halo, 1), 0)
    keep = jnp.logical_and(jnp.logical_or(rid >= halo, jnp.logical_not(first)),
                           jnp.logical_or(rid < tm + halo, jnp.logical_not(last)))
    h = jnp.where(keep, h, 0.0)
    pbuf[...] = jnp.dot(h.astype(BF16), w_ref[...], preferred_element_type=F32)
    half = GDN_CONV // 2

    def conv_block(c0):
        acc = None
        for t in range(GDN_CONV):
            term = pbuf[pl.ds(halo - half + t, tm), pl.ds(c0, LANES)] * cw_ref[t:t + 1, pl.ds(c0, LANES)]
            acc = term if acc is None else acc + term
        return _silu(acc)

    for hh in range(3 * d // LANES):
        c0 = hh * LANES
        blk = conv_block(c0)
        if hh < 2 * d // LANES:
            blk = blk * lax.rsqrt(jnp.sum(blk * blk, axis=-1, keepdims=True) + EPS)
        if hh < d // LANES:
            q_ref[:, c0:c0 + LANES] = (blk * dk ** -0.5).astype(BF16)
        elif hh < 2 * d // LANES:
            k_ref[:, c0 - d:c0 - d + LANES] = blk.astype(BF16)
        else:
            v_ref[:, c0 - 2 * d:c0 - 2 * d + LANES] = blk.astype(BF16)
    z_ref[...] = pbuf[halo:halo + tm, 3 * d:4 * d].astype(BF16)
    ab = pbuf[halo:halo + tm, 4 * d:4 * d + LANES]
    lane = lax.broadcasted_iota(I32, ab.shape, 1)
    is_a = (lane % 16) < 8
    g = -jnp.exp(alog_ref[...]) * jax.nn.softplus(ab + dtb_ref[...])
    gb_ref[...] = jnp.where(is_a, g, jax.nn.sigmoid(ab))


TRI_BASE = 16


def _mm(a, b):
    return jnp.dot(a.astype(BF16), b.astype(BF16), preferred_element_type=F32)


def _tri_inverse_many(lms, ri, ci):
    n = lms[0].shape[0]

    def same(s):
        shift = int(math.log2(s))
        return (ri >> shift) == (ci >> shift)

    eye = jnp.where(ri == ci, 1.0, 0.0)
    base = same(TRI_BASE)
    ms = [jnp.where(base, lm, 0.0) for lm in lms]
    ps = [eye - m for m in ms]
    for _ in range(int(math.log2(TRI_BASE)) - 1):
        ms = [_mm(m, m) for m in ms]
        ps = [p + _mm(p, m) for p, m in zip(ps, ms)]
    s = TRI_BASE
    while s < n:
        band = jnp.logical_and(same(2 * s), jnp.logical_not(same(s)))
        ts = [_mm(p, jnp.where(band, lm, 0.0)) for p, lm in zip(ps, lms)]
        ps = [p - _mm(t, p) for p, t in zip(ps, ts)]
        s *= 2
    return ps


def _gdn_prep(probs, ri, ci):
    c = probs[0][0].shape[0]
    nt_dims = (((1,), (1,)), ((), ()))
    incl = {False: ri >= ci, True: ri <= ci}
    strict = {False: ri > ci, True: ri < ci}
    decays, kbs, rhss, qgs, kdecs, glasts = [], [], [], [], [], []
    for q, k, v, gc, gct, beta, upper in probs:
        decays.append(jnp.exp(jnp.where(incl[upper], gc - gct, NEG)))
        kf = k.astype(F32)
        kb = kf * beta
        eg = jnp.exp(gc)
        g_last = gc[0:1, :] if upper else gc[c - 1:c, :]
        kbs.append(kb.astype(BF16))
        rhss.append(jnp.concatenate([v.astype(F32) * beta, kb * eg], axis=1).astype(BF16))
        qgs.append((q.astype(F32) * eg).astype(BF16))
        kdecs.append((kf * jnp.exp(g_last - gc)).astype(BF16))
        glasts.append(g_last)
    kks = [lax.dot_general(kb, p[1], nt_dims, preferred_element_type=F32) for kb, p in zip(kbs, probs)]
    qks = [lax.dot_general(p[0], p[1], nt_dims, preferred_element_type=F32) for p in probs]
    lowers = [jnp.where(strict[p[6]], kk * dec, 0.0) for kk, dec, p in zip(kks, decays, probs)]
    intras = [(qk * dec).astype(BF16) for qk, dec in zip(qks, decays)]
    tinvs = _tri_inverse_many(lowers, ri, ci)
    uws = [jnp.dot(t.astype(BF16), r, preferred_element_type=F32) for t, r in zip(tinvs, rhss)]
    return list(zip(uws, qgs, intras, kdecs, glasts))


def _gdn_advance(preps, states, dv):
    tn_dims = (((0,), (0,)), ((), ()))
    sbs = [st.astype(BF16) for st in states]
    wss = [jnp.dot(p[0][:, dv:].astype(BF16), sb, preferred_element_type=F32) for p, sb in zip(preps, sbs)]
    o1s = [jnp.dot(p[1], sb, preferred_element_type=F32) for p, sb in zip(preps, sbs)]
    v_news = [(p[0][:, :dv] - ws).astype(BF16) for p, ws in zip(preps, wss)]
    o2s = [jnp.dot(p[2], vn, preferred_element_type=F32) for p, vn in zip(preps, v_news)]
    upds = [lax.dot_general(p[3], vn, tn_dims, preferred_element_type=F32) for p, vn in zip(preps, v_news)]
    outs = [o1 + o2 for o1, o2 in zip(o1s, o2s)]
    new_states = [st * jnp.exp(p[4]) + upd for st, p, upd in zip(states, preps, upds)]
    return outs, new_states


GDN_CHUNKS_PER_STEP = 2


def _gdn_scan_kernel(qf_ref, kf_ref, vf_ref, gf_ref, qb_ref, kb_ref, vb_ref, gbk_ref,
                     of_ref, ob_ref, sf, sb, *, dk):
    s = pl.program_id(1)

    @pl.when(s == 0)
    def _():
        sf[...] = jnp.zeros_like(sf)
        sb[...] = jnp.zeros_like(sb)

    c = GDN_CHUNK
    cps = qf_ref.shape[0] // c
    ri = lax.broadcasted_iota(I32, (c, c), 0)
    ci = lax.broadcasted_iota(I32, (c, c), 1)
    tri_l = (ri >= ci).astype(F32)
    tri_u = (ri <= ci).astype(F32)
    dirs = ((qf_ref, kf_ref, vf_ref, gf_ref, of_ref, sf), (qb_ref, kb_ref, vb_ref, gbk_ref, ob_ref, sb))
    probs, sinks = [], []
    for t in range(cps):
        for upper, (q_ref, k_ref, v_ref, g_ref, o_ref, st) in enumerate(dirs):
            j = cps - 1 - t if upper else t
            rows = slice(j * c, (j + 1) * c)
            gbv = g_ref[rows, :]
            csum = jnp.dot(tri_u if upper else tri_l, gbv, precision=HIGHEST, preferred_element_type=F32)
            csum_t = csum.T
            base = 16 * upper
            for hh in range(HEADS):
                sl = slice(hh * dk, (hh + 1) * dk)
                probs.append((q_ref[rows, sl], k_ref[rows, sl], v_ref[rows, sl],
                              csum[:, base + hh:base + hh + 1], csum_t[base + hh:base + hh + 1, :],
                              gbv[:, base + 8 + hh:base + 9 + hh], bool(upper)))
                sinks.append((o_ref, rows, sl))
    preps = _gdn_prep(probs, ri, ci)
    per = 2 * HEADS
    states = [st[hh] for (_, _, _, _, _, st) in dirs for hh in range(HEADS)]
    for t in range(cps):
        outs, states = _gdn_advance(preps[t * per:(t + 1) * per], states, dk)
        for (o_ref, rows, sl), o in zip(sinks[t * per:(t + 1) * per], outs):
            o_ref[rows, sl] = o.astype(BF16)
    for i, (_, _, _, _, _, st) in enumerate(dirs):
        for hh in range(HEADS):
            st[hh] = states[i * HEADS + hh]


def _gdn_mixer(xall, mod_i, norm_gain, w_in, conv_w, a_log, dt_bias, dims):
    bsz, ctx_len, seq, d, tm = dims
    ltot = ctx_len + seq
    nt, nct = ltot // tm, ctx_len // tm
    n_tiles = bsz * nt
    rows = n_tiles * tm
    dk = d // HEADS
    n_in = w_in.shape[1]
    n_pad = -(-n_in // LANES) * LANES
    wp = jnp.pad(w_in, ((0, 0), (0, n_pad - n_in))).astype(BF16)
    zeros8 = jnp.zeros((2, HEADS), F32)
    lay = lambda t: jnp.pad(jnp.concatenate([t, zeros8], axis=1).reshape(1, 4 * HEADS),
                            ((0, 0), (0, LANES - 4 * HEADS)))
    mrow = _mod_row_map(nt, nct, bsz)
    hb = tm // SUBLANES
    last_hblk = rows // SUBLANES - 1
    const = lambda j: (0, 0)
    q, k, v, z, gb = pl.pallas_call(
        functools.partial(_gdn_proj_kernel, d=d, nt=nt, nct=nct, tm=tm, dk=dk),
        grid=(n_tiles,),
        in_specs=[pl.BlockSpec((SUBLANES, d), lambda j: (jnp.maximum(j * hb - 1, 0), 0)),
                  pl.BlockSpec((tm, d), lambda j: (j, 0)),
                  pl.BlockSpec((SUBLANES, d), lambda j: (jnp.minimum((j + 1) * hb, last_hblk), 0)),
                  pl.BlockSpec((None, 6, d), lambda j: (mrow(j), 0, 0)),
                  pl.BlockSpec((1, d), const),
                  pl.BlockSpec((d, n_pad), const),
                  pl.BlockSpec((GDN_CONV, 3 * d), const),
                  pl.BlockSpec((1, LANES), const),
                  pl.BlockSpec((1, LANES), const)],
        out_specs=[pl.BlockSpec((tm, d), lambda j: (j, 0))] * 4 + [pl.BlockSpec((tm, LANES), lambda j: (j, 0))],
        out_shape=[jax.ShapeDtypeStruct((rows, d), BF16)] * 4 + [jax.ShapeDtypeStruct((rows, LANES), F32)],
        scratch_shapes=[pltpu.VMEM((tm + 2 * SUBLANES, n_pad), F32)],
        compiler_params=_cparams("parallel"),
        name="gdn_proj",
    )(xall, xall, xall, mod_i, norm_gain.reshape(1, d), wp, conv_w, lay(a_log), lay(dt_bias))
    c = GDN_CHUNK * GDN_CHUNKS_PER_STEP
    assert ctx_len % c == 0 and ltot % c == 0
    ncl, ncc = ltot // c, ctx_len // c

    def fwd(b, s):
        return (b * ncl + s, 0)

    def bwd(b, s):
        return (b * ncl + jnp.where(s < ncc, ncc - 1 - s, ncl + ncc - 1 - s), 0)

    blk = lambda m: pl.BlockSpec((c, d), m)
    gblk = lambda m: pl.BlockSpec((c, LANES), m)
    o_f, o_b = pl.pallas_call(
        functools.partial(_gdn_scan_kernel, dk=dk),
        grid=(bsz, ncl),
        in_specs=[blk(fwd), blk(fwd), blk(fwd), gblk(fwd), blk(bwd), blk(bwd), blk(bwd), gblk(bwd)],
        out_specs=[blk(fwd), blk(bwd)],
        out_shape=[jax.ShapeDtypeStruct((rows, d), BF16)] * 2,
        scratch_shapes=[pltpu.VMEM((HEADS, dk, dk), F32), pltpu.VMEM((HEADS, dk, dk), F32)],
        compiler_params=_cparams("parallel", "arbitrary"),
        name="gdn_scan",
    )(q, k, v, gb, q, k, v, gb)
    return (o_f, o_b, z)


def _split_bf16(x):
    hi = x.astype(BF16)
    lo = (x - hi.astype(F32)).astype(BF16)
    return hi, lo


def _post_kernel(*refs, kind, d, tm, dk):
    if kind == 0:
        of_ref, ob_ref, z_ref, og_ref = refs[:4]
        refs = refs[4:]
    else:
        o_ref = refs[0]
        refs = refs[1:]
    (x_ref, mod_ref, wo_ref, g_ref, wrh_ref, wrl_ref, br_ref,
     xo_ref, h_ref, ids_ref, cnt_ref, base) = refs
    j = pl.program_id(0)

    @pl.when(j == 0)
    def _():
        base[...] = jnp.zeros_like(base)

    if kind == 0:
        parts = []
        for hh in range(d // dk):
            sl = slice(hh * dk, (hh + 1) * dk)
            o = of_ref[:, sl].astype(F32) + ob_ref[:, sl].astype(F32)
            o = o * lax.rsqrt(jnp.mean(o * o, axis=-1, keepdims=True) + EPS) * og_ref[...]
            parts.append((o * _silu(z_ref[:, sl].astype(F32))).astype(BF16))
        o_in = jnp.concatenate(parts, axis=1)
    else:
        o_in = o_ref[...]
    mod = mod_ref[...]
    x = x_ref[...] + mod[2:3, :] * jnp.dot(o_in, wo_ref[...], preferred_element_type=F32)
    xo_ref[...] = x
    h = _norm_mod(x, g_ref[...], mod[3:4, :], mod[4:5, :])
    h_ref[:, :d] = h
    hi, lo = _split_bf16(h)
    logits = (jnp.dot(hi, wrh_ref[...], preferred_element_type=F32)
              + jnp.dot(lo, wrh_ref[...], preferred_element_type=F32)
              + jnp.dot(hi, wrl_ref[...], preferred_element_type=F32)) + br_ref[...]
    lane = lax.broadcasted_iota(I32, logits.shape, 1)
    big = jnp.int32(1 << 20)
    is_g = lane < MOE_GROUPS
    gl = jnp.where(is_g, logits, NEG)
    gmax = jnp.max(gl, axis=-1, keepdims=True)
    gsel = jnp.min(jnp.where(gl == gmax, lane, big), axis=-1, keepdims=True)
    p_group = 1.0 / jnp.sum(jnp.where(is_g, jnp.exp(gl - gmax), 0.0), axis=-1, keepdims=True)
    in_grp = jnp.logical_and(lane >= MOE_GROUPS + gsel * MOE_PER_GROUP,
                             lane < MOE_GROUPS + (gsel + 1) * MOE_PER_GROUP)
    el = jnp.where(in_grp, logits, NEG)
    v0 = jnp.max(el, axis=-1, keepdims=True)
    i0 = jnp.min(jnp.where(el == v0, lane, big), axis=-1, keepdims=True)
    el1 = jnp.where(lane == i0, NEG, el)
    v1 = jnp.max(el1, axis=-1, keepdims=True)
    i1 = jnp.min(jnp.where(el1 == v1, lane, big), axis=-1, keepdims=True)
    e1 = jnp.exp(v1 - v0)
    w0 = p_group / (1.0 + e1)
    w1 = p_group * e1 / (1.0 + e1)
    a0 = i0 - MOE_GROUPS - gsel * MOE_PER_GROUP
    a1 = i1 - MOE_GROUPS - gsel * MOE_PER_GROUP
    lo_e = jnp.minimum(a0, a1)
    hi_e = jnp.maximum(a0, a1)
    lof = lo_e.astype(F32)
    pair = (lof * MOE_PER_GROUP - lof * (lof + 1.0) * 0.5).astype(I32) + (hi_e - lo_e - 1)
    cls = gsel * MOE_PAIRS + pair
    first_lo = a0 < a1
    g_lo = jnp.where(first_lo, w0, w1)
    g_hi = jnp.where(first_lo, w1, w0)
    oh = lane == cls
    onehot = jnp.where(oh, 1.0, 0.0)
    ri = lax.broadcasted_iota(I32, (tm, tm), 0)
    ci = lax.broadcasted_iota(I32, (tm, tm), 1)
    tri = jnp.where(ri > ci, 1.0, 0.0).astype(BF16)
    before = base[...] + jnp.dot(tri, onehot.astype(BF16), preferred_element_type=F32)
    rank = jnp.sum(jnp.where(oh, before, 0.0), axis=-1, keepdims=True)
    new_base = base[...] + jnp.sum(onehot, axis=0, keepdims=True)
    base[...] = new_base
    cnt_ref[...] = new_base
    ids = jnp.where(lane == 0, cls, jnp.where(lane == 1, rank.astype(I32), 0))
    ids_ref[...] = ids.T[0:SUBLANES, :]
    h_ref[:, d:] = jnp.where(lane == 0, g_lo, jnp.where(lane == 1, g_hi, 0.0))


def _post_mixer(kind, mixer_out, xall, mod_i, w_out, ffn_gain, w_group, b_group, w_expert, b_expert,
                dims, o_gain=None):
    bsz, ctx_len, seq, d, tm = dims
    ltot = ctx_len + seq
    nt, nct = ltot // tm, ctx_len // tm
    n_tiles = bsz * nt
    rows = n_tiles * tm
    dk = d // HEADS
    mrow = _mod_row_map(nt, nct, bsz)
    wr = jnp.pad(jnp.concatenate([w_group, w_expert], axis=1),
                 ((0, 0), (0, LANES - MOE_GROUPS - MOE_EXPERTS)))
    wr_hi = wr.astype(BF16)
    wr_lo = (wr - wr_hi.astype(F32)).astype(BF16)
    br = jnp.pad(jnp.concatenate([b_group, b_expert]), (0, LANES - MOE_GROUPS - MOE_EXPERTS)).reshape(1, LANES)
    const = lambda j: (0, 0)
    row = lambda j: (j, 0)
    lead_specs = [pl.BlockSpec((tm, d), row)] * len(mixer_out)
    lead_args = list(mixer_out)
    if kind == 0:
        lead_specs.append(pl.BlockSpec((1, dk), const))
        lead_args.append(o_gain.reshape(1, dk))
    n_lead = len(lead_args)
    outs = pl.pallas_call(
        functools.partial(_post_kernel, kind=kind, d=d, tm=tm, dk=dk),
        grid=(n_tiles,),
        in_specs=lead_specs + [pl.BlockSpec((tm, d), row),
                               pl.BlockSpec((None, 6, d), lambda j: (mrow(j), 0, 0)),
                               pl.BlockSpec((w_out.shape[0], d), const),
                               pl.BlockSpec((1, d), const),
                               pl.BlockSpec((d, LANES), const),
                               pl.BlockSpec((d, LANES), const),
                               pl.BlockSpec((1, LANES), const)],
        out_specs=[pl.BlockSpec((tm, d), row), pl.BlockSpec((tm, d + LANES), row),
                   pl.BlockSpec((SUBLANES, tm), lambda j: (0, j)),
                   pl.BlockSpec((1, LANES), const)],
        out_shape=[jax.ShapeDtypeStruct((rows, d), F32), jax.ShapeDtypeStruct((rows, d + LANES), F32),
                   jax.ShapeDtypeStruct((SUBLANES, rows), I32),
                   jax.ShapeDtypeStruct((1, LANES), F32)],
        scratch_shapes=[pltpu.VMEM((1, LANES), F32)],
        input_output_aliases={n_lead: 0},
        compiler_params=_cparams("arbitrary"),
        name="post_mixer",
    )(*lead_args, xall, mod_i, w_out.astype(BF16), ffn_gain.reshape(1, d), wr_hi, wr_lo, br)
    return outs


def _dispatch_kernel(zlo_ref, zhi_ref, nu_ref, dest_ref, h_ref, xs_ref, zblk, hbuf, sem, zsem, *, tm, blk,
                     n_blocks, n_tiles):
    j = pl.program_id(0)
    slot = j % 2

    def tile_wait(s):
        pltpu.make_async_copy(hbuf.at[s], xs_ref.at[pl.ds(0, tm)], sem.at[s]).wait()

    def row_copy(src, dst_row, s):
        return pltpu.make_async_copy(src, xs_ref.at[pl.ds(dst_row, 1)], s)

    def blk_copy(bi):
        return pltpu.make_async_copy(zblk, xs_ref.at[pl.ds(pl.multiple_of(bi * blk, blk), blk)], zsem)

    @pl.when(j == 0)
    def _():
        zblk[...] = jnp.zeros_like(zblk)

        def per_class(e, carry):
            lo, hi = zlo_ref[e], zhi_ref[e]

            def start(r, c):
                row_copy(zblk.at[pl.ds(0, 1)], r, zsem).start()
                return c

            def wait(r, c):
                row_copy(zblk.at[pl.ds(0, 1)], r, zsem).wait()
                return c

            lax.fori_loop(lo, hi, start, 0)
            lax.fori_loop(lo, hi, wait, 0)
            return carry

        lax.fori_loop(0, MOE_CLASSES, per_class, 0)

        def tail_start(bi, c):
            blk_copy(bi).start()
            return c

        def tail_wait(bi, c):
            blk_copy(bi).wait()
            return c

        lax.fori_loop(nu_ref[0], n_blocks, tail_start, 0)
        lax.fori_loop(nu_ref[0], n_blocks, tail_wait, 0)

    @pl.when(j > 0)
    def _():
        tile_wait(1 - slot)

    hbuf[slot] = h_ref[...]

    def body(r, c):
        row_copy(hbuf.at[slot, pl.ds(r, 1)], dest_ref[0, r], sem.at[slot]).start()
        return c

    lax.fori_loop(0, tm, body, 0, unroll=8)

    @pl.when(j == n_tiles - 1)
    def _():
        tile_wait(slot)


def _expert_kernel(ea_ref, eb_ref, nu_ref, x_ref, w1a_ref, w3a_ref, w2a_ref, w1b_ref, w3b_ref, w2b_ref,
                   y_ref, *, d):
    j = pl.program_id(0)

    @pl.when(j < nu_ref[0])
    def _():
        x = x_ref[:, :d].astype(BF16)
        gates = x_ref[:, d:]

        def expert(w1_ref, w3_ref, w2_ref):
            a = jnp.dot(x, w1_ref[...], preferred_element_type=F32)
            b = jnp.dot(x, w3_ref[...], preferred_element_type=F32)
            return jnp.dot((_silu(a) * b).astype(BF16), w2_ref[...], preferred_element_type=F32)

        y_ref[...] = (expert(w1a_ref, w3a_ref, w2a_ref) * gates[:, 0:1]
                      + expert(w1b_ref, w3b_ref, w2b_ref) * gates[:, 1:2])

    @pl.when(j >= nu_ref[0])
    def _():
        y_ref[...] = jnp.zeros_like(y_ref)


def _combine_kernel(dest_ref, dnext_ref, x_ref, mod_ref, fg_ref, yb_ref, xo_ref, ybuf, sem, *, tm, final,
                    inner, n_steps):
    j = pl.program_id(0) if inner is None else pl.program_id(0) * inner + pl.program_id(1)
    slot = j % 2

    def gather(d_ref, s):
        def body(r, c):
            pltpu.make_async_copy(yb_ref.at[pl.ds(d_ref[0, r], 1)], ybuf.at[s, pl.ds(r, 1)], sem.at[s]).start()
            return c

        lax.fori_loop(0, tm, body, 0, unroll=8)

    @pl.when(j == 0)
    def _():
        gather(dest_ref, slot)

    @pl.when(j + 1 < n_steps)
    def _():
        gather(dnext_ref, 1 - slot)

    pltpu.make_async_copy(yb_ref.at[pl.ds(0, tm)], ybuf.at[slot], sem.at[slot]).wait()
    x = x_ref[...] + mod_ref[5:6, :] * ybuf[slot]
    if final:
        x = x * lax.rsqrt(jnp.mean(x * x, axis=-1, keepdims=True) + EPS) * fg_ref[...]
    xo_ref[...] = x


def _class_experts():
    lo, hi = [], []
    for g in range(MOE_GROUPS):
        for a in range(MOE_PER_GROUP):
            for b in range(a + 1, MOE_PER_GROUP):
                lo.append(g * MOE_PER_GROUP + a)
                hi.append(g * MOE_PER_GROUP + b)
    return jnp.asarray(lo, I32), jnp.asarray(hi, I32)


def _moe(xall, h, ids, counts, mod_i, w1, w3, w2, final_gain, dims, final):
    bsz, ctx_len, seq, d, tm = dims
    ltot = ctx_len + seq
    nt, nct = ltot // tm, ctx_len // tm
    n_tiles = bsz * nt
    rows = n_tiles * tm
    ncls = MOE_CLASSES
    blk = MOE_BLOCK
    dw = d + LANES
    n_blocks = -(-(rows + ncls * (blk - 1)) // blk)
    cnt = counts[0, :ncls].astype(I32)
    padded = (cnt + blk - 1) // blk * blk
    pad_end = jnp.cumsum(padded)
    pad_start = pad_end - padded
    cls_of_row = ids[0]
    start_of_row = jnp.sum(jnp.where(cls_of_row[:, None] == jnp.arange(ncls, dtype=I32)[None, :],
                                     pad_start[None, :], 0), axis=1)
    dest = (start_of_row + ids[1]).astype(I32).reshape(n_tiles, 1, tm)
    n_used = (pad_end[-1] // blk).astype(I32).reshape(1)
    blk_first = jnp.arange(n_blocks, dtype=I32) * blk
    blk_class = jnp.minimum(jnp.sum((pad_end[None, :] <= blk_first[:, None]).astype(I32), axis=1), ncls - 1)
    cls_lo, cls_hi = _class_experts()
    blk_lo, blk_hi = cls_lo[blk_class], cls_hi[blk_class]
    smem_dest = pl.BlockSpec((None, 1, tm), lambda j, *_: (j, 0, 0), memory_space=pltpu.SMEM)
    xs = pl.pallas_call(
        functools.partial(_dispatch_kernel, tm=tm, blk=blk, n_blocks=n_blocks, n_tiles=n_tiles),
        grid_spec=pltpu.PrefetchScalarGridSpec(
            num_scalar_prefetch=3, grid=(n_tiles,),
            in_specs=[smem_dest, pl.BlockSpec((tm, dw), lambda j, *_: (j, 0))],
            out_specs=pl.BlockSpec(memory_space=pl.ANY),
            scratch_shapes=[pltpu.VMEM((blk, dw), F32), pltpu.VMEM((2, tm, dw), F32),
                            pltpu.SemaphoreType.DMA((2,)), pltpu.SemaphoreType.DMA]),
        out_shape=jax.ShapeDtypeStruct((n_blocks * blk, dw), F32),
        compiler_params=_cparams("arbitrary"),
        name="moe_dispatch",
    )((pad_start + cnt).astype(I32), pad_end.astype(I32), n_used, dest, h)

    def xmap(j, ea, eb, nu):
        return (jnp.minimum(j, nu[0] - 1), 0)

    def wmap_lo(j, ea, eb, nu):
        return (ea[jnp.minimum(j, nu[0] - 1)], 0, 0)

    def wmap_hi(j, ea, eb, nu):
        return (eb[jnp.minimum(j, nu[0] - 1)], 0, 0)

    f = w1.shape[-1]
    w1b, w3b, w2b = w1.astype(BF16), w3.astype(BF16), w2.astype(BF16)
    yb = pl.pallas_call(
        functools.partial(_expert_kernel, d=d),
        grid_spec=pltpu.PrefetchScalarGridSpec(
            num_scalar_prefetch=3, grid=(n_blocks,),
            in_specs=[pl.BlockSpec((blk, dw), xmap),
                      pl.BlockSpec((None, d, f), wmap_lo),
                      pl.BlockSpec((None, d, f), wmap_lo),
                      pl.BlockSpec((None, f, d), wmap_lo),
                      pl.BlockSpec((None, d, f), wmap_hi),
                      pl.BlockSpec((None, d, f), wmap_hi),
                      pl.BlockSpec((None, f, d), wmap_hi)],
            out_specs=pl.BlockSpec((blk, d), lambda j, ea, eb, nu: (j, 0))),
        out_shape=jax.ShapeDtypeStruct((n_blocks * blk, d), F32),
        compiler_params=_cparams("arbitrary"),
        name="moe_experts",
    )(blk_lo, blk_hi, n_used, xs, w1b, w3b, w2b, w1b, w3b, w2b)

    if final:
        nlt = seq // tm
        grid, inner, n_steps = (bsz, nlt), nlt, bsz * nlt
        lin = lambda b, i: b * nlt + i
        tile_of = lambda l: (l // nlt) * nt + nct + l % nlt
        mod_map = lambda b, i: (b, 0, 0)
        out_map = lambda b, i: (b * nlt + i, 0)
        out_rows, aliases = bsz * seq, {}
    else:
        mrow = _mod_row_map(nt, nct, bsz)
        grid, inner, n_steps = (n_tiles,), None, n_tiles
        lin = lambda j: j
        tile_of = lambda l: l
        mod_map = lambda j: (mrow(j), 0, 0)
        out_map = lambda j: (j, 0)
        out_rows, aliases = rows, {2: 0}
    tile = lambda *g: tile_of(lin(*g))
    tile_next = lambda *g: tile_of(jnp.minimum(lin(*g) + 1, n_steps - 1))
    out = pl.pallas_call(
        functools.partial(_combine_kernel, tm=tm, final=final, inner=inner, n_steps=n_steps),
        grid=grid,
        in_specs=[pl.BlockSpec((None, 1, tm), lambda *g: (tile(*g), 0, 0), memory_space=pltpu.SMEM),
                  pl.BlockSpec((None, 1, tm), lambda *g: (tile_next(*g), 0, 0), memory_space=pltpu.SMEM),
                  pl.BlockSpec((tm, d), lambda *g: (tile(*g), 0)),
                  pl.BlockSpec((None, 6, d), mod_map),
                  pl.BlockSpec((1, d), lambda *g: (0, 0)),
                  pl.BlockSpec(memory_space=pl.ANY)],
        out_specs=pl.BlockSpec((tm, d), out_map),
        out_shape=jax.ShapeDtypeStruct((out_rows, d), F32),
        scratch_shapes=[pltpu.VMEM((2, tm, d), F32), pltpu.SemaphoreType.DMA((2,))],
        input_output_aliases=aliases,
        compiler_params=_cparams(*(("arbitrary",) * len(grid))),
        name="moe_combine",
    )(dest, dest, xall, mod_i, final_gain.reshape(1, d), yb)
    return out


def kernel(x, c, ctx, c_ctx, w_mod, b_mod, norm_mix, norm_ffn, gdn_w_in, gdn_conv, gdn_a_log, gdn_dt_bias, gdn_norm, gdn_w_out, diff_w_qkv, diff_lambda, diff_norm, diff_w_out, mla_w_down, mla_q_norm, mla_kv_norm, mla_w_uq, mla_w_ukv, mla_w_out, moe_w_group, moe_b_group, moe_w_expert, moe_b_expert, moe_w1, moe_w3, moe_w2, final_norm):
    bsz, seq, d = x.shape
    ctx_len = ctx.shape[1]
    depth = w_mod.shape[0]
    tm = _row_tile(ctx_len)
    assert d % LANES == 0 and d // HEADS == LANES
    assert ctx_len % tm == 0 and seq % tm == 0 and ctx_len % GDN_CHUNK == 0 and seq % GDN_CHUNK == 0
    dims = (bsz, ctx_len, seq, d, tm)
    ltot = ctx_len + seq
    xall = jnp.concatenate([ctx, x], axis=1).reshape(bsz * ltot, d)
    mod = _mod_vectors(c, c_ctx, w_mod, b_mod)
    tables = _rope_tables(seq, ctx_len)
    for i in range(depth):
        kind, j = i % N_MIXERS, i // N_MIXERS
        if kind == 0:
            mixer_out = _gdn_mixer(xall, mod[i], norm_mix[i], gdn_w_in[j], gdn_conv[j], gdn_a_log[j],
                                   gdn_dt_bias[j], dims)
            w_out, o_gain = gdn_w_out[j], gdn_norm[j]
        elif kind == 1:
            lam_init = 0.8 - 0.6 * math.exp(-0.3 * i)
            mixer_out = _diff_mixer(xall, mod[i], norm_mix[i], diff_w_qkv[j], diff_lambda[j], diff_norm[j],
                                    dims, tables, lam_init)
            w_out, o_gain = diff_w_out[j], None
        else:
            mixer_out = _mla_mixer(xall, mod[i], norm_mix[i], mla_w_down[j], mla_q_norm[j], mla_kv_norm[j],
                                   mla_w_uq[j], mla_w_ukv[j], dims, tables)
            w_out, o_gain = mla_w_out[j], None
        xall, h, ids, counts = _post_mixer(kind, mixer_out, xall, mod[i], w_out, norm_ffn[i],
                                           moe_w_group[i], moe_b_group[i], moe_w_expert[i],
                                           moe_b_expert[i], dims, o_gain)
        xall = _moe(xall, h, ids, counts, mod[i], moe_w1[i], moe_w3[i], moe_w2[i], final_norm, dims,
                    final=(i == depth - 1))
    return xall.reshape(bsz, seq, d)
```

```python
import functools
import math

import jax
import jax.numpy as jnp
from jax import lax
from jax.experimental import pallas as pl
from jax.experimental.pallas import tpu as pltpu

F32 = jnp.float32
BF16 = jnp.bfloat16
I32 = jnp.int32
HIGHEST = lax.Precision.HIGHEST

LANES = 128
SUBLANES = 8
VMEM_LIMIT = 56 * 1024 * 1024

EPS = 1e-6
GRID_W = 64
ROPE_THETA = 10000.0
N_MIXERS = 3
HEADS = 8
GDN_CONV = 5
GDN_CHUNK = 64
DIFF_SUBLN_EPS = 1e-5
MLA_NOPE = 128
MLA_ROPE = 64
MLA_V = 128
MOE_GROUPS = 4
MOE_PER_GROUP = 8
MOE_EXPERTS = MOE_GROUPS * MOE_PER_GROUP
MOE_TOP_K = 2
MOE_PAIRS = MOE_PER_GROUP * (MOE_PER_GROUP - 1) // 2
MOE_CLASSES = MOE_GROUPS * MOE_PAIRS
MOE_BLOCK = 256
NEG = -1e30


def _cparams(*sem):
    return pltpu.CompilerParams(dimension_semantics=sem, vmem_limit_bytes=VMEM_LIMIT)


def _row_tile(ctx_len):
    return 256 if ctx_len % 256 == 0 else 128


def _mod_row_map(nt, nct, bsz):
    def f(j):
        return jnp.where(j % nt < nct, bsz, j // nt)
    return f


def _norm_mod(x, gain, shift, scale, eps=EPS):
    var = jnp.mean(x * x, axis=-1, keepdims=True)
    y = x * lax.rsqrt(var + eps) * gain
    return y * (1.0 + scale) + shift


def _silu(x):
    return x * jax.nn.sigmoid(x)


def _mod_kernel(c_ref, w_ref, b_ref, o_ref):
    s = _silu(c_ref[...])
    o_ref[...] = jnp.dot(s, w_ref[...], precision=HIGHEST, preferred_element_type=F32) + b_ref[...]


def _mod_vectors(c, c_ctx, w_mod, b_mod):
    depth, d, n = w_mod.shape
    bsz = c.shape[0]
    rows = -(-(bsz + 1) // SUBLANES) * SUBLANES
    cc = jnp.zeros((rows, d), F32).at[:bsz].set(c).at[bsz].set(c_ctx)
    tn = 512
    out = pl.pallas_call(
        _mod_kernel,
        grid=(depth, n // tn),
        in_specs=[pl.BlockSpec((rows, d), lambda i, j: (0, 0)),
                  pl.BlockSpec((None, d, tn), lambda i, j: (i, 0, j)),
                  pl.BlockSpec((None, 1, tn), lambda i, j: (i, 0, j))],
        out_specs=pl.BlockSpec((None, rows, tn), lambda i, j: (i, 0, j)),
        out_shape=jax.ShapeDtypeStruct((depth, rows, n), F32),
        compiler_params=_cparams("parallel", "parallel"),
        name="mod_vectors",
    )(cc, w_mod, b_mod.reshape(depth, 1, n))
    return out.reshape(depth, rows, 6, d)


def _rope_tables(seq, ctx_len):
    quarter = 16
    inv_freq = ROPE_THETA ** (-jnp.arange(quarter, dtype=F32) / quarter)
    t = jnp.arange(seq)
    row = (t // GRID_W).astype(F32)[:, None] * inv_freq
    col = (t % GRID_W).astype(F32)[:, None] * inv_freq
    cos = jnp.concatenate([jnp.cos(row), jnp.cos(row), jnp.cos(col), jnp.cos(col)], axis=1)
    sin = jnp.concatenate([-jnp.sin(row), jnp.sin(row), -jnp.sin(col), jnp.sin(col)], axis=1)
    cos = jnp.concatenate([jnp.ones((ctx_len, 64), F32), cos], axis=0)
    sin = jnp.concatenate([jnp.zeros((ctx_len, 64), F32), sin], axis=0)
    return jnp.tile(cos, (1, 2)), jnp.tile(sin, (1, 2))


def _rope128(blk, cos, sin):
    lane = lax.broadcasted_iota(I32, blk.shape, 1)
    first = (lane % 32) < 16
    partner = jnp.where(first, pltpu.roll(blk, LANES - 16, 1), pltpu.roll(blk, 16, 1))
    return blk * cos + partner * sin


def _diff_proj_kernel(x_ref, mod_ref, g_ref, w_ref, cos_ref, sin_ref, o_ref, vt_ref, *, d, q_scale):
    h = _norm_mod(x_ref[...], g_ref[...], mod_ref[0:1, :], mod_ref[1:2, :])
    p = jnp.dot(h.astype(BF16), w_ref[...], preferred_element_type=F32)
    cos = cos_ref[...]
    sin = sin_ref[...]
    nqk = 2 * d // LANES
    for cb in range(nqk):
        r = _rope128(p[:, cb * LANES:(cb + 1) * LANES], cos, sin)
        if cb < nqk // 2:
            r = r * q_scale
        o_ref[:, cb * LANES:(cb + 1) * LANES] = r.astype(BF16)
    vt_ref[...] = p[:, 2 * d:].T.astype(BF16)


def _pick_tk(n):
    for cand in (768, 512, 384, 256, 128):
        if n % cand == 0:
            return cand
    raise ValueError(n)


def _flash_t(streams, k_ref, vt_ref, m_ref, l_ref, acc_ref, nsteps, tk):
    for s in range(len(streams)):
        m_ref[s] = jnp.full(m_ref.shape[1:], NEG, F32)
        l_ref[s] = jnp.zeros(l_ref.shape[1:], F32)
        acc_ref[s] = jnp.zeros(acc_ref.shape[1:], F32)
    nt_dims = (((1,), (1,)), ((), ()))

    def scores(i):
        return [lax.dot_general(k_ref[i * tk:(i + 1) * tk, kc], q, nt_dims, preferred_element_type=F32)
                for q, kc, _ in streams]

    sts = scores(0)
    for i in range(nsteps):
        nxt = scores(i + 1) if i + 1 < nsteps else None
        alphas, ps = [], []
        for s, st in enumerate(sts):
            m_old = m_ref[s]
            m_new = jnp.maximum(m_old, jnp.max(st, axis=0, keepdims=True))
            alpha = jnp.exp2(m_old - m_new)
            p = jnp.exp2(st - m_new)
            l_ref[s] = alpha * l_ref[s] + jnp.sum(p, axis=0, keepdims=True)
            m_ref[s] = m_new
            alphas.append(alpha)
            ps.append(p.astype(BF16))
        pvs = [jnp.dot(vt_ref[vr, i * tk:(i + 1) * tk], p, preferred_element_type=F32)
               for (_, _, vr), p in zip(streams, ps)]
        for s in range(len(streams)):
            acc_ref[s] = alphas[s] * acc_ref[s] + pvs[s]
        sts = nxt


def _flash_ctx_or_all(streams, k_ref, vt_ref, m_ref, l_ref, acc_ref, nct, ctx_len, ltot):
    i = pl.program_id(2)
    tk_c, tk_l = _pick_tk(ctx_len), _pick_tk(ltot)

    @pl.when(i < nct)
    def _():
        _flash_t(streams, k_ref, vt_ref, m_ref, l_ref, acc_ref, ctx_len // tk_c, tk_c)

    @pl.when(i >= nct)
    def _():
        _flash_t(streams, k_ref, vt_ref, m_ref, l_ref, acc_ref, ltot // tk_l, tk_l)


ATTN_HEADS_PER_STEP = 4
MLA_HEADS_PER_STEP = 4


def _diff_attn_kernel(lam_ref, gain_ref, q_ref, k_ref, vt_ref, o_ref, m_ref, l_ref, acc_ref, *, nct,
                      ctx_len, ltot, lam_init, hw):
    streams = []
    for g in range(ATTN_HEADS_PER_STEP):
        cols = slice(g * hw, (g + 1) * hw)
        q = q_ref[:, cols]
        lane = lax.broadcasted_iota(I32, q.shape, 1)
        zero = jnp.zeros_like(q)
        streams.append((jnp.where(lane < hw // 2, q, zero), cols, cols))
        streams.append((jnp.where(lane >= hw // 2, q, zero), cols, cols))
    _flash_ctx_or_all(streams, k_ref, vt_ref, m_ref, l_ref, acc_ref, nct, ctx_len, ltot)
    lv = lam_ref[...]
    lam = (jnp.exp(jnp.sum(lv[0:1] * lv[1:2], keepdims=True))
           - jnp.exp(jnp.sum(lv[2:3] * lv[3:4], keepdims=True)) + lam_init)
    for g in range(ATTN_HEADS_PER_STEP):
        o = acc_ref[2 * g] / l_ref[2 * g] - lam * (acc_ref[2 * g + 1] / l_ref[2 * g + 1])
        var = jnp.mean(o * o, axis=0, keepdims=True)
        o = o * lax.rsqrt(var + DIFF_SUBLN_EPS) * gain_ref[...] * (1.0 - lam_init)
        o_ref[:, g * hw:(g + 1) * hw] = o.T.astype(BF16)


def _diff_mixer(xall, mod_i, norm_gain, w_qkv, lam_vec, sub_gain, dims, tables, lam_init):
    bsz, ctx_len, seq, d, tm = dims
    ltot = ctx_len + seq
    nt, nct = ltot // tm, ctx_len // tm
    n_tiles = bsz * nt
    dh = d // HEADS // 2
    cos, sin = tables
    mrow = _mod_row_map(nt, nct, bsz)
    rows = n_tiles * tm
    qk, vt = pl.pallas_call(
        functools.partial(_diff_proj_kernel, d=d, q_scale=dh ** -0.5 * math.log2(math.e)),
        grid=(n_tiles,),
        in_specs=[pl.BlockSpec((tm, d), lambda j: (j, 0)),
                  pl.BlockSpec((None, 6, d), lambda j: (mrow(j), 0, 0)),
                  pl.BlockSpec((1, d), lambda j: (0, 0)),
                  pl.BlockSpec((d, 3 * d), lambda j: (0, 0)),
                  pl.BlockSpec((tm, LANES), lambda j: (j % nt, 0)),
                  pl.BlockSpec((tm, LANES), lambda j: (j % nt, 0))],
        out_specs=[pl.BlockSpec((tm, 2 * d), lambda j: (j, 0)),
                   pl.BlockSpec((d, tm), lambda j: (0, j))],
        out_shape=[jax.ShapeDtypeStruct((rows, 2 * d), BF16), jax.ShapeDtypeStruct((d, rows), BF16)],
        compiler_params=_cparams("parallel"),
        name="diff_proj",
    )(xall, mod_i, norm_gain.reshape(1, d), w_qkv.astype(BF16), cos, sin)
    hw = 2 * dh
    hps = ATTN_HEADS_PER_STEP
    hg = HEADS // hps
    o = pl.pallas_call(
        functools.partial(_diff_attn_kernel, nct=nct, ctx_len=ctx_len, ltot=ltot, lam_init=lam_init, hw=hw),
        grid=(bsz, hg, nt),
        in_specs=[pl.BlockSpec((4, dh), lambda b, h, i: (0, 0)),
                  pl.BlockSpec((hw, 1), lambda b, h, i: (0, 0)),
                  pl.BlockSpec((tm, hps * hw), lambda b, h, i: (b * nt + i, h)),
                  pl.BlockSpec((ltot, hps * hw), lambda b, h, i: (b, hg + h), pipeline_mode=pl.Buffered(1)),
                  pl.BlockSpec((hps * hw, ltot), lambda b, h, i: (h, b), pipeline_mode=pl.Buffered(1))],
        out_specs=pl.BlockSpec((tm, hps * hw), lambda b, h, i: (b * nt + i, h)),
        out_shape=jax.ShapeDtypeStruct((rows, d), BF16),
        scratch_shapes=[pltpu.VMEM((2 * hps, 1, tm), F32), pltpu.VMEM((2 * hps, 1, tm), F32),
                        pltpu.VMEM((2 * hps, hw, tm), F32)],
        compiler_params=_cparams("parallel", "parallel", "arbitrary"),
        name="diff_attn",
    )(lam_vec, sub_gain.reshape(hw, 1), qk, qk, vt)
    return (o,)


def _mla_proj_kernel(x_ref, mod_ref, g_ref, wd_ref, qg_ref, kvg_ref, wq_ref, wkv_ref, cos_ref, sin_ref,
                     q_ref, k_ref, v_ref, *, q_lora, kv_lora, scale):
    h = _norm_mod(x_ref[...], g_ref[...], mod_ref[0:1, :], mod_ref[1:2, :])
    p = jnp.dot(h.astype(BF16), wd_ref[...], preferred_element_type=F32)
    cq = p[:, :q_lora]
    cq = cq * lax.rsqrt(jnp.mean(cq * cq, axis=-1, keepdims=True) + EPS) * qg_ref[...]
    ckv = p[:, q_lora:q_lora + kv_lora]
    ckv = ckv * lax.rsqrt(jnp.mean(ckv * ckv, axis=-1, keepdims=True) + EPS) * kvg_ref[...]
    cos = cos_ref[...]
    sin = sin_ref[...]
    kr = _rope128(p[:, q_lora + kv_lora:], cos, sin).astype(BF16)
    q = jnp.dot(cq.astype(BF16), wq_ref[...], preferred_element_type=F32)
    kv = jnp.dot(ckv.astype(BF16), wkv_ref[...], preferred_element_type=F32)
    hq = MLA_NOPE + LANES
    for hh in range(HEADS):
        q_ref[:, hh * hq:hh * hq + MLA_NOPE] = (q[:, hh * hq:hh * hq + MLA_NOPE] * scale).astype(BF16)
        qr = _rope128(q[:, hh * hq + MLA_NOPE:(hh + 1) * hq], cos, sin) * scale
        q_ref[:, hh * hq + MLA_NOPE:(hh + 1) * hq] = qr.astype(BF16)
        k_ref[:, hh * hq:hh * hq + MLA_NOPE] = kv[:, hh * MLA_NOPE:(hh + 1) * MLA_NOPE].astype(BF16)
        k_ref[:, hh * hq + MLA_NOPE:(hh + 1) * hq] = kr
    v_ref[...] = kv[:, HEADS * MLA_NOPE:].T.astype(BF16)


def _mla_attn_kernel(q_ref, k_ref, vt_ref, o_ref, m_ref, l_ref, acc_ref, *, nct, ctx_len, ltot, hq):
    streams = [(q_ref[:, g * hq:(g + 1) * hq], slice(g * hq, (g + 1) * hq), slice(g * MLA_V, (g + 1) * MLA_V))
               for g in range(MLA_HEADS_PER_STEP)]
    _flash_ctx_or_all(streams, k_ref, vt_ref, m_ref, l_ref, acc_ref, nct, ctx_len, ltot)
    for g in range(MLA_HEADS_PER_STEP):
        o_ref[:, g * MLA_V:(g + 1) * MLA_V] = (acc_ref[g] / l_ref[g]).T.astype(BF16)


def _mla_mixer(xall, mod_i, norm_gain, w_down, q_gain, kv_gain, w_uq, w_ukv, dims, tables):
    bsz, ctx_len, seq, d, tm = dims
    ltot = ctx_len + seq
    nt, nct = ltot // tm, ctx_len // tm
    n_tiles = bsz * nt
    rows = n_tiles * tm
    q_lora, kv_lora = q_gain.shape[0], kv_gain.shape[0]
    cos, sin = tables
    mrow = _mod_row_map(nt, nct, bsz)
    hq = MLA_NOPE + LANES
    wd = jnp.pad(w_down, ((0, 0), (0, LANES - MLA_ROPE))).astype(BF16)
    nd = wd.shape[1]
    wq = jnp.pad(w_uq.reshape(q_lora, HEADS, MLA_NOPE + MLA_ROPE),
                 ((0, 0), (0, 0), (0, LANES - MLA_ROPE))).reshape(q_lora, HEADS * hq).astype(BF16)
    wkv = w_ukv.reshape(kv_lora, HEADS, MLA_NOPE + MLA_V)
    wkv = jnp.concatenate([wkv[:, :, :MLA_NOPE].reshape(kv_lora, HEADS * MLA_NOPE),
                           wkv[:, :, MLA_NOPE:].reshape(kv_lora, HEADS * MLA_V)], axis=1).astype(BF16)
    scale = (MLA_NOPE + MLA_ROPE) ** -0.5 * math.log2(math.e)
    const = lambda j: (0, 0)
    q, k, v = pl.pallas_call(
        functools.partial(_mla_proj_kernel, q_lora=q_lora, kv_lora=kv_lora, scale=scale),
        grid=(n_tiles,),
        in_specs=[pl.BlockSpec((tm, d), lambda j: (j, 0)),
                  pl.BlockSpec((None, 6, d), lambda j: (mrow(j), 0, 0)),
                  pl.BlockSpec((1, d), const),
                  pl.BlockSpec((d, nd), const),
                  pl.BlockSpec((1, q_lora), const),
                  pl.BlockSpec((1, kv_lora), const),
                  pl.BlockSpec((q_lora, HEADS * hq), const),
                  pl.BlockSpec((kv_lora, HEADS * (MLA_NOPE + MLA_V)), const),
                  pl.BlockSpec((tm, LANES), lambda j: (j % nt, 0)),
                  pl.BlockSpec((tm, LANES), lambda j: (j % nt, 0))],
        out_specs=[pl.BlockSpec((tm, HEADS * hq), lambda j: (j, 0)),
                   pl.BlockSpec((tm, HEADS * hq), lambda j: (j, 0)),
                   pl.BlockSpec((HEADS * MLA_V, tm), lambda j: (0, j))],
        out_shape=[jax.ShapeDtypeStruct((rows, HEADS * hq), BF16),
                   jax.ShapeDtypeStruct((rows, HEADS * hq), BF16),
                   jax.ShapeDtypeStruct((HEADS * MLA_V, rows), BF16)],
        compiler_params=_cparams("parallel"),
        name="mla_proj",
    )(xall, mod_i, norm_gain.reshape(1, d), wd, q_gain.reshape(1, q_lora), kv_gain.reshape(1, kv_lora),
      wq, wkv, cos, sin)
    hps = MLA_HEADS_PER_STEP
    o = pl.pallas_call(
        functools.partial(_mla_attn_kernel, nct=nct, ctx_len=ctx_len, ltot=ltot, hq=hq),
        grid=(bsz, HEADS // hps, nt),
        in_specs=[pl.BlockSpec((tm, hps * hq), lambda b, h, i: (b * nt + i, h)),
                  pl.BlockSpec((ltot, hps * hq), lambda b, h, i: (b, h), pipeline_mode=pl.Buffered(1)),
                  pl.BlockSpec((hps * MLA_V, ltot), lambda b, h, i: (h, b), pipeline_mode=pl.Buffered(1))],
        out_specs=pl.BlockSpec((tm, hps * MLA_V), lambda b, h, i: (b * nt + i, h)),
        out_shape=jax.ShapeDtypeStruct((rows, HEADS * MLA_V), BF16),
        scratch_shapes=[pltpu.VMEM((hps, 1, tm), F32), pltpu.VMEM((hps, 1, tm), F32),
                        pltpu.VMEM((hps, MLA_V, tm), F32)],
        compiler_params=_cparams("parallel", "parallel", "arbitrary"),
        name="mla_attn",
    )(q, k, v)
    return (o,)


def _gdn_proj_kernel(xp_ref, x_ref, xn_ref, mod_ref, g_ref, w_ref, cw_ref, alog_ref, dtb_ref,
                     q_ref, k_ref, v_ref, z_ref, gb_ref, pbuf, *, d, nt, nct, tm, dk):
    j = pl.program_id(0)
    r = j % nt
    first = jnp.logical_or(r == 0, r == nct)
    last = jnp.logical_or(r == nct - 1, r == nt - 1)
    halo = SUBLANES
    xe = jnp.concatenate([xp_ref[...], x_ref[...], xn_ref[...]], axis=0)
    h = _norm_mod(xe, g_ref[...], mod_ref[0:1, :], mod_ref[1:2, :])
    rid = lax.broadcasted_iota(I32, (tm + 2 * halo, 1), 0)
    keep = jnp.logical_and(jnp.logical_or(rid >= halo, jnp.logical_not(first)),
                           jnp.logical_or(rid < tm + halo, jnp.logical_not(last)))
    h = jnp.where(keep, h, 0.0)
    pbuf[...] = jnp.dot(h.astype(BF16), w_ref[...], preferred_element_type=F32)
    half = GDN_CONV // 2

    def conv_block(c0):
        col = pbuf[:, pl.ds(c0, LANES)]
        n = col.shape[0]
        acc = None
        for t in range(GDN_CONV):
            sh = col if t == half else pltpu.roll(col, (half - t) % n, 0)
            term = sh[halo:halo + tm, :] * cw_ref[t:t + 1, pl.ds(c0, LANES)]
            acc = term if acc is None else acc + term
        return _silu(acc)

    for hh in range(3 * d // LANES):
        c0 = hh * LANES
        blk = conv_block(c0)
        if hh < 2 * d // LANES:
            blk = blk * lax.rsqrt(jnp.sum(blk * blk, axis=-1, keepdims=True) + EPS)
        if hh < d // LANES:
            q_ref[:, c0:c0 + LANES] = (blk * dk ** -0.5).astype(BF16)
        elif hh < 2 * d // LANES:
            k_ref[:, c0 - d:c0 - d + LANES] = blk.astype(BF16)
        else:
            v_ref[:, c0 - 2 * d:c0 - 2 * d + LANES] = blk.astype(BF16)
    z_ref[...] = pbuf[halo:halo + tm, 3 * d:4 * d].astype(BF16)
    ab = pbuf[halo:halo + tm, 4 * d:4 * d + LANES]
    lane = lax.broadcasted_iota(I32, ab.shape, 1)
    is_a = (lane % 16) < 8
    g = -jnp.exp(alog_ref[...]) * jax.nn.softplus(ab + dtb_ref[...])
    gb_ref[...] = jnp.where(is_a, g, jax.nn.sigmoid(ab))


TRI_BASE = 16


def _mm(a, b):
    return jnp.dot(a.astype(BF16), b.astype(BF16), preferred_element_type=F32)


def _tri_inverse_many(lms, ri, ci):
    n = lms[0].shape[0]

    def same(s):
        shift = int(math.log2(s))
        return (ri >> shift) == (ci >> shift)

    eye = jnp.where(ri == ci, 1.0, 0.0)
    base = same(TRI_BASE)
    ms = [jnp.where(base, lm, 0.0) for lm in lms]
    ps = [eye - m for m in ms]
    for _ in range(int(math.log2(TRI_BASE)) - 1):
        ms = [_mm(m, m) for m in ms]
        ps = [p + _mm(p, m) for p, m in zip(ps, ms)]
    s = TRI_BASE
    while s < n:
        band = jnp.logical_and(same(2 * s), jnp.logical_not(same(s)))
        ts = [_mm(p, jnp.where(band, lm, 0.0)) for p, lm in zip(ps, lms)]
        ps = [p - _mm(t, p) for p, t in zip(ps, ts)]
        s *= 2
    return ps


def _gdn_prep(probs, ri, ci):
    c = probs[0][0].shape[0]
    nt_dims = (((1,), (1,)), ((), ()))
    incl = {False: ri >= ci, True: ri <= ci}
    strict = {False: ri > ci, True: ri < ci}
    decays, kbs, rhss, qgs, kdecs, glasts = [], [], [], [], [], []
    for q, k, v, gc, gct, beta, upper in probs:
        decays.append(jnp.exp(jnp.where(incl[upper], gc - gct, NEG)))
        kf = k.astype(F32)
        kb = kf * beta
        eg = jnp.exp(gc)
        g_last = gc[0:1, :] if upper else gc[c - 1:c, :]
        kbs.append(kb.astype(BF16))
        rhss.append(jnp.concatenate([v.astype(F32) * beta, kb * eg], axis=1).astype(BF16))
        qgs.append((q.astype(F32) * eg).astype(BF16))
        kdecs.append((kf * jnp.exp(g_last - gc)).astype(BF16))
        glasts.append(g_last)
    kks = [lax.dot_general(kb, p[1], nt_dims, preferred_element_type=F32) for kb, p in zip(kbs, probs)]
    qks = [lax.dot_general(p[0], p[1], nt_dims, preferred_element_type=F32) for p in probs]
    lowers = [jnp.where(strict[p[6]], kk * dec, 0.0) for kk, dec, p in zip(kks, decays, probs)]
    intras = [(qk * dec).astype(BF16) for qk, dec in zip(qks, decays)]
    tinvs = _tri_inverse_many(lowers, ri, ci)
    uws = [jnp.dot(t.astype(BF16), r, preferred_element_type=F32) for t, r in zip(tinvs, rhss)]
    return list(zip(uws, qgs, intras, kdecs, glasts))


def _gdn_advance(preps, states, dv):
    tn_dims = (((0,), (0,)), ((), ()))
    sbs = [st.astype(BF16) for st in states]
    wss = [jnp.dot(p[0][:, dv:].astype(BF16), sb, preferred_element_type=F32) for p, sb in zip(preps, sbs)]
    o1s = [jnp.dot(p[1], sb, preferred_element_type=F32) for p, sb in zip(preps, sbs)]
    v_news = [(p[0][:, :dv] - ws).astype(BF16) for p, ws in zip(preps, wss)]
    o2s = [jnp.dot(p[2], vn, preferred_element_type=F32) for p, vn in zip(preps, v_news)]
    upds = [lax.dot_general(p[3], vn, tn_dims, preferred_element_type=F32) for p, vn in zip(preps, v_news)]
    outs = [o1 + o2 for o1, o2 in zip(o1s, o2s)]
    new_states = [st * jnp.exp(p[4]) + upd for st, p, upd in zip(states, preps, upds)]
    return outs, new_states


GDN_CHUNKS_PER_STEP = 2


def _gdn_scan_kernel(qf_ref, kf_ref, vf_ref, gf_ref, qb_ref, kb_ref, vb_ref, gbk_ref,
                     of_ref, ob_ref, sf, sb, *, dk):
    s = pl.program_id(1)

    @pl.when(s == 0)
    def _():
        sf[...] = jnp.zeros_like(sf)
        sb[...] = jnp.zeros_like(sb)

    c = GDN_CHUNK
    cps = qf_ref.shape[0] // c
    ri = lax.broadcasted_iota(I32, (c, c), 0)
    ci = lax.broadcasted_iota(I32, (c, c), 1)
    tri_l = (ri >= ci).astype(F32)
    tri_u = (ri <= ci).astype(F32)
    dirs = ((qf_ref, kf_ref, vf_ref, gf_ref, of_ref, sf), (qb_ref, kb_ref, vb_ref, gbk_ref, ob_ref, sb))
    probs, sinks = [], []
    for t in range(cps):
        for upper, (q_ref, k_ref, v_ref, g_ref, o_ref, st) in enumerate(dirs):
            j = cps - 1 - t if upper else t
            rows = slice(j * c, (j + 1) * c)
            gbv = g_ref[rows, :]
            csum = jnp.dot(tri_u if upper else tri_l, gbv, precision=HIGHEST, preferred_element_type=F32)
            csum_t = csum.T
            base = 16 * upper
            for hh in range(HEADS):
                sl = slice(hh * dk, (hh + 1) * dk)
                probs.append((q_ref[rows, sl], k_ref[rows, sl], v_ref[rows, sl],
                              csum[:, base + hh:base + hh + 1], csum_t[base + hh:base + hh + 1, :],
                              gbv[:, base + 8 + hh:base + 9 + hh], bool(upper)))
                sinks.append((o_ref, rows, sl))
    preps = _gdn_prep(probs, ri, ci)
    per = 2 * HEADS
    states = [st[hh] for (_, _, _, _, _, st) in dirs for hh in range(HEADS)]
    for t in range(cps):
        outs, states = _gdn_advance(preps[t * per:(t + 1) * per], states, dk)
        for (o_ref, rows, sl), o in zip(sinks[t * per:(t + 1) * per], outs):
            o_ref[rows, sl] = o.astype(BF16)
    for i, (_, _, _, _, _, st) in enumerate(dirs):
        for hh in range(HEADS):
            st[hh] = states[i * HEADS + hh]


def _gdn_mixer(xall, mod_i, norm_gain, w_in, conv_w, a_log, dt_bias, dims):
    bsz, ctx_len, seq, d, tm = dims
    ltot = ctx_len + seq
    nt, nct = ltot // tm, ctx_len // tm
    n_tiles = bsz * nt
    rows = n_tiles * tm
    dk = d // HEADS
    n_in = w_in.shape[1]
    n_pad = -(-n_in // LANES) * LANES
    wp = jnp.pad(w_in, ((0, 0), (0, n_pad - n_in))).astype(BF16)
    zeros8 = jnp.zeros((2, HEADS), F32)
    lay = lambda t: jnp.pad(jnp.concatenate([t, zeros8], axis=1).reshape(1, 4 * HEADS),
                            ((0, 0), (0, LANES - 4 * HEADS)))
    mrow = _mod_row_map(nt, nct, bsz)
    hb = tm // SUBLANES
    last_hblk = rows // SUBLANES - 1
    const = lambda j: (0, 0)
    q, k, v, z, gb = pl.pallas_call(
        functools.partial(_gdn_proj_kernel, d=d, nt=nt, nct=nct, tm=tm, dk=dk),
        grid=(n_tiles,),
        in_specs=[pl.BlockSpec((SUBLANES, d), lambda j: (jnp.maximum(j * hb - 1, 0), 0)),
                  pl.BlockSpec((tm, d), lambda j: (j, 0)),
                  pl.BlockSpec((SUBLANES, d), lambda j: (jnp.minimum((j + 1) * hb, last_hblk), 0)),
                  pl.BlockSpec((None, 6, d), lambda j: (mrow(j), 0, 0)),
                  pl.BlockSpec((1, d), const),
                  pl.BlockSpec((d, n_pad), const),
                  pl.BlockSpec((GDN_CONV, 3 * d), const),
                  pl.BlockSpec((1, LANES), const),
                  pl.BlockSpec((1, LANES), const)],
        out_specs=[pl.BlockSpec((tm, d), lambda j: (j, 0))] * 4 + [pl.BlockSpec((tm, LANES), lambda j: (j, 0))],
        out_shape=[jax.ShapeDtypeStruct((rows, d), BF16)] * 4 + [jax.ShapeDtypeStruct((rows, LANES), F32)],
        scratch_shapes=[pltpu.VMEM((tm + 2 * SUBLANES, n_pad), F32)],
        compiler_params=_cparams("parallel"),
        name="gdn_proj",
    )(xall, xall, xall, mod_i, norm_gain.reshape(1, d), wp, conv_w, lay(a_log), lay(dt_bias))
    c = GDN_CHUNK * GDN_CHUNKS_PER_STEP
    assert ctx_len % c == 0 and ltot % c == 0
    ncl, ncc = ltot // c, ctx_len // c

    def fwd(b, s):
        return (b * ncl + s, 0)

    def bwd(b, s):
        return (b * ncl + jnp.where(s < ncc, ncc - 1 - s, ncl + ncc - 1 - s), 0)

    blk = lambda m: pl.BlockSpec((c, d), m)
    gblk = lambda m: pl.BlockSpec((c, LANES), m)
    o_f, o_b = pl.pallas_call(
        functools.partial(_gdn_scan_kernel, dk=dk),
        grid=(bsz, ncl),
        in_specs=[blk(fwd), blk(fwd), blk(fwd), gblk(fwd), blk(bwd), blk(bwd), blk(bwd), gblk(bwd)],
        out_specs=[blk(fwd), blk(bwd)],
        out_shape=[jax.ShapeDtypeStruct((rows, d), BF16)] * 2,
        scratch_shapes=[pltpu.VMEM((HEADS, dk, dk), F32), pltpu.VMEM((HEADS, dk, dk), F32)],
        compiler_params=_cparams("parallel", "arbitrary"),
        name="gdn_scan",
    )(q, k, v, gb, q, k, v, gb)
    return (o_f, o_b, z)


def _split_bf16(x):
    hi = x.astype(BF16)
    lo = (x - hi.astype(F32)).astype(BF16)
    return hi, lo


def _post_kernel(*refs, kind, d, tm, dk):
    if kind == 0:
        of_ref, ob_ref, z_ref, og_ref = refs[:4]
        refs = refs[4:]
    else:
        o_ref = refs[0]
        refs = refs[1:]
    (x_ref, mod_ref, wo_ref, g_ref, wrh_ref, wrl_ref, br_ref,
     xo_ref, h_ref, ids_ref, cnt_ref, base) = refs
    j = pl.program_id(0)

    @pl.when(j == 0)
    def _():
        base[...] = jnp.zeros_like(base)

    if kind == 0:
        parts = []
        for hh in range(d // dk):
            sl = slice(hh * dk, (hh + 1) * dk)
            o = of_ref[:, sl].astype(F32) + ob_ref[:, sl].astype(F32)
            o = o * lax.rsqrt(jnp.mean(o * o, axis=-1, keepdims=True) + EPS) * og_ref[...]
            parts.append((o * _silu(z_ref[:, sl].astype(F32))).astype(BF16))
        o_in = jnp.concatenate(parts, axis=1)
    else:
        o_in = o_ref[...]
    mod = mod_ref[...]
    x = x_ref[...] + mod[2:3, :] * jnp.dot(o_in, wo_ref[...], preferred_element_type=F32)
    xo_ref[...] = x
    h = _norm_mod(x, g_ref[...], mod[3:4, :], mod[4:5, :])
    h_ref[:, :d] = h
    hi, lo = _split_bf16(h)
    logits = (jnp.dot(hi, wrh_ref[...], preferred_element_type=F32)
              + jnp.dot(lo, wrh_ref[...], preferred_element_type=F32)
              + jnp.dot(hi, wrl_ref[...], preferred_element_type=F32)) + br_ref[...]
    lane = lax.broadcasted_iota(I32, logits.shape, 1)
    big = jnp.int32(1 << 20)
    is_g = lane < MOE_GROUPS
    gl = jnp.where(is_g, logits, NEG)
    gmax = jnp.max(gl, axis=-1, keepdims=True)
    gsel = jnp.min(jnp.where(gl == gmax, lane, big), axis=-1, keepdims=True)
    p_group = 1.0 / jnp.sum(jnp.where(is_g, jnp.exp(gl - gmax), 0.0), axis=-1, keepdims=True)
    in_grp = jnp.logical_and(lane >= MOE_GROUPS + gsel * MOE_PER_GROUP,
                             lane < MOE_GROUPS + (gsel + 1) * MOE_PER_GROUP)
    el = jnp.where(in_grp, logits, NEG)
    v0 = jnp.max(el, axis=-1, keepdims=True)
    i0 = jnp.min(jnp.where(el == v0, lane, big), axis=-1, keepdims=True)
    el1 = jnp.where(lane == i0, NEG, el)
    v1 = jnp.max(el1, axis=-1, keepdims=True)
    i1 = jnp.min(jnp.where(el1 == v1, lane, big), axis=-1, keepdims=True)
    e1 = jnp.exp(v1 - v0)
    w0 = p_group / (1.0 + e1)
    w1 = p_group * e1 / (1.0 + e1)
    a0 = i0 - MOE_GROUPS - gsel * MOE_PER_GROUP
    a1 = i1 - MOE_GROUPS - gsel * MOE_PER_GROUP
    lo_e = jnp.minimum(a0, a1)
    hi_e = jnp.maximum(a0, a1)
    lof = lo_e.astype(F32)
    pair = (lof * MOE_PER_GROUP - lof * (lof + 1.0) * 0.5).astype(I32) + (hi_e - lo_e - 1)
    cls = gsel * MOE_PAIRS + pair
    first_lo = a0 < a1
    g_lo = jnp.where(first_lo, w0, w1)
    g_hi = jnp.where(first_lo, w1, w0)
    oh = lane == cls
    onehot = jnp.where(oh, 1.0, 0.0)
    ri = lax.broadcasted_iota(I32, (tm, tm), 0)
    ci = lax.broadcasted_iota(I32, (tm, tm), 1)
    tri = jnp.where(ri > ci, 1.0, 0.0).astype(BF16)
    before = base[...] + jnp.dot(tri, onehot.astype(BF16), preferred_element_type=F32)
    rank = jnp.sum(jnp.where(oh, before, 0.0), axis=-1, keepdims=True)
    new_base = base[...] + jnp.sum(onehot, axis=0, keepdims=True)
    base[...] = new_base
    cnt_ref[...] = new_base
    ids = jnp.where(lane == 0, cls, jnp.where(lane == 1, rank.astype(I32), 0))
    ids_ref[...] = ids.T[0:SUBLANES, :]
    h_ref[:, d:] = jnp.where(lane == 0, g_lo, jnp.where(lane == 1, g_hi, 0.0))


def _post_mixer(kind, mixer_out, xall, mod_i, w_out, ffn_gain, w_group, b_group, w_expert, b_expert,
                dims, o_gain=None):
    bsz, ctx_len, seq, d, tm = dims
    ltot = ctx_len + seq
    nt, nct = ltot // tm, ctx_len // tm
    n_tiles = bsz * nt
    rows = n_tiles * tm
    dk = d // HEADS
    mrow = _mod_row_map(nt, nct, bsz)
    wr = jnp.pad(jnp.concatenate([w_group, w_expert], axis=1),
                 ((0, 0), (0, LANES - MOE_GROUPS - MOE_EXPERTS)))
    wr_hi = wr.astype(BF16)
    wr_lo = (wr - wr_hi.astype(F32)).astype(BF16)
    br = jnp.pad(jnp.concatenate([b_group, b_expert]), (0, LANES - MOE_GROUPS - MOE_EXPERTS)).reshape(1, LANES)
    const = lambda j: (0, 0)
    row = lambda j: (j, 0)
    lead_specs = [pl.BlockSpec((tm, d), row)] * len(mixer_out)
    lead_args = list(mixer_out)
    if kind == 0:
        lead_specs.append(pl.BlockSpec((1, dk), const))
        lead_args.append(o_gain.reshape(1, dk))
    n_lead = len(lead_args)
    outs = pl.pallas_call(
        functools.partial(_post_kernel, kind=kind, d=d, tm=tm, dk=dk),
        grid=(n_tiles,),
        in_specs=lead_specs + [pl.BlockSpec((tm, d), row),
                               pl.BlockSpec((None, 6, d), lambda j: (mrow(j), 0, 0)),
                               pl.BlockSpec((w_out.shape[0], d), const),
                               pl.BlockSpec((1, d), const),
                               pl.BlockSpec((d, LANES), const),
                               pl.BlockSpec((d, LANES), const),
                               pl.BlockSpec((1, LANES), const)],
        out_specs=[pl.BlockSpec((tm, d), row), pl.BlockSpec((tm, d + LANES), row),
                   pl.BlockSpec((SUBLANES, tm), lambda j: (0, j)),
                   pl.BlockSpec((1, LANES), const)],
        out_shape=[jax.ShapeDtypeStruct((rows, d), F32), jax.ShapeDtypeStruct((rows, d + LANES), F32),
                   jax.ShapeDtypeStruct((SUBLANES, rows), I32),
                   jax.ShapeDtypeStruct((1, LANES), F32)],
        scratch_shapes=[pltpu.VMEM((1, LANES), F32)],
        input_output_aliases={n_lead: 0},
        compiler_params=_cparams("arbitrary"),
        name="post_mixer",
    )(*lead_args, xall, mod_i, w_out.astype(BF16), ffn_gain.reshape(1, d), wr_hi, wr_lo, br)
    return outs


def _dispatch_kernel(zlo_ref, zhi_ref, nu_ref, dest_ref, h_ref, xs_ref, zblk, hbuf, sem, zsem, *, tm, blk,
                     n_blocks, n_tiles):
    j = pl.program_id(0)
    slot = j % 2

    def tile_wait(s):
        pltpu.make_async_copy(hbuf.at[s], xs_ref.at[pl.ds(0, tm)], sem.at[s]).wait()

    def row_copy(src, dst_row, s):
        return pltpu.make_async_copy(src, xs_ref.at[pl.ds(dst_row, 1)], s)

    def blk_copy(bi):
        return pltpu.make_async_copy(zblk, xs_ref.at[pl.ds(pl.multiple_of(bi * blk, blk), blk)], zsem)

    @pl.when(j == 0)
    def _():
        zblk[...] = jnp.zeros_like(zblk)

        def per_class(e, carry):
            lo, hi = zlo_ref[e], zhi_ref[e]

            def start(r, c):
                row_copy(zblk.at[pl.ds(0, 1)], r, zsem).start()
                return c

            def wait(r, c):
                row_copy(zblk.at[pl.ds(0, 1)], r, zsem).wait()
                return c

            lax.fori_loop(lo, hi, start, 0)
            lax.fori_loop(lo, hi, wait, 0)
            return carry

        lax.fori_loop(0, MOE_CLASSES, per_class, 0)

        def tail_start(bi, c):
            blk_copy(bi).start()
            return c

        def tail_wait(bi, c):
            blk_copy(bi).wait()
            return c

        lax.fori_loop(nu_ref[0], n_blocks, tail_start, 0)
        lax.fori_loop(nu_ref[0], n_blocks, tail_wait, 0)

    @pl.when(j > 0)
    def _():
        tile_wait(1 - slot)

    hbuf[slot] = h_ref[...]

    def body(r, c):
        row_copy(hbuf.at[slot, pl.ds(r, 1)], dest_ref[0, r], sem.at[slot]).start()
        return c

    lax.fori_loop(0, tm, body, 0, unroll=8)

    @pl.when(j == n_tiles - 1)
    def _():
        tile_wait(slot)


def _expert_kernel(ea_ref, eb_ref, nu_ref, x_ref, w1a_ref, w3a_ref, w2a_ref, w1b_ref, w3b_ref, w2b_ref,
                   y_ref, *, d):
    j = pl.program_id(0)

    @pl.when(j < nu_ref[0])
    def _():
        x = x_ref[:, :d].astype(BF16)
        gates = x_ref[:, d:]

        def expert(w1_ref, w3_ref, w2_ref):
            a = jnp.dot(x, w1_ref[...], preferred_element_type=F32)
            b = jnp.dot(x, w3_ref[...], preferred_element_type=F32)
            return jnp.dot((_silu(a) * b).astype(BF16), w2_ref[...], preferred_element_type=F32)

        y_ref[...] = (expert(w1a_ref, w3a_ref, w2a_ref) * gates[:, 0:1]
                      + expert(w1b_ref, w3b_ref, w2b_ref) * gates[:, 1:2])

    @pl.when(j >= nu_ref[0])
    def _():
        y_ref[...] = jnp.zeros_like(y_ref)


def _combine_kernel(dest_ref, dnext_ref, x_ref, mod_ref, fg_ref, yb_ref, xo_ref, ybuf, sem, *, tm, final,
                    inner, n_steps):
    j = pl.program_id(0) if inner is None else pl.program_id(0) * inner + pl.program_id(1)
    slot = j % 2

    def gather(d_ref, s):
        def body(r, c):
            pltpu.make_async_copy(yb_ref.at[pl.ds(d_ref[0, r], 1)], ybuf.at[s, pl.ds(r, 1)], sem.at[s]).start()
            return c

        lax.fori_loop(0, tm, body, 0, unroll=8)

    @pl.when(j == 0)
    def _():
        gather(dest_ref, slot)

    @pl.when(j + 1 < n_steps)
    def _():
        gather(dnext_ref, 1 - slot)

    pltpu.make_async_copy(yb_ref.at[pl.ds(0, tm)], ybuf.at[slot], sem.at[slot]).wait()
    x = x_ref[...] + mod_ref[5:6, :] * ybuf[slot]
    if final:
        x = x * lax.rsqrt(jnp.mean(x * x, axis=-1, keepdims=True) + EPS) * fg_ref[...]
    xo_ref[...] = x


def _class_experts():
    lo, hi = [], []
    for g in range(MOE_GROUPS):
        for a in range(MOE_PER_GROUP):
            for b in range(a + 1, MOE_PER_GROUP):
                lo.append(g * MOE_PER_GROUP + a)
                hi.append(g * MOE_PER_GROUP + b)
    return jnp.asarray(lo, I32), jnp.asarray(hi, I32)


def _moe(xall, h, ids, counts, mod_i, w1, w3, w2, final_gain, dims, final):
    bsz, ctx_len, seq, d, tm = dims
    ltot = ctx_len + seq
    nt, nct = ltot // tm, ctx_len // tm
    n_tiles = bsz * nt
    rows = n_tiles * tm
    ncls = MOE_CLASSES
    blk = MOE_BLOCK
    dw = d + LANES
    n_blocks = -(-(rows + ncls * (blk - 1)) // blk)
    cnt = counts[0, :ncls].astype(I32)
    padded = (cnt + blk - 1) // blk * blk
    pad_end = jnp.cumsum(padded)
    pad_start = pad_end - padded
    cls_of_row = ids[0]
    start_of_row = jnp.sum(jnp.where(cls_of_row[:, None] == jnp.arange(ncls, dtype=I32)[None, :],
                                     pad_start[None, :], 0), axis=1)
    dest = (start_of_row + ids[1]).astype(I32).reshape(n_tiles, 1, tm)
    n_used = (pad_end[-1] // blk).astype(I32).reshape(1)
    blk_first = jnp.arange(n_blocks, dtype=I32) * blk
    blk_class = jnp.minimum(jnp.sum((pad_end[None, :] <= blk_first[:, None]).astype(I32), axis=1), ncls - 1)
    cls_lo, cls_hi = _class_experts()
    blk_lo, blk_hi = cls_lo[blk_class], cls_hi[blk_class]
    smem_dest = pl.BlockSpec((None, 1, tm), lambda j, *_: (j, 0, 0), memory_space=pltpu.SMEM)
    xs = pl.pallas_call(
        functools.partial(_dispatch_kernel, tm=tm, blk=blk, n_blocks=n_blocks, n_tiles=n_tiles),
        grid_spec=pltpu.PrefetchScalarGridSpec(
            num_scalar_prefetch=3, grid=(n_tiles,),
            in_specs=[smem_dest, pl.BlockSpec((tm, dw), lambda j, *_: (j, 0))],
            out_specs=pl.BlockSpec(memory_space=pl.ANY),
            scratch_shapes=[pltpu.VMEM((blk, dw), F32), pltpu.VMEM((2, tm, dw), F32),
                            pltpu.SemaphoreType.DMA((2,)), pltpu.SemaphoreType.DMA]),
        out_shape=jax.ShapeDtypeStruct((n_blocks * blk, dw), F32),
        compiler_params=_cparams("arbitrary"),
        name="moe_dispatch",
    )((pad_start + cnt).astype(I32), pad_end.astype(I32), n_used, dest, h)

    def xmap(j, ea, eb, nu):
        return (jnp.minimum(j, nu[0] - 1), 0)

    def wmap_lo(j, ea, eb, nu):
        return (ea[jnp.minimum(j, nu[0] - 1)], 0, 0)

    def wmap_hi(j, ea, eb, nu):
        return (eb[jnp.minimum(j, nu[0] - 1)], 0, 0)

    f = w1.shape[-1]
    w1b, w3b, w2b = w1.astype(BF16), w3.astype(BF16), w2.astype(BF16)
    yb = pl.pallas_call(
        functools.partial(_expert_kernel, d=d),
        grid_spec=pltpu.PrefetchScalarGridSpec(
            num_scalar_prefetch=3, grid=(n_blocks,),
            in_specs=[pl.BlockSpec((blk, dw), xmap),
                      pl.BlockSpec((None, d, f), wmap_lo),
                      pl.BlockSpec((None, d, f), wmap_lo),
                      pl.BlockSpec((None, f, d), wmap_lo),
                      pl.BlockSpec((None, d, f), wmap_hi),
                      pl.BlockSpec((None, d, f), wmap_hi),
                      pl.BlockSpec((None, f, d), wmap_hi)],
            out_specs=pl.BlockSpec((blk, d), lambda j, ea, eb, nu: (j, 0))),
        out_shape=jax.ShapeDtypeStruct((n_blocks * blk, d), F32),
        compiler_params=_cparams("arbitrary"),
        name="moe_experts",
    )(blk_lo, blk_hi, n_used, xs, w1b, w3b, w2b, w1b, w3b, w2b)

    if final:
        nlt = seq // tm
        grid, inner, n_steps = (bsz, nlt), nlt, bsz * nlt
        lin = lambda b, i: b * nlt + i
        tile_of = lambda l: (l // nlt) * nt + nct + l % nlt
        mod_map = lambda b, i: (b, 0, 0)
        out_map = lambda b, i: (b * nlt + i, 0)
        out_rows, aliases = bsz * seq, {}
    else:
        mrow = _mod_row_map(nt, nct, bsz)
        grid, inner, n_steps = (n_tiles,), None, n_tiles
        lin = lambda j: j
        tile_of = lambda l: l
        mod_map = lambda j: (mrow(j), 0, 0)
        out_map = lambda j: (j, 0)
        out_rows, aliases = rows, {2: 0}
    tile = lambda *g: tile_of(lin(*g))
    tile_next = lambda *g: tile_of(jnp.minimum(lin(*g) + 1, n_steps - 1))
    out = pl.pallas_call(
        functools.partial(_combine_kernel, tm=tm, final=final, inner=inner, n_steps=n_steps),
        grid=grid,
        in_specs=[pl.BlockSpec((None, 1, tm), lambda *g: (tile(*g), 0, 0), memory_space=pltpu.SMEM),
                  pl.BlockSpec((None, 1, tm), lambda *g: (tile_next(*g), 0, 0), memory_space=pltpu.SMEM),
                  pl.BlockSpec((tm, d), lambda *g: (tile(*g), 0)),
                  pl.BlockSpec((None, 6, d), mod_map),
                  pl.BlockSpec((1, d), lambda *g: (0, 0)),
                  pl.BlockSpec(memory_space=pl.ANY)],
        out_specs=pl.BlockSpec((tm, d), out_map),
        out_shape=jax.ShapeDtypeStruct((out_rows, d), F32),
        scratch_shapes=[pltpu.VMEM((2, tm, d), F32), pltpu.SemaphoreType.DMA((2,))],
        input_output_aliases=aliases,
        compiler_params=_cparams(*(("arbitrary",) * len(grid))),
        name="moe_combine",
    )(dest, dest, xall, mod_i, final_gain.reshape(1, d), yb)
    return out


def kernel(x, c, ctx, c_ctx, w_mod, b_mod, norm_mix, norm_ffn, gdn_w_in, gdn_conv, gdn_a_log, gdn_dt_bias, gdn_norm, gdn_w_out, diff_w_qkv, diff_lambda, diff_norm, diff_w_out, mla_w_down, mla_q_norm, mla_kv_norm, mla_w_uq, mla_w_ukv, mla_w_out, moe_w_group, moe_b_group, moe_w_expert, moe_b_expert, moe_w1, moe_w3, moe_w2, final_norm):
    bsz, seq, d = x.shape
    ctx_len = ctx.shape[1]
    depth = w_mod.shape[0]
    tm = _row_tile(ctx_len)
    assert d % LANES == 0 and d // HEADS == LANES
    assert ctx_len % tm == 0 and seq % tm == 0 and ctx_len % GDN_CHUNK == 0 and seq % GDN_CHUNK == 0
    dims = (bsz, ctx_len, seq, d, tm)
    ltot = ctx_len + seq
    xall = jnp.concatenate([ctx, x], axis=1).reshape(bsz * ltot, d)
    mod = _mod_vectors(c, c_ctx, w_mod, b_mod)
    tables = _rope_tables(seq, ctx_len)
    for i in range(depth):
        kind, j = i % N_MIXERS, i // N_MIXERS
        if kind == 0:
            mixer_out = _gdn_mixer(xall, mod[i], norm_mix[i], gdn_w_in[j], gdn_conv[j], gdn_a_log[j],
                                   gdn_dt_bias[j], dims)
            w_out, o_gain = gdn_w_out[j], gdn_norm[j]
        elif kind == 1:
            lam_init = 0.8 - 0.6 * math.exp(-0.3 * i)
            mixer_out = _diff_mixer(xall, mod[i], norm_mix[i], diff_w_qkv[j], diff_lambda[j], diff_norm[j],
                                    dims, tables, lam_init)
            w_out, o_gain = diff_w_out[j], None
        else:
            mixer_out = _mla_mixer(xall, mod[i], norm_mix[i], mla_w_down[j], mla_q_norm[j], mla_kv_norm[j],
                                   mla_w_uq[j], mla_w_ukv[j], dims, tables)
            w_out, o_gain = mla_w_out[j], None
        xall, h, ids, counts = _post_mixer(kind, mixer_out, xall, mod[i], w_out, norm_ffn[i],
                                           moe_w_group[i], moe_b_group[i], moe_w_expert[i],
                                           moe_b_expert[i], dims, o_gain)
        xall = _moe(xall, h, ids, counts, mod[i], moe_w1[i], moe_w3[i], moe_w2[i], final_norm, dims,
                    final=(i == depth - 1))
    return xall.reshape(bsz, seq, d)
```

```python
import functools
import math

import jax
import jax.numpy as jnp
from jax import lax
from jax.experimental import pallas as pl
from jax.experimental.pallas import tpu as pltpu

F32 = jnp.float32
BF16 = jnp.bfloat16
I32 = jnp.int32
HIGHEST = lax.Precision.HIGHEST

LANES = 128
SUBLANES = 8
VMEM_LIMIT = 56 * 1024 * 1024

EPS = 1e-6
GRID_W = 64
ROPE_THETA = 10000.0
N_MIXERS = 3
HEADS = 8
GDN_CONV = 5
GDN_CHUNK = 64
DIFF_SUBLN_EPS = 1e-5
MLA_NOPE = 128
MLA_ROPE = 64
MLA_V = 128
MOE_GROUPS = 4
MOE_PER_GROUP = 8
MOE_EXPERTS = MOE_GROUPS * MOE_PER_GROUP
MOE_TOP_K = 2
MOE_PAIRS = MOE_PER_GROUP * (MOE_PER_GROUP - 1) // 2
MOE_CLASSES = MOE_GROUPS * MOE_PAIRS
MOE_BLOCK = 256
NEG = -1e30


def _cparams(*sem):
    return pltpu.CompilerParams(dimension_semantics=sem, vmem_limit_bytes=VMEM_LIMIT)


def _row_tile(ctx_len):
    return 256 if ctx_len % 256 == 0 else 128


def _mod_row_map(nt, nct, bsz):
    def f(j):
        return jnp.where(j % nt < nct, bsz, j // nt)
    return f


def _norm_mod(x, gain, shift, scale, eps=EPS):
    var = jnp.mean(x * x, axis=-1, keepdims=True)
    y = x * lax.rsqrt(var + eps) * gain
    return y * (1.0 + scale) + shift


def _silu(x):
    return x * jax.nn.sigmoid(x)


def _mod_kernel(c_ref, w_ref, b_ref, o_ref):
    s = _silu(c_ref[...])
    o_ref[...] = jnp.dot(s, w_ref[...], precision=HIGHEST, preferred_element_type=F32) + b_ref[...]


def _mod_vectors(c, c_ctx, w_mod, b_mod):
    depth, d, n = w_mod.shape
    bsz = c.shape[0]
    rows = -(-(bsz + 1) // SUBLANES) * SUBLANES
    cc = jnp.zeros((rows, d), F32).at[:bsz].set(c).at[bsz].set(c_ctx)
    tn = 512
    out = pl.pallas_call(
        _mod_kernel,
        grid=(depth, n // tn),
        in_specs=[pl.BlockSpec((rows, d), lambda i, j: (0, 0)),
                  pl.BlockSpec((None, d, tn), lambda i, j: (i, 0, j)),
                  pl.BlockSpec((None, 1, tn), lambda i, j: (i, 0, j))],
        out_specs=pl.BlockSpec((None, rows, tn), lambda i, j: (i, 0, j)),
        out_shape=jax.ShapeDtypeStruct((depth, rows, n), F32),
        compiler_params=_cparams("parallel", "parallel"),
        name="mod_vectors",
    )(cc, w_mod, b_mod.reshape(depth, 1, n))
    return out.reshape(depth, rows, 6, d)


def _rope_tables(seq, ctx_len):
    quarter = 16
    inv_freq = ROPE_THETA ** (-jnp.arange(quarter, dtype=F32) / quarter)
    t = jnp.arange(seq)
    row = (t // GRID_W).astype(F32)[:, None] * inv_freq
    col = (t % GRID_W).astype(F32)[:, None] * inv_freq
    cos = jnp.concatenate([jnp.cos(row), jnp.cos(row), jnp.cos(col), jnp.cos(col)], axis=1)
    sin = jnp.concatenate([-jnp.sin(row), jnp.sin(row), -jnp.sin(col), jnp.sin(col)], axis=1)
    cos = jnp.concatenate([jnp.ones((ctx_len, 64), F32), cos], axis=0)
    sin = jnp.concatenate([jnp.zeros((ctx_len, 64), F32), sin], axis=0)
    return jnp.tile(cos, (1, 2)), jnp.tile(sin, (1, 2))


def _rope128(blk, cos, sin):
    lane = lax.broadcasted_iota(I32, blk.shape, 1)
    first = (lane % 32) < 16
    partner = jnp.where(first, pltpu.roll(blk, LANES - 16, 1), pltpu.roll(blk, 16, 1))
    return blk * cos + partner * sin


def _diff_proj_kernel(x_ref, mod_ref, g_ref, w_ref, cos_ref, sin_ref, o_ref, vt_ref, *, d, q_scale):
    h = _norm_mod(x_ref[...], g_ref[...], mod_ref[0:1, :], mod_ref[1:2, :])
    p = jnp.dot(h.astype(BF16), w_ref[...], preferred_element_type=F32)
    cos = cos_ref[...]
    sin = sin_ref[...]
    nqk = 2 * d // LANES
    for cb in range(nqk):
        r = _rope128(p[:, cb * LANES:(cb + 1) * LANES], cos, sin)
        if cb < nqk // 2:
            r = r * q_scale
        o_ref[:, cb * LANES:(cb + 1) * LANES] = r.astype(BF16)
    vt_ref[...] = p[:, 2 * d:].T.astype(BF16)


KEY_CHUNKS = (768, 512, 384, 256, 128)


def _pick_tk(n, cands=KEY_CHUNKS):
    for cand in cands:
        if n % cand == 0:
            return cand
    raise ValueError(n)


def _flash_t(streams, k_ref, vt_ref, m_ref, l_ref, acc_ref, nsteps, tk):
    for s in range(len(streams)):
        m_ref[s] = jnp.full(m_ref.shape[1:], NEG, F32)
        l_ref[s] = jnp.zeros(l_ref.shape[1:], F32)
        acc_ref[s] = jnp.zeros(acc_ref.shape[1:], F32)
    nt_dims = (((1,), (1,)), ((), ()))

    def scores(i):
        return [lax.dot_general(k_ref[i * tk:(i + 1) * tk, kc], q, nt_dims, preferred_element_type=F32)
                for q, kc, _ in streams]

    sts = scores(0)
    for i in range(nsteps):
        nxt = scores(i + 1) if i + 1 < nsteps else None
        alphas, ps = [], []
        for s, st in enumerate(sts):
            m_old = m_ref[s]
            m_new = jnp.maximum(m_old, jnp.max(st, axis=0, keepdims=True))
            alpha = jnp.exp2(m_old - m_new)
            p = jnp.exp2(st - m_new)
            l_ref[s] = alpha * l_ref[s] + jnp.sum(p, axis=0, keepdims=True)
            m_ref[s] = m_new
            alphas.append(alpha)
            ps.append(p.astype(BF16))
        pvs = [jnp.dot(vt_ref[vr, i * tk:(i + 1) * tk], p, preferred_element_type=F32)
               for (_, _, vr), p in zip(streams, ps)]
        for s in range(len(streams)):
            acc_ref[s] = alphas[s] * acc_ref[s] + pvs[s]
        sts = nxt


def _flash_ctx_or_all(streams, k_ref, vt_ref, m_ref, l_ref, acc_ref, nct, ctx_len, ltot, cands=KEY_CHUNKS):
    i = pl.program_id(2)
    tk_c, tk_l = _pick_tk(ctx_len, cands), _pick_tk(ltot, cands)

    @pl.when(i < nct)
    def _():
        _flash_t(streams, k_ref, vt_ref, m_ref, l_ref, acc_ref, ctx_len // tk_c, tk_c)

    @pl.when(i >= nct)
    def _():
        _flash_t(streams, k_ref, vt_ref, m_ref, l_ref, acc_ref, ltot // tk_l, tk_l)


ATTN_HEADS_PER_STEP = 4
MLA_HEADS_PER_STEP = 4


def _diff_attn_kernel(lam_ref, gain_ref, q_ref, k_ref, vt_ref, o_ref, m_ref, l_ref, acc_ref, *, nct,
                      ctx_len, ltot, lam_init, hw):
    streams = []
    for g in range(ATTN_HEADS_PER_STEP):
        cols = slice(g * hw, (g + 1) * hw)
        q = q_ref[:, cols]
        lane = lax.broadcasted_iota(I32, q.shape, 1)
        zero = jnp.zeros_like(q)
        streams.append((jnp.where(lane < hw // 2, q, zero), cols, cols))
        streams.append((jnp.where(lane >= hw // 2, q, zero), cols, cols))
    _flash_ctx_or_all(streams, k_ref, vt_ref, m_ref, l_ref, acc_ref, nct, ctx_len, ltot)
    lv = lam_ref[...]
    lam = (jnp.exp(jnp.sum(lv[0:1] * lv[1:2], keepdims=True))
           - jnp.exp(jnp.sum(lv[2:3] * lv[3:4], keepdims=True)) + lam_init)
    for g in range(ATTN_HEADS_PER_STEP):
        o = acc_ref[2 * g] / l_ref[2 * g] - lam * (acc_ref[2 * g + 1] / l_ref[2 * g + 1])
        var = jnp.mean(o * o, axis=0, keepdims=True)
        o = o * lax.rsqrt(var + DIFF_SUBLN_EPS) * gain_ref[...] * (1.0 - lam_init)
        o_ref[:, g * hw:(g + 1) * hw] = o.T.astype(BF16)


def _diff_mixer(xall, mod_i, norm_gain, w_qkv, lam_vec, sub_gain, dims, tables, lam_init):
    bsz, ctx_len, seq, d, tm = dims
    ltot = ctx_len + seq
    nt, nct = ltot // tm, ctx_len // tm
    n_tiles = bsz * nt
    dh = d // HEADS // 2
    cos, sin = tables
    mrow = _mod_row_map(nt, nct, bsz)
    rows = n_tiles * tm
    qk, vt = pl.pallas_call(
        functools.partial(_diff_proj_kernel, d=d, q_scale=dh ** -0.5 * math.log2(math.e)),
        grid=(n_tiles,),
        in_specs=[pl.BlockSpec((tm, d), lambda j: (j, 0)),
                  pl.BlockSpec((None, 6, d), lambda j: (mrow(j), 0, 0)),
                  pl.BlockSpec((1, d), lambda j: (0, 0)),
                  pl.BlockSpec((d, 3 * d), lambda j: (0, 0)),
                  pl.BlockSpec((tm, LANES), lambda j: (j % nt, 0)),
                  pl.BlockSpec((tm, LANES), lambda j: (j % nt, 0))],
        out_specs=[pl.BlockSpec((tm, 2 * d), lambda j: (j, 0)),
                   pl.BlockSpec((d, tm), lambda j: (0, j))],
        out_shape=[jax.ShapeDtypeStruct((rows, 2 * d), BF16), jax.ShapeDtypeStruct((d, rows), BF16)],
        compiler_params=_cparams("parallel"),
        name="diff_proj",
    )(xall, mod_i, norm_gain.reshape(1, d), w_qkv.astype(BF16), cos, sin)
    hw = 2 * dh
    hps = ATTN_HEADS_PER_STEP
    hg = HEADS // hps
    o = pl.pallas_call(
        functools.partial(_diff_attn_kernel, nct=nct, ctx_len=ctx_len, ltot=ltot, lam_init=lam_init, hw=hw),
        grid=(bsz, hg, nt),
        in_specs=[pl.BlockSpec((4, dh), lambda b, h, i: (0, 0)),
                  pl.BlockSpec((hw, 1), lambda b, h, i: (0, 0)),
                  pl.BlockSpec((tm, hps * hw), lambda b, h, i: (b * nt + i, h)),
                  pl.BlockSpec((ltot, hps * hw), lambda b, h, i: (b, hg + h), pipeline_mode=pl.Buffered(1)),
                  pl.BlockSpec((hps * hw, ltot), lambda b, h, i: (h, b), pipeline_mode=pl.Buffered(1))],
        out_specs=pl.BlockSpec((tm, hps * hw), lambda b, h, i: (b * nt + i, h)),
        out_shape=jax.ShapeDtypeStruct((rows, d), BF16),
        scratch_shapes=[pltpu.VMEM((2 * hps, 1, tm), F32), pltpu.VMEM((2 * hps, 1, tm), F32),
                        pltpu.VMEM((2 * hps, hw, tm), F32)],
        compiler_params=_cparams("parallel", "parallel", "arbitrary"),
        name="diff_attn",
    )(lam_vec, sub_gain.reshape(hw, 1), qk, qk, vt)
    return (o,)


def _mla_proj_kernel(x_ref, mod_ref, g_ref, wd_ref, qg_ref, kvg_ref, wq_ref, wkv_ref, cos_ref, sin_ref,
                     q_ref, k_ref, v_ref, *, q_lora, kv_lora, scale):
    h = _norm_mod(x_ref[...], g_ref[...], mod_ref[0:1, :], mod_ref[1:2, :])
    p = jnp.dot(h.astype(BF16), wd_ref[...], preferred_element_type=F32)
    cq = p[:, :q_lora]
    cq = cq * lax.rsqrt(jnp.mean(cq * cq, axis=-1, keepdims=True) + EPS) * qg_ref[...]
    ckv = p[:, q_lora:q_lora + kv_lora]
    ckv = ckv * lax.rsqrt(jnp.mean(ckv * ckv, axis=-1, keepdims=True) + EPS) * kvg_ref[...]
    cos = cos_ref[...]
    sin = sin_ref[...]
    kr = _rope128(p[:, q_lora + kv_lora:], cos, sin).astype(BF16)
    q = jnp.dot(cq.astype(BF16), wq_ref[...], preferred_element_type=F32)
    kv = jnp.dot(ckv.astype(BF16), wkv_ref[...], preferred_element_type=F32)
    hq = MLA_NOPE + LANES
    for hh in range(HEADS):
        q_ref[:, hh * hq:hh * hq + MLA_NOPE] = (q[:, hh * hq:hh * hq + MLA_NOPE] * scale).astype(BF16)
        qr = _rope128(q[:, hh * hq + MLA_NOPE:(hh + 1) * hq], cos, sin) * scale
        q_ref[:, hh * hq + MLA_NOPE:(hh + 1) * hq] = qr.astype(BF16)
        k_ref[:, hh * hq:hh * hq + MLA_NOPE] = kv[:, hh * MLA_NOPE:(hh + 1) * MLA_NOPE].astype(BF16)
        k_ref[:, hh * hq + MLA_NOPE:(hh + 1) * hq] = kr
    v_ref[...] = kv[:, HEADS * MLA_NOPE:].T.astype(BF16)


def _mla_attn_kernel(q_ref, k_ref, vt_ref, o_ref, m_ref, l_ref, acc_ref, *, nct, ctx_len, ltot, hq):
    streams = [(q_ref[:, g * hq:(g + 1) * hq], slice(g * hq, (g + 1) * hq), slice(g * MLA_V, (g + 1) * MLA_V))
               for g in range(MLA_HEADS_PER_STEP)]
    _flash_ctx_or_all(streams, k_ref, vt_ref, m_ref, l_ref, acc_ref, nct, ctx_len, ltot, (1408,) + KEY_CHUNKS)
    for g in range(MLA_HEADS_PER_STEP):
        o_ref[:, g * MLA_V:(g + 1) * MLA_V] = (acc_ref[g] / l_ref[g]).T.astype(BF16)


def _mla_mixer(xall, mod_i, norm_gain, w_down, q_gain, kv_gain, w_uq, w_ukv, dims, tables):
    bsz, ctx_len, seq, d, tm = dims
    ltot = ctx_len + seq
    nt, nct = ltot // tm, ctx_len // tm
    n_tiles = bsz * nt
    rows = n_tiles * tm
    q_lora, kv_lora = q_gain.shape[0], kv_gain.shape[0]
    cos, sin = tables
    mrow = _mod_row_map(nt, nct, bsz)
    hq = MLA_NOPE + LANES
    wd = jnp.pad(w_down, ((0, 0), (0, LANES - MLA_ROPE))).astype(BF16)
    nd = wd.shape[1]
    wq = jnp.pad(w_uq.reshape(q_lora, HEADS, MLA_NOPE + MLA_ROPE),
                 ((0, 0), (0, 0), (0, LANES - MLA_ROPE))).reshape(q_lora, HEADS * hq).astype(BF16)
    wkv = w_ukv.reshape(kv_lora, HEADS, MLA_NOPE + MLA_V)
    wkv = jnp.concatenate([wkv[:, :, :MLA_NOPE].reshape(kv_lora, HEADS * MLA_NOPE),
                           wkv[:, :, MLA_NOPE:].reshape(kv_lora, HEADS * MLA_V)], axis=1).astype(BF16)
    scale = (MLA_NOPE + MLA_ROPE) ** -0.5 * math.log2(math.e)
    const = lambda j: (0, 0)
    q, k, v = pl.pallas_call(
        functools.partial(_mla_proj_kernel, q_lora=q_lora, kv_lora=kv_lora, scale=scale),
        grid=(n_tiles,),
        in_specs=[pl.BlockSpec((tm, d), lambda j: (j, 0)),
                  pl.BlockSpec((None, 6, d), lambda j: (mrow(j), 0, 0)),
                  pl.BlockSpec((1, d), const),
                  pl.BlockSpec((d, nd), const),
                  pl.BlockSpec((1, q_lora), const),
                  pl.BlockSpec((1, kv_lora), const),
                  pl.BlockSpec((q_lora, HEADS * hq), const),
                  pl.BlockSpec((kv_lora, HEADS * (MLA_NOPE + MLA_V)), const),
                  pl.BlockSpec((tm, LANES), lambda j: (j % nt, 0)),
                  pl.BlockSpec((tm, LANES), lambda j: (j % nt, 0))],
        out_specs=[pl.BlockSpec((tm, HEADS * hq), lambda j: (j, 0)),
                   pl.BlockSpec((tm, HEADS * hq), lambda j: (j, 0)),
                   pl.BlockSpec((HEADS * MLA_V, tm), lambda j: (0, j))],
        out_shape=[jax.ShapeDtypeStruct((rows, HEADS * hq), BF16),
                   jax.ShapeDtypeStruct((rows, HEADS * hq), BF16),
                   jax.ShapeDtypeStruct((HEADS * MLA_V, rows), BF16)],
        compiler_params=_cparams("parallel"),
        name="mla_proj",
    )(xall, mod_i, norm_gain.reshape(1, d), wd, q_gain.reshape(1, q_lora), kv_gain.reshape(1, kv_lora),
      wq, wkv, cos, sin)
    hps = MLA_HEADS_PER_STEP
    o = pl.pallas_call(
        functools.partial(_mla_attn_kernel, nct=nct, ctx_len=ctx_len, ltot=ltot, hq=hq),
        grid=(bsz, HEADS // hps, nt),
        in_specs=[pl.BlockSpec((tm, hps * hq), lambda b, h, i: (b * nt + i, h)),
                  pl.BlockSpec((ltot, hps * hq), lambda b, h, i: (b, h), pipeline_mode=pl.Buffered(1)),
                  pl.BlockSpec((hps * MLA_V, ltot), lambda b, h, i: (h, b), pipeline_mode=pl.Buffered(1))],
        out_specs=pl.BlockSpec((tm, hps * MLA_V), lambda b, h, i: (b * nt + i, h)),
        out_shape=jax.ShapeDtypeStruct((rows, HEADS * MLA_V), BF16),
        scratch_shapes=[pltpu.VMEM((hps, 1, tm), F32), pltpu.VMEM((hps, 1, tm), F32),
                        pltpu.VMEM((hps, MLA_V, tm), F32)],
        compiler_params=_cparams("parallel", "parallel", "arbitrary"),
        name="mla_attn",
    )(q, k, v)
    return (o,)


def _gdn_proj_kernel(xp_ref, x_ref, xn_ref, mod_ref, g_ref, w_ref, cw_ref, alog_ref, dtb_ref,
                     q_ref, k_ref, v_ref, z_ref, gb_ref, pbuf, *, d, nt, nct, tm, dk):
    j = pl.program_id(0)
    r = j % nt
    first = jnp.logical_or(r == 0, r == nct)
    last = jnp.logical_or(r == nct - 1, r == nt - 1)
    halo = SUBLANES
    xe = jnp.concatenate([xp_ref[...], x_ref[...], xn_ref[...]], axis=0)
    h = _norm_mod(xe, g_ref[...], mod_ref[0:1, :], mod_ref[1:2, :])
    rid = lax.broadcasted_iota(I32, (tm + 2 * halo, 1), 0)
    keep = jnp.logical_and(jnp.logical_or(rid >= halo, jnp.logical_not(first)),
                           jnp.logical_or(rid < tm + halo, jnp.logical_not(last)))
    h = jnp.where(keep, h, 0.0)
    pbuf[...] = jnp.dot(h.astype(BF16), w_ref[...], preferred_element_type=F32)
    half = GDN_CONV // 2

    def conv_block(c0):
        col = pbuf[:, pl.ds(c0, LANES)]
        n = col.shape[0]
        acc = None
        for t in range(GDN_CONV):
            sh = col if t == half else pltpu.roll(col, (half - t) % n, 0)
            term = sh[halo:halo + tm, :] * cw_ref[t:t + 1, pl.ds(c0, LANES)]
            acc = term if acc is None else acc + term
        return _silu(acc)

    for hh in range(3 * d // LANES):
        c0 = hh * LANES
        blk = conv_block(c0)
        if hh < 2 * d // LANES:
            blk = blk * lax.rsqrt(jnp.sum(blk * blk, axis=-1, keepdims=True) + EPS)
        if hh < d // LANES:
            q_ref[:, c0:c0 + LANES] = (blk * dk ** -0.5).astype(BF16)
        elif hh < 2 * d // LANES:
            k_ref[:, c0 - d:c0 - d + LANES] = blk.astype(BF16)
        else:
            v_ref[:, c0 - 2 * d:c0 - 2 * d + LANES] = blk.astype(BF16)
    z_ref[...] = pbuf[halo:halo + tm, 3 * d:4 * d].astype(BF16)
    ab = pbuf[halo:halo + tm, 4 * d:4 * d + LANES]
    lane = lax.broadcasted_iota(I32, ab.shape, 1)
    is_a = (lane % 16) < 8
    g = -jnp.exp(alog_ref[...]) * jax.nn.softplus(ab + dtb_ref[...])
    gb_ref[...] = jnp.where(is_a, g, jax.nn.sigmoid(ab))


TRI_BASE = 16


def _mm(a, b):
    return jnp.dot(a.astype(BF16), b.astype(BF16), preferred_element_type=F32)


def _tri_inverse_many(lms, ri, ci):
    n = lms[0].shape[0]

    def same(s):
        shift = int(math.log2(s))
        return (ri >> shift) == (ci >> shift)

    eye = jnp.where(ri == ci, 1.0, 0.0)
    base = same(TRI_BASE)
    ms = [jnp.where(base, lm, 0.0) for lm in lms]
    ps = [eye - m for m in ms]
    for _ in range(int(math.log2(TRI_BASE)) - 1):
        ms = [_mm(m, m) for m in ms]
        ps = [p + _mm(p, m) for p, m in zip(ps, ms)]
    s = TRI_BASE
    while s < n:
        band = jnp.logical_and(same(2 * s), jnp.logical_not(same(s)))
        ts = [_mm(p, jnp.where(band, lm, 0.0)) for p, lm in zip(ps, lms)]
        ps = [p - _mm(t, p) for p, t in zip(ps, ts)]
        s *= 2
    return ps


def _gdn_prep(probs, ri, ci):
    c = probs[0][0].shape[0]
    nt_dims = (((1,), (1,)), ((), ()))
    incl = {False: ri >= ci, True: ri <= ci}
    strict = {False: ri > ci, True: ri < ci}
    decays, kbs, rhss, qgs, kdecs, glasts = [], [], [], [], [], []
    for q, k, v, gc, gct, beta, upper in probs:
        decays.append(jnp.exp(jnp.where(incl[upper], gc - gct, NEG)))
        kf = k.astype(F32)
        kb = kf * beta
        eg = jnp.exp(gc)
        g_last = gc[0:1, :] if upper else gc[c - 1:c, :]
        kbs.append(kb.astype(BF16))
        rhss.append(jnp.concatenate([v.astype(F32) * beta, kb * eg], axis=1).astype(BF16))
        qgs.append((q.astype(F32) * eg).astype(BF16))
        kdecs.append((kf * jnp.exp(g_last - gc)).astype(BF16))
        glasts.append(g_last)
    kks = [lax.dot_general(kb, p[1], nt_dims, preferred_element_type=F32) for kb, p in zip(kbs, probs)]
    qks = [lax.dot_general(p[0], p[1], nt_dims, preferred_element_type=F32) for p in probs]
    lowers = [jnp.where(strict[p[6]], kk * dec, 0.0) for kk, dec, p in zip(kks, decays, probs)]
    intras = [(qk * dec).astype(BF16) for qk, dec in zip(qks, decays)]
    tinvs = _tri_inverse_many(lowers, ri, ci)
    uws = [jnp.dot(t.astype(BF16), r, preferred_element_type=F32) for t, r in zip(tinvs, rhss)]
    return list(zip(uws, qgs, intras, kdecs, glasts))


def _gdn_advance(preps, states, dv):
    tn_dims = (((0,), (0,)), ((), ()))
    sbs = [st.astype(BF16) for st in states]
    wss = [jnp.dot(p[0][:, dv:].astype(BF16), sb, preferred_element_type=F32) for p, sb in zip(preps, sbs)]
    o1s = [jnp.dot(p[1], sb, preferred_element_type=F32) for p, sb in zip(preps, sbs)]
    v_news = [(p[0][:, :dv] - ws).astype(BF16) for p, ws in zip(preps, wss)]
    o2s = [jnp.dot(p[2], vn, preferred_element_type=F32) for p, vn in zip(preps, v_news)]
    upds = [lax.dot_general(p[3], vn, tn_dims, preferred_element_type=F32) for p, vn in zip(preps, v_news)]
    outs = [o1 + o2 for o1, o2 in zip(o1s, o2s)]
    new_states = [st * jnp.exp(p[4]) + upd for st, p, upd in zip(states, preps, upds)]
    return outs, new_states


GDN_CHUNKS_PER_STEP = 2


def _gdn_scan_kernel(qf_ref, kf_ref, vf_ref, gf_ref, qb_ref, kb_ref, vb_ref, gbk_ref,
                     of_ref, ob_ref, sf, sb, *, dk):
    s = pl.program_id(1)

    @pl.when(s == 0)
    def _():
        sf[...] = jnp.zeros_like(sf)
        sb[...] = jnp.zeros_like(sb)

    c = GDN_CHUNK
    cps = qf_ref.shape[0] // c
    ri = lax.broadcasted_iota(I32, (c, c), 0)
    ci = lax.broadcasted_iota(I32, (c, c), 1)
    tri_l = (ri >= ci).astype(F32)
    tri_u = (ri <= ci).astype(F32)
    dirs = ((qf_ref, kf_ref, vf_ref, gf_ref, of_ref, sf), (qb_ref, kb_ref, vb_ref, gbk_ref, ob_ref, sb))
    probs, sinks = [], []
    for t in range(cps):
        for upper, (q_ref, k_ref, v_ref, g_ref, o_ref, st) in enumerate(dirs):
            j = cps - 1 - t if upper else t
            rows = slice(j * c, (j + 1) * c)
            gbv = g_ref[rows, :]
            csum = jnp.dot(tri_u if upper else tri_l, gbv, precision=HIGHEST, preferred_element_type=F32)
            csum_t = csum.T
            base = 16 * upper
            for hh in range(HEADS):
                sl = slice(hh * dk, (hh + 1) * dk)
                probs.append((q_ref[rows, sl], k_ref[rows, sl], v_ref[rows, sl],
                              csum[:, base + hh:base + hh + 1], csum_t[base + hh:base + hh + 1, :],
                              gbv[:, base + 8 + hh:base + 9 + hh], bool(upper)))
                sinks.append((o_ref, rows, sl))
    preps = _gdn_prep(probs, ri, ci)
    per = 2 * HEADS
    states = [st[hh] for (_, _, _, _, _, st) in dirs for hh in range(HEADS)]
    for t in range(cps):
        outs, states = _gdn_advance(preps[t * per:(t + 1) * per], states, dk)
        for (o_ref, rows, sl), o in zip(sinks[t * per:(t + 1) * per], outs):
            o_ref[rows, sl] = o.astype(BF16)
    for i, (_, _, _, _, _, st) in enumerate(dirs):
        for hh in range(HEADS):
            st[hh] = states[i * HEADS + hh]


def _gdn_mixer(xall, mod_i, norm_gain, w_in, conv_w, a_log, dt_bias, dims):
    bsz, ctx_len, seq, d, tm = dims
    ltot = ctx_len + seq
    nt, nct = ltot // tm, ctx_len // tm
    n_tiles = bsz * nt
    rows = n_tiles * tm
    dk = d // HEADS
    n_in = w_in.shape[1]
    n_pad = -(-n_in // LANES) * LANES
    wp = jnp.pad(w_in, ((0, 0), (0, n_pad - n_in))).astype(BF16)
    zeros8 = jnp.zeros((2, HEADS), F32)
    lay = lambda t: jnp.pad(jnp.concatenate([t, zeros8], axis=1).reshape(1, 4 * HEADS),
                            ((0, 0), (0, LANES - 4 * HEADS)))
    mrow = _mod_row_map(nt, nct, bsz)
    hb = tm // SUBLANES
    last_hblk = rows // SUBLANES - 1
    const = lambda j: (0, 0)
    q, k, v, z, gb = pl.pallas_call(
        functools.partial(_gdn_proj_kernel, d=d, nt=nt, nct=nct, tm=tm, dk=dk),
        grid=(n_tiles,),
        in_specs=[pl.BlockSpec((SUBLANES, d), lambda j: (jnp.maximum(j * hb - 1, 0), 0)),
                  pl.BlockSpec((tm, d), lambda j: (j, 0)),
                  pl.BlockSpec((SUBLANES, d), lambda j: (jnp.minimum((j + 1) * hb, last_hblk), 0)),
                  pl.BlockSpec((None, 6, d), lambda j: (mrow(j), 0, 0)),
                  pl.BlockSpec((1, d), const),
                  pl.BlockSpec((d, n_pad), const),
                  pl.BlockSpec((GDN_CONV, 3 * d), const),
                  pl.BlockSpec((1, LANES), const),
                  pl.BlockSpec((1, LANES), const)],
        out_specs=[pl.BlockSpec((tm, d), lambda j: (j, 0))] * 4 + [pl.BlockSpec((tm, LANES), lambda j: (j, 0))],
        out_shape=[jax.ShapeDtypeStruct((rows, d), BF16)] * 4 + [jax.ShapeDtypeStruct((rows, LANES), F32)],
        scratch_shapes=[pltpu.VMEM((tm + 2 * SUBLANES, n_pad), F32)],
        compiler_params=_cparams("parallel"),
        name="gdn_proj",
    )(xall, xall, xall, mod_i, norm_gain.reshape(1, d), wp, conv_w, lay(a_log), lay(dt_bias))
    c = GDN_CHUNK * GDN_CHUNKS_PER_STEP
    assert ctx_len % c == 0 and ltot % c == 0
    ncl, ncc = ltot // c, ctx_len // c

    def fwd(b, s):
        return (b * ncl + s, 0)

    def bwd(b, s):
        return (b * ncl + jnp.where(s < ncc, ncc - 1 - s, ncl + ncc - 1 - s), 0)

    blk = lambda m: pl.BlockSpec((c, d), m)
    gblk = lambda m: pl.BlockSpec((c, LANES), m)
    o_f, o_b = pl.pallas_call(
        functools.partial(_gdn_scan_kernel, dk=dk),
        grid=(bsz, ncl),
        in_specs=[blk(fwd), blk(fwd), blk(fwd), gblk(fwd), blk(bwd), blk(bwd), blk(bwd), gblk(bwd)],
        out_specs=[blk(fwd), blk(bwd)],
        out_shape=[jax.ShapeDtypeStruct((rows, d), BF16)] * 2,
        scratch_shapes=[pltpu.VMEM((HEADS, dk, dk), F32), pltpu.VMEM((HEADS, dk, dk), F32)],
        compiler_params=_cparams("parallel", "arbitrary"),
        name="gdn_scan",
    )(q, k, v, gb, q, k, v, gb)
    return (o_f, o_b, z)


def _split_bf16(x):
    hi = x.astype(BF16)
    lo = (x - hi.astype(F32)).astype(BF16)
    return hi, lo


def _post_kernel(*refs, kind, d, tm, dk):
    if kind == 0:
        of_ref, ob_ref, z_ref, og_ref = refs[:4]
        refs = refs[4:]
    else:
        o_ref = refs[0]
        refs = refs[1:]
    (x_ref, mod_ref, wo_ref, g_ref, wrh_ref, wrl_ref, br_ref,
     xo_ref, h_ref, ids_ref, cnt_ref, base) = refs
    j = pl.program_id(0)

    @pl.when(j == 0)
    def _():
        base[...] = jnp.zeros_like(base)

    if kind == 0:
        parts = []
        for hh in range(d // dk):
            sl = slice(hh * dk, (hh + 1) * dk)
            o = of_ref[:, sl].astype(F32) + ob_ref[:, sl].astype(F32)
            o = o * lax.rsqrt(jnp.mean(o * o, axis=-1, keepdims=True) + EPS) * og_ref[...]
            parts.append((o * _silu(z_ref[:, sl].astype(F32))).astype(BF16))
        o_in = jnp.concatenate(parts, axis=1)
    else:
        o_in = o_ref[...]
    mod = mod_ref[...]
    x = x_ref[...] + mod[2:3, :] * jnp.dot(o_in, wo_ref[...], preferred_element_type=F32)
    xo_ref[...] = x
    h = _norm_mod(x, g_ref[...], mod[3:4, :], mod[4:5, :])
    h_ref[:, :d] = h
    hi, lo = _split_bf16(h)
    logits = (jnp.dot(hi, wrh_ref[...], preferred_element_type=F32)
              + jnp.dot(lo, wrh_ref[...], preferred_element_type=F32)
              + jnp.dot(hi, wrl_ref[...], preferred_element_type=F32)) + br_ref[...]
    lane = lax.broadcasted_iota(I32, logits.shape, 1)
    big = jnp.int32(1 << 20)
    is_g = lane < MOE_GROUPS
    gl = jnp.where(is_g, logits, NEG)
    gmax = jnp.max(gl, axis=-1, keepdims=True)
    gsel = jnp.min(jnp.where(gl == gmax, lane, big), axis=-1, keepdims=True)
    p_group = 1.0 / jnp.sum(jnp.where(is_g, jnp.exp(gl - gmax), 0.0), axis=-1, keepdims=True)
    in_grp = jnp.logical_and(lane >= MOE_GROUPS + gsel * MOE_PER_GROUP,
                             lane < MOE_GROUPS + (gsel + 1) * MOE_PER_GROUP)
    el = jnp.where(in_grp, logits, NEG)
    v0 = jnp.max(el, axis=-1, keepdims=True)
    i0 = jnp.min(jnp.where(el == v0, lane, big), axis=-1, keepdims=True)
    el1 = jnp.where(lane == i0, NEG, el)
    v1 = jnp.max(el1, axis=-1, keepdims=True)
    i1 = jnp.min(jnp.where(el1 == v1, lane, big), axis=-1, keepdims=True)
    e1 = jnp.exp(v1 - v0)
    w0 = p_group / (1.0 + e1)
    w1 = p_group * e1 / (1.0 + e1)
    a0 = i0 - MOE_GROUPS - gsel * MOE_PER_GROUP
    a1 = i1 - MOE_GROUPS - gsel * MOE_PER_GROUP
    lo_e = jnp.minimum(a0, a1)
    hi_e = jnp.maximum(a0, a1)
    lof = lo_e.astype(F32)
    pair = (lof * MOE_PER_GROUP - lof * (lof + 1.0) * 0.5).astype(I32) + (hi_e - lo_e - 1)
    cls = gsel * MOE_PAIRS + pair
    first_lo = a0 < a1
    g_lo = jnp.where(first_lo, w0, w1)
    g_hi = jnp.where(first_lo, w1, w0)
    oh = lane == cls
    onehot = jnp.where(oh, 1.0, 0.0)
    ri = lax.broadcasted_iota(I32, (tm, tm), 0)
    ci = lax.broadcasted_iota(I32, (tm, tm), 1)
    tri = jnp.where(ri > ci, 1.0, 0.0).astype(BF16)
    before = base[...] + jnp.dot(tri, onehot.astype(BF16), preferred_element_type=F32)
    rank = jnp.sum(jnp.where(oh, before, 0.0), axis=-1, keepdims=True)
    new_base = base[...] + jnp.sum(onehot, axis=0, keepdims=True)
    base[...] = new_base
    cnt_ref[...] = new_base
    ids = jnp.where(lane == 0, cls, jnp.where(lane == 1, rank.astype(I32), 0))
    ids_ref[...] = ids.T[0:SUBLANES, :]
    h_ref[:, d:] = jnp.where(lane == 0, g_lo, jnp.where(lane == 1, g_hi, 0.0))


def _post_mixer(kind, mixer_out, xall, mod_i, w_out, ffn_gain, w_group, b_group, w_expert, b_expert,
                dims, o_gain=None):
    bsz, ctx_len, seq, d, tm = dims
    ltot = ctx_len + seq
    nt, nct = ltot // tm, ctx_len // tm
    n_tiles = bsz * nt
    rows = n_tiles * tm
    dk = d // HEADS
    mrow = _mod_row_map(nt, nct, bsz)
    wr = jnp.pad(jnp.concatenate([w_group, w_expert], axis=1),
                 ((0, 0), (0, LANES - MOE_GROUPS - MOE_EXPERTS)))
    wr_hi = wr.astype(BF16)
    wr_lo = (wr - wr_hi.astype(F32)).astype(BF16)
    br = jnp.pad(jnp.concatenate([b_group, b_expert]), (0, LANES - MOE_GROUPS - MOE_EXPERTS)).reshape(1, LANES)
    const = lambda j: (0, 0)
    row = lambda j: (j, 0)
    lead_specs = [pl.BlockSpec((tm, d), row)] * len(mixer_out)
    lead_args = list(mixer_out)
    if kind == 0:
        lead_specs.append(pl.BlockSpec((1, dk), const))
        lead_args.append(o_gain.reshape(1, dk))
    n_lead = len(lead_args)
    outs = pl.pallas_call(
        functools.partial(_post_kernel, kind=kind, d=d, tm=tm, dk=dk),
        grid=(n_tiles,),
        in_specs=lead_specs + [pl.BlockSpec((tm, d), row),
                               pl.BlockSpec((None, 6, d), lambda j: (mrow(j), 0, 0)),
                               pl.BlockSpec((w_out.shape[0], d), const),
                               pl.BlockSpec((1, d), const),
                               pl.BlockSpec((d, LANES), const),
                               pl.BlockSpec((d, LANES), const),
                               pl.BlockSpec((1, LANES), const)],
        out_specs=[pl.BlockSpec((tm, d), row), pl.BlockSpec((tm, d + LANES), row),
                   pl.BlockSpec((SUBLANES, tm), lambda j: (0, j)),
                   pl.BlockSpec((1, LANES), const)],
        out_shape=[jax.ShapeDtypeStruct((rows, d), F32), jax.ShapeDtypeStruct((rows, d + LANES), F32),
                   jax.ShapeDtypeStruct((SUBLANES, rows), I32),
                   jax.ShapeDtypeStruct((1, LANES), F32)],
        scratch_shapes=[pltpu.VMEM((1, LANES), F32)],
        input_output_aliases={n_lead: 0},
        compiler_params=_cparams("arbitrary"),
        name="post_mixer",
    )(*lead_args, xall, mod_i, w_out.astype(BF16), ffn_gain.reshape(1, d), wr_hi, wr_lo, br)
    return outs


def _dispatch_kernel(zlo_ref, zhi_ref, nu_ref, dest_ref, h_ref, xs_ref, zblk, hbuf, sem, zsem, *, tm, blk,
                     n_blocks, n_tiles):
    j = pl.program_id(0)
    slot = j % 2

    def tile_wait(s):
        pltpu.make_async_copy(hbuf.at[s], xs_ref.at[pl.ds(0, tm)], sem.at[s]).wait()

    def row_copy(src, dst_row, s):
        return pltpu.make_async_copy(src, xs_ref.at[pl.ds(dst_row, 1)], s)

    def blk_copy(bi):
        return pltpu.make_async_copy(zblk, xs_ref.at[pl.ds(pl.multiple_of(bi * blk, blk), blk)], zsem)

    @pl.when(j == 0)
    def _():
        zblk[...] = jnp.zeros_like(zblk)

        def per_class(e, carry):
            lo, hi = zlo_ref[e], zhi_ref[e]

            def start(r, c):
                row_copy(zblk.at[pl.ds(0, 1)], r, zsem).start()
                return c

            def wait(r, c):
                row_copy(zblk.at[pl.ds(0, 1)], r, zsem).wait()
                return c

            lax.fori_loop(lo, hi, start, 0)
            lax.fori_loop(lo, hi, wait, 0)
            return carry

        lax.fori_loop(0, MOE_CLASSES, per_class, 0)

        def tail_start(bi, c):
            blk_copy(bi).start()
            return c

        def tail_wait(bi, c):
            blk_copy(bi).wait()
            return c

        lax.fori_loop(nu_ref[0], n_blocks, tail_start, 0)
        lax.fori_loop(nu_ref[0], n_blocks, tail_wait, 0)

    @pl.when(j > 0)
    def _():
        tile_wait(1 - slot)

    hbuf[slot] = h_ref[...]

    def body(r, c):
        row_copy(hbuf.at[slot, pl.ds(r, 1)], dest_ref[0, r], sem.at[slot]).start()
        return c

    lax.fori_loop(0, tm, body, 0, unroll=8)

    @pl.when(j == n_tiles - 1)
    def _():
        tile_wait(slot)


def _expert_kernel(ea_ref, eb_ref, nu_ref, x_ref, w1a_ref, w3a_ref, w2a_ref, w1b_ref, w3b_ref, w2b_ref,
                   y_ref, *, d):
    j = pl.program_id(0)

    @pl.when(j < nu_ref[0])
    def _():
        x = x_ref[:, :d].astype(BF16)
        gates = x_ref[:, d:]

        def expert(w1_ref, w3_ref, w2_ref):
            a = jnp.dot(x, w1_ref[...], preferred_element_type=F32)
            b = jnp.dot(x, w3_ref[...], preferred_element_type=F32)
            return jnp.dot((_silu(a) * b).astype(BF16), w2_ref[...], preferred_element_type=F32)

        y_ref[...] = (expert(w1a_ref, w3a_ref, w2a_ref) * gates[:, 0:1]
                      + expert(w1b_ref, w3b_ref, w2b_ref) * gates[:, 1:2])

    @pl.when(j >= nu_ref[0])
    def _():
        y_ref[...] = jnp.zeros_like(y_ref)


def _combine_kernel(dest_ref, dnext_ref, x_ref, mod_ref, fg_ref, yb_ref, xo_ref, ybuf, sem, *, tm, final,
                    inner, n_steps):
    j = pl.program_id(0) if inner is None else pl.program_id(0) * inner + pl.program_id(1)
    slot = j % 2

    def gather(d_ref, s):
        def body(r, c):
            pltpu.make_async_copy(yb_ref.at[pl.ds(d_ref[0, r], 1)], ybuf.at[s, pl.ds(r, 1)], sem.at[s]).start()
            return c

        lax.fori_loop(0, tm, body, 0, unroll=8)

    @pl.when(j == 0)
    def _():
        gather(dest_ref, slot)

    @pl.when(j + 1 < n_steps)
    def _():
        gather(dnext_ref, 1 - slot)

    pltpu.make_async_copy(yb_ref.at[pl.ds(0, tm)], ybuf.at[slot], sem.at[slot]).wait()
    x = x_ref[...] + mod_ref[5:6, :] * ybuf[slot]
    if final:
        x = x * lax.rsqrt(jnp.mean(x * x, axis=-1, keepdims=True) + EPS) * fg_ref[...]
    xo_ref[...] = x


def _class_experts():
    lo, hi = [], []
    for g in range(MOE_GROUPS):
        for a in range(MOE_PER_GROUP):
            for b in range(a + 1, MOE_PER_GROUP):
                lo.append(g * MOE_PER_GROUP + a)
                hi.append(g * MOE_PER_GROUP + b)
    return jnp.asarray(lo, I32), jnp.asarray(hi, I32)


def _moe(xall, h, ids, counts, mod_i, w1, w3, w2, final_gain, dims, final):
    bsz, ctx_len, seq, d, tm = dims
    ltot = ctx_len + seq
    nt, nct = ltot // tm, ctx_len // tm
    n_tiles = bsz * nt
    rows = n_tiles * tm
    ncls = MOE_CLASSES
    blk = MOE_BLOCK
    dw = d + LANES
    n_blocks = -(-(rows + ncls * (blk - 1)) // blk)
    cnt = counts[0, :ncls].astype(I32)
    padded = (cnt + blk - 1) // blk * blk
    pad_end = jnp.cumsum(padded)
    pad_start = pad_end - padded
    cls_of_row = ids[0]
    start_of_row = jnp.sum(jnp.where(cls_of_row[:, None] == jnp.arange(ncls, dtype=I32)[None, :],
                                     pad_start[None, :], 0), axis=1)
    dest = (start_of_row + ids[1]).astype(I32).reshape(n_tiles, 1, tm)
    n_used = (pad_end[-1] // blk).astype(I32).reshape(1)
    blk_first = jnp.arange(n_blocks, dtype=I32) * blk
    blk_class = jnp.minimum(jnp.sum((pad_end[None, :] <= blk_first[:, None]).astype(I32), axis=1), ncls - 1)
    cls_lo, cls_hi = _class_experts()
    blk_lo, blk_hi = cls_lo[blk_class], cls_hi[blk_class]
    smem_dest = pl.BlockSpec((None, 1, tm), lambda j, *_: (j, 0, 0), memory_space=pltpu.SMEM)
    xs = pl.pallas_call(
        functools.partial(_dispatch_kernel, tm=tm, blk=blk, n_blocks=n_blocks, n_tiles=n_tiles),
        grid_spec=pltpu.PrefetchScalarGridSpec(
            num_scalar_prefetch=3, grid=(n_tiles,),
            in_specs=[smem_dest, pl.BlockSpec((tm, dw), lambda j, *_: (j, 0))],
            out_specs=pl.BlockSpec(memory_space=pl.ANY),
            scratch_shapes=[pltpu.VMEM((blk, dw), F32), pltpu.VMEM((2, tm, dw), F32),
                            pltpu.SemaphoreType.DMA((2,)), pltpu.SemaphoreType.DMA]),
        out_shape=jax.ShapeDtypeStruct((n_blocks * blk, dw), F32),
        compiler_params=_cparams("arbitrary"),
        name="moe_dispatch",
    )((pad_start + cnt).astype(I32), pad_end.astype(I32), n_used, dest, h)

    def xmap(j, ea, eb, nu):
        return (jnp.minimum(j, nu[0] - 1), 0)

    def wmap_lo(j, ea, eb, nu):
        return (ea[jnp.minimum(j, nu[0] - 1)], 0, 0)

    def wmap_hi(j, ea, eb, nu):
        return (eb[jnp.minimum(j, nu[0] - 1)], 0, 0)

    f = w1.shape[-1]
    w1b, w3b, w2b = w1.astype(BF16), w3.astype(BF16), w2.astype(BF16)
    yb = pl.pallas_call(
        functools.partial(_expert_kernel, d=d),
        grid_spec=pltpu.PrefetchScalarGridSpec(
            num_scalar_prefetch=3, grid=(n_blocks,),
            in_specs=[pl.BlockSpec((blk, dw), xmap),
                      pl.BlockSpec((None, d, f), wmap_lo),
                      pl.BlockSpec((None, d, f), wmap_lo),
                      pl.BlockSpec((None, f, d), wmap_lo),
                      pl.BlockSpec((None, d, f), wmap_hi),
                      pl.BlockSpec((None, d, f), wmap_hi),
                      pl.BlockSpec((None, f, d), wmap_hi)],
            out_specs=pl.BlockSpec((blk, d), lambda j, ea, eb, nu: (j, 0))),
        out_shape=jax.ShapeDtypeStruct((n_blocks * blk, d), F32),
        compiler_params=_cparams("arbitrary"),
        name="moe_experts",
    )(blk_lo, blk_hi, n_used, xs, w1b, w3b, w2b, w1b, w3b, w2b)

    if final:
        nlt = seq // tm
        grid, inner, n_steps = (bsz, nlt), nlt, bsz * nlt
        lin = lambda b, i: b * nlt + i
        tile_of = lambda l: (l // nlt) * nt + nct + l % nlt
        mod_map = lambda b, i: (b, 0, 0)
        out_map = lambda b, i: (b * nlt + i, 0)
        out_rows, aliases = bsz * seq, {}
    else:
        mrow = _mod_row_map(nt, nct, bsz)
        grid, inner, n_steps = (n_tiles,), None, n_tiles
        lin = lambda j: j
        tile_of = lambda l: l
        mod_map = lambda j: (mrow(j), 0, 0)
        out_map = lambda j: (j, 0)
        out_rows, aliases = rows, {2: 0}
    tile = lambda *g: tile_of(lin(*g))
    tile_next = lambda *g: tile_of(jnp.minimum(lin(*g) + 1, n_steps - 1))
    out = pl.pallas_call(
        functools.partial(_combine_kernel, tm=tm, final=final, inner=inner, n_steps=n_steps),
        grid=grid,
        in_specs=[pl.BlockSpec((None, 1, tm), lambda *g: (tile(*g), 0, 0), memory_space=pltpu.SMEM),
                  pl.BlockSpec((None, 1, tm), lambda *g: (tile_next(*g), 0, 0), memory_space=pltpu.SMEM),
                  pl.BlockSpec((tm, d), lambda *g: (tile(*g), 0)),
                  pl.BlockSpec((None, 6, d), mod_map),
                  pl.BlockSpec((1, d), lambda *g: (0, 0)),
                  pl.BlockSpec(memory_space=pl.ANY)],
        out_specs=pl.BlockSpec((tm, d), out_map),
        out_shape=jax.ShapeDtypeStruct((out_rows, d), F32),
        scratch_shapes=[pltpu.VMEM((2, tm, d), F32), pltpu.SemaphoreType.DMA((2,))],
        input_output_aliases=aliases,
        compiler_params=_cparams(*(("arbitrary",) * len(grid))),
        name="moe_combine",
    )(dest, dest, xall, mod_i, final_gain.reshape(1, d), yb)
    return out


def kernel(x, c, ctx, c_ctx, w_mod, b_mod, norm_mix, norm_ffn, gdn_w_in, gdn_conv, gdn_a_log, gdn_dt_bias, gdn_norm, gdn_w_out, diff_w_qkv, diff_lambda, diff_norm, diff_w_out, mla_w_down, mla_q_norm, mla_kv_norm, mla_w_uq, mla_w_ukv, mla_w_out, moe_w_group, moe_b_group, moe_w_expert, moe_b_expert, moe_w1, moe_w3, moe_w2, final_norm):
    bsz, seq, d = x.shape
    ctx_len = ctx.shape[1]
    depth = w_mod.shape[0]
    tm = _row_tile(ctx_len)
    assert d % LANES == 0 and d // HEADS == LANES
    assert ctx_len % tm == 0 and seq % tm == 0 and ctx_len % GDN_CHUNK == 0 and seq % GDN_CHUNK == 0
    dims = (bsz, ctx_len, seq, d, tm)
    ltot = ctx_len + seq
    xall = jnp.concatenate([ctx, x], axis=1).reshape(bsz * ltot, d)
    mod = _mod_vectors(c, c_ctx, w_mod, b_mod)
    tables = _rope_tables(seq, ctx_len)
    for i in range(depth):
        kind, j = i % N_MIXERS, i // N_MIXERS
        if kind == 0:
            mixer_out = _gdn_mixer(xall, mod[i], norm_mix[i], gdn_w_in[j], gdn_conv[j], gdn_a_log[j],
                                   gdn_dt_bias[j], dims)
            w_out, o_gain = gdn_w_out[j], gdn_norm[j]
        elif kind == 1:
            lam_init = 0.8 - 0.6 * math.exp(-0.3 * i)
            mixer_out = _diff_mixer(xall, mod[i], norm_mix[i], diff_w_qkv[j], diff_lambda[j], diff_norm[j],
                                    dims, tables, lam_init)
            w_out, o_gain = diff_w_out[j], None
        else:
            mixer_out = _mla_mixer(xall, mod[i], norm_mix[i], mla_w_down[j], mla_q_norm[j], mla_kv_norm[j],
                                   mla_w_uq[j], mla_w_ukv[j], dims, tables)
            w_out, o_gain = mla_w_out[j], None
        xall, h, ids, counts = _post_mixer(kind, mixer_out, xall, mod[i], w_out, norm_ffn[i],
                                           moe_w_group[i], moe_b_group[i], moe_w_expert[i],
                                           moe_b_expert[i], dims, o_gain)
        xall = _moe(xall, h, ids, counts, mod[i], moe_w1[i], moe_w3[i], moe_w2[i], final_norm, dims,
                    final=(i == depth - 1))
    return xall.reshape(bsz, seq, d)
```

```python
import functools
import math

import jax
import jax.numpy as jnp
from jax import lax
from jax.experimental import pallas as pl
from jax.experimental.pallas import tpu as pltpu

F32 = jnp.float32
BF16 = jnp.bfloat16
I32 = jnp.int32
HIGHEST = lax.Precision.HIGHEST

LANES = 128
SUBLANES = 8
VMEM_LIMIT = 56 * 1024 * 1024

EPS = 1e-6
GRID_W = 64
ROPE_THETA = 10000.0
N_MIXERS = 3
HEADS = 8
GDN_CONV = 5
GDN_CHUNK = 64
DIFF_SUBLN_EPS = 1e-5
MLA_NOPE = 128
MLA_ROPE = 64
MLA_V = 128
MOE_GROUPS = 4
MOE_PER_GROUP = 8
MOE_EXPERTS = MOE_GROUPS * MOE_PER_GROUP
MOE_TOP_K = 2
MOE_PAIRS = MOE_PER_GROUP * (MOE_PER_GROUP - 1) // 2
MOE_CLASSES = MOE_GROUPS * MOE_PAIRS
MOE_BLOCK = 256
NEG = -1e30


def _cparams(*sem):
    return pltpu.CompilerParams(dimension_semantics=sem, vmem_limit_bytes=VMEM_LIMIT)


def _row_tile(ctx_len):
    return 256 if ctx_len % 256 == 0 else 128


def _mod_row_map(nt, nct, bsz):
    def f(j):
        return jnp.where(j % nt < nct, bsz, j // nt)
    return f


def _norm_mod(x, gain, shift, scale, eps=EPS):
    var = jnp.mean(x * x, axis=-1, keepdims=True)
    y = x * lax.rsqrt(var + eps) * gain
    return y * (1.0 + scale) + shift


def _silu(x):
    return x * jax.nn.sigmoid(x)


def _mod_kernel(c_ref, w_ref, b_ref, o_ref):
    s = _silu(c_ref[...])
    o_ref[...] = jnp.dot(s, w_ref[...], precision=HIGHEST, preferred_element_type=F32) + b_ref[...]


def _mod_vectors(c, c_ctx, w_mod, b_mod):
    depth, d, n = w_mod.shape
    bsz = c.shape[0]
    rows = -(-(bsz + 1) // SUBLANES) * SUBLANES
    cc = jnp.zeros((rows, d), F32).at[:bsz].set(c).at[bsz].set(c_ctx)
    tn = 512
    out = pl.pallas_call(
        _mod_kernel,
        grid=(depth, n // tn),
        in_specs=[pl.BlockSpec((rows, d), lambda i, j: (0, 0)),
                  pl.BlockSpec((None, d, tn), lambda i, j: (i, 0, j)),
                  pl.BlockSpec((None, 1, tn), lambda i, j: (i, 0, j))],
        out_specs=pl.BlockSpec((None, rows, tn), lambda i, j: (i, 0, j)),
        out_shape=jax.ShapeDtypeStruct((depth, rows, n), F32),
        compiler_params=_cparams("parallel", "parallel"),
        name="mod_vectors",
    )(cc, w_mod, b_mod.reshape(depth, 1, n))
    return out.reshape(depth, rows, 6, d)


def _rope_tables(seq, ctx_len):
    quarter = 16
    inv_freq = ROPE_THETA ** (-jnp.arange(quarter, dtype=F32) / quarter)
    t = jnp.arange(seq)
    row = (t // GRID_W).astype(F32)[:, None] * inv_freq
    col = (t % GRID_W).astype(F32)[:, None] * inv_freq
    cos = jnp.concatenate([jnp.cos(row), jnp.cos(row), jnp.cos(col), jnp.cos(col)], axis=1)
    sin = jnp.concatenate([-jnp.sin(row), jnp.sin(row), -jnp.sin(col), jnp.sin(col)], axis=1)
    cos = jnp.concatenate([jnp.ones((ctx_len, 64), F32), cos], axis=0)
    sin = jnp.concatenate([jnp.zeros((ctx_len, 64), F32), sin], axis=0)
    return jnp.tile(cos, (1, 2)), jnp.tile(sin, (1, 2))


def _rope128(blk, cos, sin):
    lane = lax.broadcasted_iota(I32, blk.shape, 1)
    first = (lane % 32) < 16
    partner = jnp.where(first, pltpu.roll(blk, LANES - 16, 1), pltpu.roll(blk, 16, 1))
    return blk * cos + partner * sin


def _diff_proj_kernel(x_ref, mod_ref, g_ref, w_ref, cos_ref, sin_ref, o_ref, vt_ref, *, d, q_scale):
    h = _norm_mod(x_ref[...], g_ref[...], mod_ref[0:1, :], mod_ref[1:2, :])
    p = jnp.dot(h.astype(BF16), w_ref[...], preferred_element_type=F32)
    cos = cos_ref[...]
    sin = sin_ref[...]
    nqk = 2 * d // LANES
    for cb in range(nqk):
        r = _rope128(p[:, cb * LANES:(cb + 1) * LANES], cos, sin)
        if cb < nqk // 2:
            r = r * q_scale
        o_ref[:, cb * LANES:(cb + 1) * LANES] = r.astype(BF16)
    vt_ref[...] = p[:, 2 * d:].T.astype(BF16)


KEY_CHUNKS = (1408, 768, 512, 384, 256, 128)


def _pick_tk(n, cands=KEY_CHUNKS):
    for cand in cands:
        if n % cand == 0:
            return cand
    raise ValueError(n)


def _flash_t(streams, k_ref, vt_ref, m_ref, l_ref, acc_ref, nsteps, tk):
    for s in range(len(streams)):
        m_ref[s] = jnp.full(m_ref.shape[1:], NEG, F32)
        l_ref[s] = jnp.zeros(l_ref.shape[1:], F32)
        acc_ref[s] = jnp.zeros(acc_ref.shape[1:], F32)
    nt_dims = (((1,), (1,)), ((), ()))

    def scores(i):
        return [lax.dot_general(k_ref[i * tk:(i + 1) * tk, kc], q, nt_dims, preferred_element_type=F32)
                for q, kc, _ in streams]

    sts = scores(0)
    for i in range(nsteps):
        nxt = scores(i + 1) if i + 1 < nsteps else None
        alphas, ps = [], []
        for s, st in enumerate(sts):
            m_old = m_ref[s]
            m_new = jnp.maximum(m_old, jnp.max(st, axis=0, keepdims=True))
            alpha = jnp.exp2(m_old - m_new)
            p = jnp.exp2(st - m_new)
            l_ref[s] = alpha * l_ref[s] + jnp.sum(p, axis=0, keepdims=True)
            m_ref[s] = m_new
            alphas.append(alpha)
            ps.append(p.astype(BF16))
        pvs = [jnp.dot(vt_ref[vr, i * tk:(i + 1) * tk], p, preferred_element_type=F32)
               for (_, _, vr), p in zip(streams, ps)]
        for s in range(len(streams)):
            acc_ref[s] = alphas[s] * acc_ref[s] + pvs[s]
        sts = nxt


def _flash_ctx_or_all(streams, k_ref, vt_ref, m_ref, l_ref, acc_ref, nct, ctx_len, ltot, cands=KEY_CHUNKS):
    i = pl.program_id(2)
    tk_c, tk_l = _pick_tk(ctx_len, cands), _pick_tk(ltot, cands)

    @pl.when(i < nct)
    def _():
        _flash_t(streams, k_ref, vt_ref, m_ref, l_ref, acc_ref, ctx_len // tk_c, tk_c)

    @pl.when(i >= nct)
    def _():
        _flash_t(streams, k_ref, vt_ref, m_ref, l_ref, acc_ref, ltot // tk_l, tk_l)


ATTN_HEADS_PER_STEP = 4
MLA_HEADS_PER_STEP = 4


def _diff_attn_kernel(lam_ref, gain_ref, q_ref, k_ref, vt_ref, o_ref, m_ref, l_ref, acc_ref, *, nct,
                      ctx_len, ltot, lam_init, hw):
    streams = []
    for g in range(ATTN_HEADS_PER_STEP):
        cols = slice(g * hw, (g + 1) * hw)
        q = q_ref[:, cols]
        lane = lax.broadcasted_iota(I32, q.shape, 1)
        zero = jnp.zeros_like(q)
        streams.append((jnp.where(lane < hw // 2, q, zero), cols, cols))
        streams.append((jnp.where(lane >= hw // 2, q, zero), cols, cols))
    _flash_ctx_or_all(streams, k_ref, vt_ref, m_ref, l_ref, acc_ref, nct, ctx_len, ltot)
    lv = lam_ref[...]
    lam = (jnp.exp(jnp.sum(lv[0:1] * lv[1:2], keepdims=True))
           - jnp.exp(jnp.sum(lv[2:3] * lv[3:4], keepdims=True)) + lam_init)
    for g in range(ATTN_HEADS_PER_STEP):
        o = acc_ref[2 * g] / l_ref[2 * g] - lam * (acc_ref[2 * g + 1] / l_ref[2 * g + 1])
        var = jnp.mean(o * o, axis=0, keepdims=True)
        o = o * lax.rsqrt(var + DIFF_SUBLN_EPS) * gain_ref[...] * (1.0 - lam_init)
        o_ref[:, g * hw:(g + 1) * hw] = o.T.astype(BF16)


def _diff_mixer(xall, mod_i, norm_gain, w_qkv, lam_vec, sub_gain, dims, tables, lam_init):
    bsz, ctx_len, seq, d, tm = dims
    ltot = ctx_len + seq
    nt, nct = ltot // tm, ctx_len // tm
    n_tiles = bsz * nt
    dh = d // HEADS // 2
    cos, sin = tables
    mrow = _mod_row_map(nt, nct, bsz)
    rows = n_tiles * tm
    qk, vt = pl.pallas_call(
        functools.partial(_diff_proj_kernel, d=d, q_scale=dh ** -0.5 * math.log2(math.e)),
        grid=(n_tiles,),
        in_specs=[pl.BlockSpec((tm, d), lambda j: (j, 0)),
                  pl.BlockSpec((None, 6, d), lambda j: (mrow(j), 0, 0)),
                  pl.BlockSpec((1, d), lambda j: (0, 0)),
                  pl.BlockSpec((d, 3 * d), lambda j: (0, 0)),
                  pl.BlockSpec((tm, LANES), lambda j: (j % nt, 0)),
                  pl.BlockSpec((tm, LANES), lambda j: (j % nt, 0))],
        out_specs=[pl.BlockSpec((tm, 2 * d), lambda j: (j, 0)),
                   pl.BlockSpec((d, tm), lambda j: (0, j))],
        out_shape=[jax.ShapeDtypeStruct((rows, 2 * d), BF16), jax.ShapeDtypeStruct((d, rows), BF16)],
        compiler_params=_cparams("parallel"),
        name="diff_proj",
    )(xall, mod_i, norm_gain.reshape(1, d), w_qkv.astype(BF16), cos, sin)
    hw = 2 * dh
    hps = ATTN_HEADS_PER_STEP
    hg = HEADS // hps
    o = pl.pallas_call(
        functools.partial(_diff_attn_kernel, nct=nct, ctx_len=ctx_len, ltot=ltot, lam_init=lam_init, hw=hw),
        grid=(bsz, hg, nt),
        in_specs=[pl.BlockSpec((4, dh), lambda b, h, i: (0, 0)),
                  pl.BlockSpec((hw, 1), lambda b, h, i: (0, 0)),
                  pl.BlockSpec((tm, hps * hw), lambda b, h, i: (b * nt + i, h)),
                  pl.BlockSpec((ltot, hps * hw), lambda b, h, i: (b, hg + h), pipeline_mode=pl.Buffered(1)),
                  pl.BlockSpec((hps * hw, ltot), lambda b, h, i: (h, b), pipeline_mode=pl.Buffered(1))],
        out_specs=pl.BlockSpec((tm, hps * hw), lambda b, h, i: (b * nt + i, h)),
        out_shape=jax.ShapeDtypeStruct((rows, d), BF16),
        scratch_shapes=[pltpu.VMEM((2 * hps, 1, tm), F32), pltpu.VMEM((2 * hps, 1, tm), F32),
                        pltpu.VMEM((2 * hps, hw, tm), F32)],
        compiler_params=_cparams("parallel", "parallel", "arbitrary"),
        name="diff_attn",
    )(lam_vec, sub_gain.reshape(hw, 1), qk, qk, vt)
    return (o,)


def _mla_proj_kernel(x_ref, mod_ref, g_ref, wd_ref, qg_ref, kvg_ref, wq_ref, wkv_ref, cos_ref, sin_ref,
                     q_ref, k_ref, v_ref, *, q_lora, kv_lora, scale):
    h = _norm_mod(x_ref[...], g_ref[...], mod_ref[0:1, :], mod_ref[1:2, :])
    p = jnp.dot(h.astype(BF16), wd_ref[...], preferred_element_type=F32)
    cq = p[:, :q_lora]
    cq = cq * lax.rsqrt(jnp.mean(cq * cq, axis=-1, keepdims=True) + EPS) * qg_ref[...]
    ckv = p[:, q_lora:q_lora + kv_lora]
    ckv = ckv * lax.rsqrt(jnp.mean(ckv * ckv, axis=-1, keepdims=True) + EPS) * kvg_ref[...]
    cos = cos_ref[...]
    sin = sin_ref[...]
    kr = _rope128(p[:, q_lora + kv_lora:], cos, sin).astype(BF16)
    q = jnp.dot(cq.astype(BF16), wq_ref[...], preferred_element_type=F32)
    kv = jnp.dot(ckv.astype(BF16), wkv_ref[...], preferred_element_type=F32)
    hq = MLA_NOPE + LANES
    for hh in range(HEADS):
        q_ref[:, hh * hq:hh * hq + MLA_NOPE] = (q[:, hh * hq:hh * hq + MLA_NOPE] * scale).astype(BF16)
        qr = _rope128(q[:, hh * hq + MLA_NOPE:(hh + 1) * hq], cos, sin) * scale
        q_ref[:, hh * hq + MLA_NOPE:(hh + 1) * hq] = qr.astype(BF16)
        k_ref[:, hh * hq:hh * hq + MLA_NOPE] = kv[:, hh * MLA_NOPE:(hh + 1) * MLA_NOPE].astype(BF16)
        k_ref[:, hh * hq + MLA_NOPE:(hh + 1) * hq] = kr
    v_ref[...] = kv[:, HEADS * MLA_NOPE:].T.astype(BF16)


def _mla_attn_kernel(q_ref, k_ref, vt_ref, o_ref, m_ref, l_ref, acc_ref, *, nct, ctx_len, ltot, hq):
    streams = [(q_ref[:, g * hq:(g + 1) * hq], slice(g * hq, (g + 1) * hq), slice(g * MLA_V, (g + 1) * MLA_V))
               for g in range(MLA_HEADS_PER_STEP)]
    _flash_ctx_or_all(streams, k_ref, vt_ref, m_ref, l_ref, acc_ref, nct, ctx_len, ltot)
    for g in range(MLA_HEADS_PER_STEP):
        o_ref[:, g * MLA_V:(g + 1) * MLA_V] = (acc_ref[g] / l_ref[g]).T.astype(BF16)


def _mla_mixer(xall, mod_i, norm_gain, w_down, q_gain, kv_gain, w_uq, w_ukv, dims, tables):
    bsz, ctx_len, seq, d, tm = dims
    ltot = ctx_len + seq
    nt, nct = ltot // tm, ctx_len // tm
    n_tiles = bsz * nt
    rows = n_tiles * tm
    q_lora, kv_lora = q_gain.shape[0], kv_gain.shape[0]
    cos, sin = tables
    mrow = _mod_row_map(nt, nct, bsz)
    hq = MLA_NOPE + LANES
    wd = jnp.pad(w_down, ((0, 0), (0, LANES - MLA_ROPE))).astype(BF16)
    nd = wd.shape[1]
    wq = jnp.pad(w_uq.reshape(q_lora, HEADS, MLA_NOPE + MLA_ROPE),
                 ((0, 0), (0, 0), (0, LANES - MLA_ROPE))).reshape(q_lora, HEADS * hq).astype(BF16)
    wkv = w_ukv.reshape(kv_lora, HEADS, MLA_NOPE + MLA_V)
    wkv = jnp.concatenate([wkv[:, :, :MLA_NOPE].reshape(kv_lora, HEADS * MLA_NOPE),
                           wkv[:, :, MLA_NOPE:].reshape(kv_lora, HEADS * MLA_V)], axis=1).astype(BF16)
    scale = (MLA_NOPE + MLA_ROPE) ** -0.5 * math.log2(math.e)
    const = lambda j: (0, 0)
    q, k, v = pl.pallas_call(
        functools.partial(_mla_proj_kernel, q_lora=q_lora, kv_lora=kv_lora, scale=scale),
        grid=(n_tiles,),
        in_specs=[pl.BlockSpec((tm, d), lambda j: (j, 0)),
                  pl.BlockSpec((None, 6, d), lambda j: (mrow(j), 0, 0)),
                  pl.BlockSpec((1, d), const),
                  pl.BlockSpec((d, nd), const),
                  pl.BlockSpec((1, q_lora), const),
                  pl.BlockSpec((1, kv_lora), const),
                  pl.BlockSpec((q_lora, HEADS * hq), const),
                  pl.BlockSpec((kv_lora, HEADS * (MLA_NOPE + MLA_V)), const),
                  pl.BlockSpec((tm, LANES), lambda j: (j % nt, 0)),
                  pl.BlockSpec((tm, LANES), lambda j: (j % nt, 0))],
        out_specs=[pl.BlockSpec((tm, HEADS * hq), lambda j: (j, 0)),
                   pl.BlockSpec((tm, HEADS * hq), lambda j: (j, 0)),
                   pl.BlockSpec((HEADS * MLA_V, tm), lambda j: (0, j))],
        out_shape=[jax.ShapeDtypeStruct((rows, HEADS * hq), BF16),
                   jax.ShapeDtypeStruct((rows, HEADS * hq), BF16),
                   jax.ShapeDtypeStruct((HEADS * MLA_V, rows), BF16)],
        compiler_params=_cparams("parallel"),
        name="mla_proj",
    )(xall, mod_i, norm_gain.reshape(1, d), wd, q_gain.reshape(1, q_lora), kv_gain.reshape(1, kv_lora),
      wq, wkv, cos, sin)
    hps = MLA_HEADS_PER_STEP
    o = pl.pallas_call(
        functools.partial(_mla_attn_kernel, nct=nct, ctx_len=ctx_len, ltot=ltot, hq=hq),
        grid=(bsz, HEADS // hps, nt),
        in_specs=[pl.BlockSpec((tm, hps * hq), lambda b, h, i: (b * nt + i, h)),
                  pl.BlockSpec((ltot, hps * hq), lambda b, h, i: (b, h), pipeline_mode=pl.Buffered(1)),
                  pl.BlockSpec((hps * MLA_V, ltot), lambda b, h, i: (h, b), pipeline_mode=pl.Buffered(1))],
        out_specs=pl.BlockSpec((tm, hps * MLA_V), lambda b, h, i: (b * nt + i, h)),
        out_shape=jax.ShapeDtypeStruct((rows, HEADS * MLA_V), BF16),
        scratch_shapes=[pltpu.VMEM((hps, 1, tm), F32), pltpu.VMEM((hps, 1, tm), F32),
                        pltpu.VMEM((hps, MLA_V, tm), F32)],
        compiler_params=_cparams("parallel", "parallel", "arbitrary"),
        name="mla_attn",
    )(q, k, v)
    return (o,)


def _gdn_proj_kernel(xp_ref, x_ref, xn_ref, mod_ref, g_ref, w_ref, cw_ref, alog_ref, dtb_ref,
                     q_ref, k_ref, v_ref, z_ref, gb_ref, pbuf, *, d, nt, nct, tm, dk):
    j = pl.program_id(0)
    r = j % nt
    first = jnp.logical_or(r == 0, r == nct)
    last = jnp.logical_or(r == nct - 1, r == nt - 1)
    halo = SUBLANES
    xe = jnp.concatenate([xp_ref[...], x_ref[...], xn_ref[...]], axis=0)
    h = _norm_mod(xe, g_ref[...], mod_ref[0:1, :], mod_ref[1:2, :])
    rid = lax.broadcasted_iota(I32, (tm + 2 * halo, 1), 0)
    keep = jnp.logical_and(jnp.logical_or(rid >= halo, jnp.logical_not(first)),
                           jnp.logical_or(rid < tm + halo, jnp.logical_not(last)))
    h = jnp.where(keep, h, 0.0)
    pbuf[...] = jnp.dot(h.astype(BF16), w_ref[...], preferred_element_type=F32)
    half = GDN_CONV // 2

    def conv_block(c0):
        col = pbuf[:, pl.ds(c0, LANES)]
        n = col.shape[0]
        acc = None
        for t in range(GDN_CONV):
            sh = col if t == half else pltpu.roll(col, (half - t) % n, 0)
            term = sh[halo:halo + tm, :] * cw_ref[t:t + 1, pl.ds(c0, LANES)]
            acc = term if acc is None else acc + term
        return _silu(acc)

    for hh in range(3 * d // LANES):
        c0 = hh * LANES
        blk = conv_block(c0)
        if hh < 2 * d // LANES:
            blk = blk * lax.rsqrt(jnp.sum(blk * blk, axis=-1, keepdims=True) + EPS)
        if hh < d // LANES:
            q_ref[:, c0:c0 + LANES] = (blk * dk ** -0.5).astype(BF16)
        elif hh < 2 * d // LANES:
            k_ref[:, c0 - d:c0 - d + LANES] = blk.astype(BF16)
        else:
            v_ref[:, c0 - 2 * d:c0 - 2 * d + LANES] = blk.astype(BF16)
    z_ref[...] = pbuf[halo:halo + tm, 3 * d:4 * d].astype(BF16)
    ab = pbuf[halo:halo + tm, 4 * d:4 * d + LANES]
    lane = lax.broadcasted_iota(I32, ab.shape, 1)
    is_a = (lane % 16) < 8
    g = -jnp.exp(alog_ref[...]) * jax.nn.softplus(ab + dtb_ref[...])
    gb_ref[...] = jnp.where(is_a, g, jax.nn.sigmoid(ab))


TRI_BASE = 16


def _mm(a, b):
    return jnp.dot(a.astype(BF16), b.astype(BF16), preferred_element_type=F32)


def _tri_inverse_many(lms, ri, ci):
    n = lms[0].shape[0]

    def same(s):
        shift = int(math.log2(s))
        return (ri >> shift) == (ci >> shift)

    eye = jnp.where(ri == ci, 1.0, 0.0)
    base = same(TRI_BASE)
    ms = [jnp.where(base, lm, 0.0) for lm in lms]
    ps = [eye - m for m in ms]
    for _ in range(int(math.log2(TRI_BASE)) - 1):
        ms = [_mm(m, m) for m in ms]
        ps = [p + _mm(p, m) for p, m in zip(ps, ms)]
    s = TRI_BASE
    while s < n:
        band = jnp.logical_and(same(2 * s), jnp.logical_not(same(s)))
        ts = [_mm(p, jnp.where(band, lm, 0.0)) for p, lm in zip(ps, lms)]
        ps = [p - _mm(t, p) for p, t in zip(ps, ts)]
        s *= 2
    return ps


def _gdn_prep(probs, ri, ci):
    c = probs[0][0].shape[0]
    nt_dims = (((1,), (1,)), ((), ()))
    incl = {False: ri >= ci, True: ri <= ci}
    strict = {False: ri > ci, True: ri < ci}
    decays, kbs, rhss, qgs, kdecs, glasts = [], [], [], [], [], []
    for q, k, v, gc, gct, beta, upper in probs:
        decays.append(jnp.exp(jnp.where(incl[upper], gc - gct, NEG)))
        kf = k.astype(F32)
        kb = kf * beta
        eg = jnp.exp(gc)
        g_last = gc[0:1, :] if upper else gc[c - 1:c, :]
        kbs.append(kb.astype(BF16))
        rhss.append(jnp.concatenate([v.astype(F32) * beta, kb * eg], axis=1).astype(BF16))
        qgs.append((q.astype(F32) * eg).astype(BF16))
        kdecs.append((kf * jnp.exp(g_last - gc)).astype(BF16))
        glasts.append(g_last)
    kks = [lax.dot_general(kb, p[1], nt_dims, preferred_element_type=F32) for kb, p in zip(kbs, probs)]
    qks = [lax.dot_general(p[0], p[1], nt_dims, preferred_element_type=F32) for p in probs]
    lowers = [jnp.where(strict[p[6]], kk * dec, 0.0) for kk, dec, p in zip(kks, decays, probs)]
    intras = [(qk * dec).astype(BF16) for qk, dec in zip(qks, decays)]
    tinvs = _tri_inverse_many(lowers, ri, ci)
    uws = [jnp.dot(t.astype(BF16), r, preferred_element_type=F32) for t, r in zip(tinvs, rhss)]
    return list(zip(uws, qgs, intras, kdecs, glasts))


def _gdn_advance(preps, states, dv):
    tn_dims = (((0,), (0,)), ((), ()))
    sbs = [st.astype(BF16) for st in states]
    wss = [jnp.dot(p[0][:, dv:].astype(BF16), sb, preferred_element_type=F32) for p, sb in zip(preps, sbs)]
    o1s = [jnp.dot(p[1], sb, preferred_element_type=F32) for p, sb in zip(preps, sbs)]
    v_news = [(p[0][:, :dv] - ws).astype(BF16) for p, ws in zip(preps, wss)]
    o2s = [jnp.dot(p[2], vn, preferred_element_type=F32) for p, vn in zip(preps, v_news)]
    upds = [lax.dot_general(p[3], vn, tn_dims, preferred_element_type=F32) for p, vn in zip(preps, v_news)]
    outs = [o1 + o2 for o1, o2 in zip(o1s, o2s)]
    new_states = [st * jnp.exp(p[4]) + upd for st, p, upd in zip(states, preps, upds)]
    return outs, new_states


GDN_CHUNKS_PER_STEP = 2


def _gdn_scan_kernel(qf_ref, kf_ref, vf_ref, gf_ref, qb_ref, kb_ref, vb_ref, gbk_ref,
                     of_ref, ob_ref, sf, sb, *, dk):
    s = pl.program_id(1)

    @pl.when(s == 0)
    def _():
        sf[...] = jnp.zeros_like(sf)
        sb[...] = jnp.zeros_like(sb)

    c = GDN_CHUNK
    cps = qf_ref.shape[0] // c
    ri = lax.broadcasted_iota(I32, (c, c), 0)
    ci = lax.broadcasted_iota(I32, (c, c), 1)
    tri_l = (ri >= ci).astype(F32)
    tri_u = (ri <= ci).astype(F32)
    dirs = ((qf_ref, kf_ref, vf_ref, gf_ref, of_ref, sf), (qb_ref, kb_ref, vb_ref, gbk_ref, ob_ref, sb))
    probs, sinks = [], []
    for t in range(cps):
        for upper, (q_ref, k_ref, v_ref, g_ref, o_ref, st) in enumerate(dirs):
            j = cps - 1 - t if upper else t
            rows = slice(j * c, (j + 1) * c)
            gbv = g_ref[rows, :]
            csum = jnp.dot(tri_u if upper else tri_l, gbv, precision=HIGHEST, preferred_element_type=F32)
            csum_t = csum.T
            base = 16 * upper
            for hh in range(HEADS):
                sl = slice(hh * dk, (hh + 1) * dk)
                probs.append((q_ref[rows, sl], k_ref[rows, sl], v_ref[rows, sl],
                              csum[:, base + hh:base + hh + 1], csum_t[base + hh:base + hh + 1, :],
                              gbv[:, base + 8 + hh:base + 9 + hh], bool(upper)))
                sinks.append((o_ref, rows, sl))
    preps = _gdn_prep(probs, ri, ci)
    per = 2 * HEADS
    states = [st[hh] for (_, _, _, _, _, st) in dirs for hh in range(HEADS)]
    for t in range(cps):
        outs, states = _gdn_advance(preps[t * per:(t + 1) * per], states, dk)
        for (o_ref, rows, sl), o in zip(sinks[t * per:(t + 1) * per], outs):
            o_ref[rows, sl] = o.astype(BF16)
    for i, (_, _, _, _, _, st) in enumerate(dirs):
        for hh in range(HEADS):
            st[hh] = states[i * HEADS + hh]


def _gdn_mixer(xall, mod_i, norm_gain, w_in, conv_w, a_log, dt_bias, dims):
    bsz, ctx_len, seq, d, tm = dims
    ltot = ctx_len + seq
    nt, nct = ltot // tm, ctx_len // tm
    n_tiles = bsz * nt
    rows = n_tiles * tm
    dk = d // HEADS
    n_in = w_in.shape[1]
    n_pad = -(-n_in // LANES) * LANES
    wp = jnp.pad(w_in, ((0, 0), (0, n_pad - n_in))).astype(BF16)
    zeros8 = jnp.zeros((2, HEADS), F32)
    lay = lambda t: jnp.pad(jnp.concatenate([t, zeros8], axis=1).reshape(1, 4 * HEADS),
                            ((0, 0), (0, LANES - 4 * HEADS)))
    mrow = _mod_row_map(nt, nct, bsz)
    hb = tm // SUBLANES
    last_hblk = rows // SUBLANES - 1
    const = lambda j: (0, 0)
    q, k, v, z, gb = pl.pallas_call(
        functools.partial(_gdn_proj_kernel, d=d, nt=nt, nct=nct, tm=tm, dk=dk),
        grid=(n_tiles,),
        in_specs=[pl.BlockSpec((SUBLANES, d), lambda j: (jnp.maximum(j * hb - 1, 0), 0)),
                  pl.BlockSpec((tm, d), lambda j: (j, 0)),
                  pl.BlockSpec((SUBLANES, d), lambda j: (jnp.minimum((j + 1) * hb, last_hblk), 0)),
                  pl.BlockSpec((None, 6, d), lambda j: (mrow(j), 0, 0)),
                  pl.BlockSpec((1, d), const),
                  pl.BlockSpec((d, n_pad), const),
                  pl.BlockSpec((GDN_CONV, 3 * d), const),
                  pl.BlockSpec((1, LANES), const),
                  pl.BlockSpec((1, LANES), const)],
        out_specs=[pl.BlockSpec((tm, d), lambda j: (j, 0))] * 4 + [pl.BlockSpec((tm, LANES), lambda j: (j, 0))],
        out_shape=[jax.ShapeDtypeStruct((rows, d), BF16)] * 4 + [jax.ShapeDtypeStruct((rows, LANES), F32)],
        scratch_shapes=[pltpu.VMEM((tm + 2 * SUBLANES, n_pad), F32)],
        compiler_params=_cparams("parallel"),
        name="gdn_proj",
    )(xall, xall, xall, mod_i, norm_gain.reshape(1, d), wp, conv_w, lay(a_log), lay(dt_bias))
    c = GDN_CHUNK * GDN_CHUNKS_PER_STEP
    assert ctx_len % c == 0 and ltot % c == 0
    ncl, ncc = ltot // c, ctx_len // c

    def fwd(b, s):
        return (b * ncl + s, 0)

    def bwd(b, s):
        return (b * ncl + jnp.where(s < ncc, ncc - 1 - s, ncl + ncc - 1 - s), 0)

    blk = lambda m: pl.BlockSpec((c, d), m)
    gblk = lambda m: pl.BlockSpec((c, LANES), m)
    o_f, o_b = pl.pallas_call(
        functools.partial(_gdn_scan_kernel, dk=dk),
        grid=(bsz, ncl),
        in_specs=[blk(fwd), blk(fwd), blk(fwd), gblk(fwd), blk(bwd), blk(bwd), blk(bwd), gblk(bwd)],
        out_specs=[blk(fwd), blk(bwd)],
        out_shape=[jax.ShapeDtypeStruct((rows, d), BF16)] * 2,
        scratch_shapes=[pltpu.VMEM((HEADS, dk, dk), F32), pltpu.VMEM((HEADS, dk, dk), F32)],
        compiler_params=_cparams("parallel", "arbitrary"),
        name="gdn_scan",
    )(q, k, v, gb, q, k, v, gb)
    return (o_f, o_b, z)


def _split_bf16(x):
    hi = x.astype(BF16)
    lo = (x - hi.astype(F32)).astype(BF16)
    return hi, lo


def _post_kernel(*refs, kind, d, tm, dk):
    if kind == 0:
        of_ref, ob_ref, z_ref, og_ref = refs[:4]
        refs = refs[4:]
    else:
        o_ref = refs[0]
        refs = refs[1:]
    (x_ref, mod_ref, wo_ref, g_ref, wrh_ref, wrl_ref, br_ref,
     xo_ref, h_ref, ids_ref, cnt_ref, base) = refs
    j = pl.program_id(0)

    @pl.when(j == 0)
    def _():
        base[...] = jnp.zeros_like(base)

    if kind == 0:
        parts = []
        for hh in range(d // dk):
            sl = slice(hh * dk, (hh + 1) * dk)
            o = of_ref[:, sl].astype(F32) + ob_ref[:, sl].astype(F32)
            o = o * lax.rsqrt(jnp.mean(o * o, axis=-1, keepdims=True) + EPS) * og_ref[...]
            parts.append((o * _silu(z_ref[:, sl].astype(F32))).astype(BF16))
        o_in = jnp.concatenate(parts, axis=1)
    else:
        o_in = o_ref[...]
    mod = mod_ref[...]
    x = x_ref[...] + mod[2:3, :] * jnp.dot(o_in, wo_ref[...], preferred_element_type=F32)
    xo_ref[...] = x
    h = _norm_mod(x, g_ref[...], mod[3:4, :], mod[4:5, :])
    h_ref[:, :d] = h
    hi, lo = _split_bf16(h)
    logits = (jnp.dot(hi, wrh_ref[...], preferred_element_type=F32)
              + jnp.dot(lo, wrh_ref[...], preferred_element_type=F32)
              + jnp.dot(hi, wrl_ref[...], preferred_element_type=F32)) + br_ref[...]
    lane = lax.broadcasted_iota(I32, logits.shape, 1)
    big = jnp.int32(1 << 20)
    is_g = lane < MOE_GROUPS
    gl = jnp.where(is_g, logits, NEG)
    gmax = jnp.max(gl, axis=-1, keepdims=True)
    gsel = jnp.min(jnp.where(gl == gmax, lane, big), axis=-1, keepdims=True)
    p_group = 1.0 / jnp.sum(jnp.where(is_g, jnp.exp(gl - gmax), 0.0), axis=-1, keepdims=True)
    in_grp = jnp.logical_and(lane >= MOE_GROUPS + gsel * MOE_PER_GROUP,
                             lane < MOE_GROUPS + (gsel + 1) * MOE_PER_GROUP)
    el = jnp.where(in_grp, logits, NEG)
    v0 = jnp.max(el, axis=-1, keepdims=True)
    i0 = jnp.min(jnp.where(el == v0, lane, big), axis=-1, keepdims=True)
    el1 = jnp.where(lane == i0, NEG, el)
    v1 = jnp.max(el1, axis=-1, keepdims=True)
    i1 = jnp.min(jnp.where(el1 == v1, lane, big), axis=-1, keepdims=True)
    e1 = jnp.exp(v1 - v0)
    w0 = p_group / (1.0 + e1)
    w1 = p_group * e1 / (1.0 + e1)
    a0 = i0 - MOE_GROUPS - gsel * MOE_PER_GROUP
    a1 = i1 - MOE_GROUPS - gsel * MOE_PER_GROUP
    lo_e = jnp.minimum(a0, a1)
    hi_e = jnp.maximum(a0, a1)
    lof = lo_e.astype(F32)
    pair = (lof * MOE_PER_GROUP - lof * (lof + 1.0) * 0.5).astype(I32) + (hi_e - lo_e - 1)
    cls = gsel * MOE_PAIRS + pair
    first_lo = a0 < a1
    g_lo = jnp.where(first_lo, w0, w1)
    g_hi = jnp.where(first_lo, w1, w0)
    oh = lane == cls
    onehot = jnp.where(oh, 1.0, 0.0)
    ri = lax.broadcasted_iota(I32, (tm, tm), 0)
    ci = lax.broadcasted_iota(I32, (tm, tm), 1)
    tri = jnp.where(ri > ci, 1.0, 0.0).astype(BF16)
    before = base[...] + jnp.dot(tri, onehot.astype(BF16), preferred_element_type=F32)
    rank = jnp.sum(jnp.where(oh, before, 0.0), axis=-1, keepdims=True)
    new_base = base[...] + jnp.sum(onehot, axis=0, keepdims=True)
    base[...] = new_base
    cnt_ref[...] = new_base
    ids = jnp.where(lane == 0, cls, jnp.where(lane == 1, rank.astype(I32), 0))
    ids_ref[...] = ids.T[0:SUBLANES, :]
    h_ref[:, d:] = jnp.where(lane == 0, g_lo, jnp.where(lane == 1, g_hi, 0.0))


def _post_mixer(kind, mixer_out, xall, mod_i, w_out, ffn_gain, w_group, b_group, w_expert, b_expert,
                dims, o_gain=None):
    bsz, ctx_len, seq, d, tm = dims
    ltot = ctx_len + seq
    nt, nct = ltot // tm, ctx_len // tm
    n_tiles = bsz * nt
    rows = n_tiles * tm
    dk = d // HEADS
    mrow = _mod_row_map(nt, nct, bsz)
    wr = jnp.pad(jnp.concatenate([w_group, w_expert], axis=1),
                 ((0, 0), (0, LANES - MOE_GROUPS - MOE_EXPERTS)))
    wr_hi = wr.astype(BF16)
    wr_lo = (wr - wr_hi.astype(F32)).astype(BF16)
    br = jnp.pad(jnp.concatenate([b_group, b_expert]), (0, LANES - MOE_GROUPS - MOE_EXPERTS)).reshape(1, LANES)
    const = lambda j: (0, 0)
    row = lambda j: (j, 0)
    lead_specs = [pl.BlockSpec((tm, d), row)] * len(mixer_out)
    lead_args = list(mixer_out)
    if kind == 0:
        lead_specs.append(pl.BlockSpec((1, dk), const))
        lead_args.append(o_gain.reshape(1, dk))
    n_lead = len(lead_args)
    outs = pl.pallas_call(
        functools.partial(_post_kernel, kind=kind, d=d, tm=tm, dk=dk),
        grid=(n_tiles,),
        in_specs=lead_specs + [pl.BlockSpec((tm, d), row),
                               pl.BlockSpec((None, 6, d), lambda j: (mrow(j), 0, 0)),
                               pl.BlockSpec((w_out.shape[0], d), const),
                               pl.BlockSpec((1, d), const),
                               pl.BlockSpec((d, LANES), const),
                               pl.BlockSpec((d, LANES), const),
                               pl.BlockSpec((1, LANES), const)],
        out_specs=[pl.BlockSpec((tm, d), row), pl.BlockSpec((tm, d + LANES), row),
                   pl.BlockSpec((SUBLANES, tm), lambda j: (0, j)),
                   pl.BlockSpec((1, LANES), const)],
        out_shape=[jax.ShapeDtypeStruct((rows, d), F32), jax.ShapeDtypeStruct((rows, d + LANES), F32),
                   jax.ShapeDtypeStruct((SUBLANES, rows), I32),
                   jax.ShapeDtypeStruct((1, LANES), F32)],
        scratch_shapes=[pltpu.VMEM((1, LANES), F32)],
        input_output_aliases={n_lead: 0},
        compiler_params=_cparams("arbitrary"),
        name="post_mixer",
    )(*lead_args, xall, mod_i, w_out.astype(BF16), ffn_gain.reshape(1, d), wr_hi, wr_lo, br)
    return outs


def _dispatch_kernel(zlo_ref, zhi_ref, nu_ref, dest_ref, h_ref, xs_ref, zblk, hbuf, sem, zsem, *, tm, blk,
                     n_blocks, n_tiles):
    j = pl.program_id(0)
    slot = j % 2

    def tile_wait(s):
        pltpu.make_async_copy(hbuf.at[s], xs_ref.at[pl.ds(0, tm)], sem.at[s]).wait()

    def row_copy(src, dst_row, s):
        return pltpu.make_async_copy(src, xs_ref.at[pl.ds(dst_row, 1)], s)

    def blk_copy(bi):
        return pltpu.make_async_copy(zblk, xs_ref.at[pl.ds(pl.multiple_of(bi * blk, blk), blk)], zsem)

    @pl.when(j == 0)
    def _():
        zblk[...] = jnp.zeros_like(zblk)

        def per_class(e, carry):
            lo, hi = zlo_ref[e], zhi_ref[e]

            def start(r, c):
                row_copy(zblk.at[pl.ds(0, 1)], r, zsem).start()
                return c

            def wait(r, c):
                row_copy(zblk.at[pl.ds(0, 1)], r, zsem).wait()
                return c

            lax.fori_loop(lo, hi, start, 0)
            lax.fori_loop(lo, hi, wait, 0)
            return carry

        lax.fori_loop(0, MOE_CLASSES, per_class, 0)

        def tail_start(bi, c):
            blk_copy(bi).start()
            return c

        def tail_wait(bi, c):
            blk_copy(bi).wait()
            return c

        lax.fori_loop(nu_ref[0], n_blocks, tail_start, 0)
        lax.fori_loop(nu_ref[0], n_blocks, tail_wait, 0)

    @pl.when(j > 0)
    def _():
        tile_wait(1 - slot)

    hbuf[slot] = h_ref[...]

    def body(r, c):
        row_copy(hbuf.at[slot, pl.ds(r, 1)], dest_ref[0, r], sem.at[slot]).start()
        return c

    lax.fori_loop(0, tm, body, 0, unroll=8)

    @pl.when(j == n_tiles - 1)
    def _():
        tile_wait(slot)


def _expert_kernel(ea_ref, eb_ref, nu_ref, x_ref, w1a_ref, w3a_ref, w2a_ref, w1b_ref, w3b_ref, w2b_ref,
                   y_ref, *, d):
    j = pl.program_id(0)

    @pl.when(j < nu_ref[0])
    def _():
        x = x_ref[:, :d].astype(BF16)
        gates = x_ref[:, d:]

        def expert(w1_ref, w3_ref, w2_ref):
            a = jnp.dot(x, w1_ref[...], preferred_element_type=F32)
            b = jnp.dot(x, w3_ref[...], preferred_element_type=F32)
            return jnp.dot((_silu(a) * b).astype(BF16), w2_ref[...], preferred_element_type=F32)

        y_ref[...] = (expert(w1a_ref, w3a_ref, w2a_ref) * gates[:, 0:1]
                      + expert(w1b_ref, w3b_ref, w2b_ref) * gates[:, 1:2])

    @pl.when(j >= nu_ref[0])
    def _():
        y_ref[...] = jnp.zeros_like(y_ref)


def _combine_kernel(dest_ref, dnext_ref, x_ref, mod_ref, fg_ref, yb_ref, xo_ref, ybuf, sem, *, tm, final,
                    inner, n_steps):
    j = pl.program_id(0) if inner is None else pl.program_id(0) * inner + pl.program_id(1)
    slot = j % 2

    def gather(d_ref, s):
        def body(r, c):
            pltpu.make_async_copy(yb_ref.at[pl.ds(d_ref[0, r], 1)], ybuf.at[s, pl.ds(r, 1)], sem.at[s]).start()
            return c

        lax.fori_loop(0, tm, body, 0, unroll=8)

    @pl.when(j == 0)
    def _():
        gather(dest_ref, slot)

    @pl.when(j + 1 < n_steps)
    def _():
        gather(dnext_ref, 1 - slot)

    pltpu.make_async_copy(yb_ref.at[pl.ds(0, tm)], ybuf.at[slot], sem.at[slot]).wait()
    x = x_ref[...] + mod_ref[5:6, :] * ybuf[slot]
    if final:
        x = x * lax.rsqrt(jnp.mean(x * x, axis=-1, keepdims=True) + EPS) * fg_ref[...]
    xo_ref[...] = x


def _class_experts():
    lo, hi = [], []
    for g in range(MOE_GROUPS):
        for a in range(MOE_PER_GROUP):
            for b in range(a + 1, MOE_PER_GROUP):
                lo.append(g * MOE_PER_GROUP + a)
                hi.append(g * MOE_PER_GROUP + b)
    return jnp.asarray(lo, I32), jnp.asarray(hi, I32)


def _moe(xall, h, ids, counts, mod_i, w1, w3, w2, final_gain, dims, final):
    bsz, ctx_len, seq, d, tm = dims
    ltot = ctx_len + seq
    nt, nct = ltot // tm, ctx_len // tm
    n_tiles = bsz * nt
    rows = n_tiles * tm
    ncls = MOE_CLASSES
    blk = MOE_BLOCK
    dw = d + LANES
    n_blocks = -(-(rows + ncls * (blk - 1)) // blk)
    cnt = counts[0, :ncls].astype(I32)
    padded = (cnt + blk - 1) // blk * blk
    pad_end = jnp.cumsum(padded)
    pad_start = pad_end - padded
    cls_of_row = ids[0]
    start_of_row = jnp.sum(jnp.where(cls_of_row[:, None] == jnp.arange(ncls, dtype=I32)[None, :],
                                     pad_start[None, :], 0), axis=1)
    dest = (start_of_row + ids[1]).astype(I32).reshape(n_tiles, 1, tm)
    n_used = (pad_end[-1] // blk).astype(I32).reshape(1)
    blk_first = jnp.arange(n_blocks, dtype=I32) * blk
    blk_class = jnp.minimum(jnp.sum((pad_end[None, :] <= blk_first[:, None]).astype(I32), axis=1), ncls - 1)
    cls_lo, cls_hi = _class_experts()
    blk_lo, blk_hi = cls_lo[blk_class], cls_hi[blk_class]
    smem_dest = pl.BlockSpec((None, 1, tm), lambda j, *_: (j, 0, 0), memory_space=pltpu.SMEM)
    xs = pl.pallas_call(
        functools.partial(_dispatch_kernel, tm=tm, blk=blk, n_blocks=n_blocks, n_tiles=n_tiles),
        grid_spec=pltpu.PrefetchScalarGridSpec(
            num_scalar_prefetch=3, grid=(n_tiles,),
            in_specs=[smem_dest, pl.BlockSpec((tm, dw), lambda j, *_: (j, 0))],
            out_specs=pl.BlockSpec(memory_space=pl.ANY),
            scratch_shapes=[pltpu.VMEM((blk, dw), F32), pltpu.VMEM((2, tm, dw), F32),
                            pltpu.SemaphoreType.DMA((2,)), pltpu.SemaphoreType.DMA]),
        out_shape=jax.ShapeDtypeStruct((n_blocks * blk, dw), F32),
        compiler_params=_cparams("arbitrary"),
        name="moe_dispatch",
    )((pad_start + cnt).astype(I32), pad_end.astype(I32), n_used, dest, h)

    def xmap(j, ea, eb, nu):
        return (jnp.minimum(j, nu[0] - 1), 0)

    def wmap_lo(j, ea, eb, nu):
        return (ea[jnp.minimum(j, nu[0] - 1)], 0, 0)

    def wmap_hi(j, ea, eb, nu):
        return (eb[jnp.minimum(j, nu[0] - 1)], 0, 0)

    f = w1.shape[-1]
    w1b, w3b, w2b = w1.astype(BF16), w3.astype(BF16), w2.astype(BF16)
    yb = pl.pallas_call(
        functools.partial(_expert_kernel, d=d),
        grid_spec=pltpu.PrefetchScalarGridSpec(
            num_scalar_prefetch=3, grid=(n_blocks,),
            in_specs=[pl.BlockSpec((blk, dw), xmap),
                      pl.BlockSpec((None, d, f), wmap_lo),
                      pl.BlockSpec((None, d, f), wmap_lo),
                      pl.BlockSpec((None, f, d), wmap_lo),
                      pl.BlockSpec((None, d, f), wmap_hi),
                      pl.BlockSpec((None, d, f), wmap_hi),
                      pl.BlockSpec((None, f, d), wmap_hi)],
            out_specs=pl.BlockSpec((blk, d), lambda j, ea, eb, nu: (j, 0))),
        out_shape=jax.ShapeDtypeStruct((n_blocks * blk, d), F32),
        compiler_params=_cparams("arbitrary"),
        name="moe_experts",
    )(blk_lo, blk_hi, n_used, xs, w1b, w3b, w2b, w1b, w3b, w2b)

    if final:
        nlt = seq // tm
        grid, inner, n_steps = (bsz, nlt), nlt, bsz * nlt
        lin = lambda b, i: b * nlt + i
        tile_of = lambda l: (l // nlt) * nt + nct + l % nlt
        mod_map = lambda b, i: (b, 0, 0)
        out_map = lambda b, i: (b * nlt + i, 0)
        out_rows, aliases = bsz * seq, {}
    else:
        mrow = _mod_row_map(nt, nct, bsz)
        grid, inner, n_steps = (n_tiles,), None, n_tiles
        lin = lambda j: j
        tile_of = lambda l: l
        mod_map = lambda j: (mrow(j), 0, 0)
        out_map = lambda j: (j, 0)
        out_rows, aliases = rows, {2: 0}
    tile = lambda *g: tile_of(lin(*g))
    tile_next = lambda *g: tile_of(jnp.minimum(lin(*g) + 1, n_steps - 1))
    out = pl.pallas_call(
        functools.partial(_combine_kernel, tm=tm, final=final, inner=inner, n_steps=n_steps),
        grid=grid,
        in_specs=[pl.BlockSpec((None, 1, tm), lambda *g: (tile(*g), 0, 0), memory_space=pltpu.SMEM),
                  pl.BlockSpec((None, 1, tm), lambda *g: (tile_next(*g), 0, 0), memory_space=pltpu.SMEM),
                  pl.BlockSpec((tm, d), lambda *g: (tile(*g), 0)),
                  pl.BlockSpec((None, 6, d), mod_map),
                  pl.BlockSpec((1, d), lambda *g: (0, 0)),
                  pl.BlockSpec(memory_space=pl.ANY)],
        out_specs=pl.BlockSpec((tm, d), out_map),
        out_shape=jax.ShapeDtypeStruct((out_rows, d), F32),
        scratch_shapes=[pltpu.VMEM((2, tm, d), F32), pltpu.SemaphoreType.DMA((2,))],
        input_output_aliases=aliases,
        compiler_params=_cparams(*(("arbitrary",) * len(grid))),
        name="moe_combine",
    )(dest, dest, xall, mod_i, final_gain.reshape(1, d), yb)
    return out


def kernel(x, c, ctx, c_ctx, w_mod, b_mod, norm_mix, norm_ffn, gdn_w_in, gdn_conv, gdn_a_log, gdn_dt_bias, gdn_norm, gdn_w_out, diff_w_qkv, diff_lambda, diff_norm, diff_w_out, mla_w_down, mla_q_norm, mla_kv_norm, mla_w_uq, mla_w_ukv, mla_w_out, moe_w_group, moe_b_group, moe_w_expert, moe_b_expert, moe_w1, moe_w3, moe_w2, final_norm):
    bsz, seq, d = x.shape
    ctx_len = ctx.shape[1]
    depth = w_mod.shape[0]
    tm = _row_tile(ctx_len)
    assert d % LANES == 0 and d // HEADS == LANES
    assert ctx_len % tm == 0 and seq % tm == 0 and ctx_len % GDN_CHUNK == 0 and seq % GDN_CHUNK == 0
    dims = (bsz, ctx_len, seq, d, tm)
    ltot = ctx_len + seq
    xall = jnp.concatenate([ctx, x], axis=1).reshape(bsz * ltot, d)
    mod = _mod_vectors(c, c_ctx, w_mod, b_mod)
    tables = _rope_tables(seq, ctx_len)
    for i in range(depth):
        kind, j = i % N_MIXERS, i // N_MIXERS
        if kind == 0:
            mixer_out = _gdn_mixer(xall, mod[i], norm_mix[i], gdn_w_in[j], gdn_conv[j], gdn_a_log[j],
                                   gdn_dt_bias[j], dims)
            w_out, o_gain = gdn_w_out[j], gdn_norm[j]
        elif kind == 1:
            lam_init = 0.8 - 0.6 * math.exp(-0.3 * i)
            mixer_out = _diff_mixer(xall, mod[i], norm_mix[i], diff_w_qkv[j], diff_lambda[j], diff_norm[j],
                                    dims, tables, lam_init)
            w_out, o_gain = diff_w_out[j], None
        else:
            mixer_out = _mla_mixer(xall, mod[i], norm_mix[i], mla_w_down[j], mla_q_norm[j], mla_kv_norm[j],
                                   mla_w_uq[j], mla_w_ukv[j], dims, tables)
            w_out, o_gain = mla_w_out[j], None
        xall, h, ids, counts = _post_mixer(kind, mixer_out, xall, mod[i], w_out, norm_ffn[i],
                                           moe_w_group[i], moe_b_group[i], moe_w_expert[i],
                                           moe_b_expert[i], dims, o_gain)
        xall = _moe(xall, h, ids, counts, mod[i], moe_w1[i], moe_w3[i], moe_w2[i], final_norm, dims,
                    final=(i == depth - 1))
    return xall.reshape(bsz, seq, d)
```

```python
import functools
import math

import jax
import jax.numpy as jnp
from jax import lax
from jax.experimental import pallas as pl
from jax.experimental.pallas import tpu as pltpu

F32 = jnp.float32
BF16 = jnp.bfloat16
I32 = jnp.int32
HIGHEST = lax.Precision.HIGHEST

LANES = 128
SUBLANES = 8
VMEM_LIMIT = 56 * 1024 * 1024

EPS = 1e-6
GRID_W = 64
ROPE_THETA = 10000.0
N_MIXERS = 3
HEADS = 8
GDN_CONV = 5
GDN_CHUNK = 64
DIFF_SUBLN_EPS = 1e-5
MLA_NOPE = 128
MLA_ROPE = 64
MLA_V = 128
MOE_GROUPS = 4
MOE_PER_GROUP = 8
MOE_EXPERTS = MOE_GROUPS * MOE_PER_GROUP
MOE_TOP_K = 2
MOE_PAIRS = MOE_PER_GROUP * (MOE_PER_GROUP - 1) // 2
MOE_CLASSES = MOE_GROUPS * MOE_PAIRS
ROW_DMA_GROUP = 8
MOE_BLOCK = 256
NEG = -1e30


def _cparams(*sem):
    return pltpu.CompilerParams(dimension_semantics=sem, vmem_limit_bytes=VMEM_LIMIT)


def _row_tile(ctx_len):
    return 256 if ctx_len % 256 == 0 else 128


def _mod_row_map(nt, nct, bsz):
    def f(j):
        return jnp.where(j % nt < nct, bsz, j // nt)
    return f


def _norm_mod(x, gain, shift, scale, eps=EPS):
    var = jnp.mean(x * x, axis=-1, keepdims=True)
    y = x * lax.rsqrt(var + eps) * gain
    return y * (1.0 + scale) + shift


def _silu(x):
    return x * jax.nn.sigmoid(x)


def _mod_kernel(c_ref, w_ref, b_ref, o_ref):
    s = _silu(c_ref[...])
    o_ref[...] = jnp.dot(s, w_ref[...], precision=HIGHEST, preferred_element_type=F32) + b_ref[...]


def _mod_vectors(c, c_ctx, w_mod, b_mod):
    depth, d, n = w_mod.shape
    bsz = c.shape[0]
    rows = -(-(bsz + 1) // SUBLANES) * SUBLANES
    cc = jnp.zeros((rows, d), F32).at[:bsz].set(c).at[bsz].set(c_ctx)
    tn = 512
    out = pl.pallas_call(
        _mod_kernel,
        grid=(depth, n // tn),
        in_specs=[pl.BlockSpec((rows, d), lambda i, j: (0, 0)),
                  pl.BlockSpec((None, d, tn), lambda i, j: (i, 0, j)),
                  pl.BlockSpec((None, 1, tn), lambda i, j: (i, 0, j))],
        out_specs=pl.BlockSpec((None, rows, tn), lambda i, j: (i, 0, j)),
        out_shape=jax.ShapeDtypeStruct((depth, rows, n), F32),
        compiler_params=_cparams("parallel", "parallel"),
        name="mod_vectors",
    )(cc, w_mod, b_mod.reshape(depth, 1, n))
    return out.reshape(depth, rows, 6, d)


def _rope_tables(seq, ctx_len):
    quarter = 16
    inv_freq = ROPE_THETA ** (-jnp.arange(quarter, dtype=F32) / quarter)
    t = jnp.arange(seq)
    row = (t // GRID_W).astype(F32)[:, None] * inv_freq
    col = (t % GRID_W).astype(F32)[:, None] * inv_freq
    cos = jnp.concatenate([jnp.cos(row), jnp.cos(row), jnp.cos(col), jnp.cos(col)], axis=1)
    sin = jnp.concatenate([-jnp.sin(row), jnp.sin(row), -jnp.sin(col), jnp.sin(col)], axis=1)
    cos = jnp.concatenate([jnp.ones((ctx_len, 64), F32), cos], axis=0)
    sin = jnp.concatenate([jnp.zeros((ctx_len, 64), F32), sin], axis=0)
    return jnp.tile(cos, (1, 2)), jnp.tile(sin, (1, 2))


def _rope128(blk, cos, sin):
    lane = lax.broadcasted_iota(I32, blk.shape, 1)
    first = (lane % 32) < 16
    partner = jnp.where(first, pltpu.roll(blk, LANES - 16, 1), pltpu.roll(blk, 16, 1))
    return blk * cos + partner * sin


def _diff_proj_kernel(x_ref, mod_ref, g_ref, w_ref, cos_ref, sin_ref, o_ref, vt_ref, *, d, q_scale):
    h = _norm_mod(x_ref[...], g_ref[...], mod_ref[0:1, :], mod_ref[1:2, :])
    p = jnp.dot(h.astype(BF16), w_ref[...], preferred_element_type=F32)
    cos = cos_ref[...]
    sin = sin_ref[...]
    nqk = 2 * d // LANES
    for cb in range(nqk):
        r = _rope128(p[:, cb * LANES:(cb + 1) * LANES], cos, sin)
        if cb < nqk // 2:
            r = r * q_scale
        o_ref[:, cb * LANES:(cb + 1) * LANES] = r.astype(BF16)
    vt_ref[...] = p[:, 2 * d:].T.astype(BF16)


KEY_CHUNKS = (1408, 768, 512, 384, 256, 128)


def _pick_tk(n, cands=KEY_CHUNKS):
    for cand in cands:
        if n % cand == 0:
            return cand
    raise ValueError(n)


def _flash_t(streams, k_ref, vt_ref, m_ref, l_ref, acc_ref, nsteps, tk):
    for s in range(len(streams)):
        m_ref[s] = jnp.full(m_ref.shape[1:], NEG, F32)
        l_ref[s] = jnp.zeros(l_ref.shape[1:], F32)
        acc_ref[s] = jnp.zeros(acc_ref.shape[1:], F32)
    nt_dims = (((1,), (1,)), ((), ()))

    def scores(i):
        return [lax.dot_general(k_ref[i * tk:(i + 1) * tk, kc], q, nt_dims, preferred_element_type=F32)
                for q, kc, _ in streams]

    sts = scores(0)
    for i in range(nsteps):
        nxt = scores(i + 1) if i + 1 < nsteps else None
        alphas, ps = [], []
        for s, st in enumerate(sts):
            m_old = m_ref[s]
            m_new = jnp.maximum(m_old, jnp.max(st, axis=0, keepdims=True))
            alpha = jnp.exp2(m_old - m_new)
            p = jnp.exp2(st - m_new)
            l_ref[s] = alpha * l_ref[s] + jnp.sum(p, axis=0, keepdims=True)
            m_ref[s] = m_new
            alphas.append(alpha)
            ps.append(p.astype(BF16))
        pvs = [jnp.dot(vt_ref[vr, i * tk:(i + 1) * tk], p, preferred_element_type=F32)
               for (_, _, vr), p in zip(streams, ps)]
        for s in range(len(streams)):
            acc_ref[s] = alphas[s] * acc_ref[s] + pvs[s]
        sts = nxt


def _flash_ctx_or_all(streams, k_ref, vt_ref, m_ref, l_ref, acc_ref, nct, ctx_len, ltot, cands=KEY_CHUNKS):
    i = pl.program_id(2)
    tk_c, tk_l = _pick_tk(ctx_len, cands), _pick_tk(ltot, cands)

    @pl.when(i < nct)
    def _():
        _flash_t(streams, k_ref, vt_ref, m_ref, l_ref, acc_ref, ctx_len // tk_c, tk_c)

    @pl.when(i >= nct)
    def _():
        _flash_t(streams, k_ref, vt_ref, m_ref, l_ref, acc_ref, ltot // tk_l, tk_l)


ATTN_HEADS_PER_STEP = 4
MLA_HEADS_PER_STEP = 4


def _diff_attn_kernel(lam_ref, gain_ref, q_ref, k_ref, vt_ref, o_ref, m_ref, l_ref, acc_ref, *, nct,
                      ctx_len, ltot, lam_init, hw):
    streams = []
    for g in range(ATTN_HEADS_PER_STEP):
        cols = slice(g * hw, (g + 1) * hw)
        q = q_ref[:, cols]
        lane = lax.broadcasted_iota(I32, q.shape, 1)
        zero = jnp.zeros_like(q)
        streams.append((jnp.where(lane < hw // 2, q, zero), cols, cols))
        streams.append((jnp.where(lane >= hw // 2, q, zero), cols, cols))
    _flash_ctx_or_all(streams, k_ref, vt_ref, m_ref, l_ref, acc_ref, nct, ctx_len, ltot)
    lv = lam_ref[...]
    lam = (jnp.exp(jnp.sum(lv[0:1] * lv[1:2], keepdims=True))
           - jnp.exp(jnp.sum(lv[2:3] * lv[3:4], keepdims=True)) + lam_init)
    for g in range(ATTN_HEADS_PER_STEP):
        o = acc_ref[2 * g] / l_ref[2 * g] - lam * (acc_ref[2 * g + 1] / l_ref[2 * g + 1])
        var = jnp.mean(o * o, axis=0, keepdims=True)
        o = o * lax.rsqrt(var + DIFF_SUBLN_EPS) * gain_ref[...] * (1.0 - lam_init)
        o_ref[:, g * hw:(g + 1) * hw] = o.T.astype(BF16)


def _diff_mixer(xall, mod_i, norm_gain, w_qkv, lam_vec, sub_gain, dims, tables, lam_init):
    bsz, ctx_len, seq, d, tm = dims
    ltot = ctx_len + seq
    nt, nct = ltot // tm, ctx_len // tm
    n_tiles = bsz * nt
    dh = d // HEADS // 2
    cos, sin = tables
    mrow = _mod_row_map(nt, nct, bsz)
    rows = n_tiles * tm
    qk, vt = pl.pallas_call(
        functools.partial(_diff_proj_kernel, d=d, q_scale=dh ** -0.5 * math.log2(math.e)),
        grid=(n_tiles,),
        in_specs=[pl.BlockSpec((tm, d), lambda j: (j, 0)),
                  pl.BlockSpec((None, 6, d), lambda j: (mrow(j), 0, 0)),
                  pl.BlockSpec((1, d), lambda j: (0, 0)),
                  pl.BlockSpec((d, 3 * d), lambda j: (0, 0)),
                  pl.BlockSpec((tm, LANES), lambda j: (j % nt, 0)),
                  pl.BlockSpec((tm, LANES), lambda j: (j % nt, 0))],
        out_specs=[pl.BlockSpec((tm, 2 * d), lambda j: (j, 0)),
                   pl.BlockSpec((d, tm), lambda j: (0, j))],
        out_shape=[jax.ShapeDtypeStruct((rows, 2 * d), BF16), jax.ShapeDtypeStruct((d, rows), BF16)],
        compiler_params=_cparams("parallel"),
        name="diff_proj",
    )(xall, mod_i, norm_gain.reshape(1, d), w_qkv.astype(BF16), cos, sin)
    hw = 2 * dh
    hps = ATTN_HEADS_PER_STEP
    hg = HEADS // hps
    o = pl.pallas_call(
        functools.partial(_diff_attn_kernel, nct=nct, ctx_len=ctx_len, ltot=ltot, lam_init=lam_init, hw=hw),
        grid=(bsz, hg, nt),
        in_specs=[pl.BlockSpec((4, dh), lambda b, h, i: (0, 0)),
                  pl.BlockSpec((hw, 1), lambda b, h, i: (0, 0)),
                  pl.BlockSpec((tm, hps * hw), lambda b, h, i: (b * nt + i, h)),
                  pl.BlockSpec((ltot, hps * hw), lambda b, h, i: (b, hg + h), pipeline_mode=pl.Buffered(1)),
                  pl.BlockSpec((hps * hw, ltot), lambda b, h, i: (h, b), pipeline_mode=pl.Buffered(1))],
        out_specs=pl.BlockSpec((tm, hps * hw), lambda b, h, i: (b * nt + i, h)),
        out_shape=jax.ShapeDtypeStruct((rows, d), BF16),
        scratch_shapes=[pltpu.VMEM((2 * hps, 1, tm), F32), pltpu.VMEM((2 * hps, 1, tm), F32),
                        pltpu.VMEM((2 * hps, hw, tm), F32)],
        compiler_params=_cparams("parallel", "parallel", "arbitrary"),
        name="diff_attn",
    )(lam_vec, sub_gain.reshape(hw, 1), qk, qk, vt)
    return (o,)


def _mla_proj_kernel(x_ref, mod_ref, g_ref, wd_ref, qg_ref, kvg_ref, wq_ref, wkv_ref, cos_ref, sin_ref,
                     q_ref, k_ref, v_ref, *, q_lora, kv_lora, scale):
    h = _norm_mod(x_ref[...], g_ref[...], mod_ref[0:1, :], mod_ref[1:2, :])
    p = jnp.dot(h.astype(BF16), wd_ref[...], preferred_element_type=F32)
    cq = p[:, :q_lora]
    cq = cq * lax.rsqrt(jnp.mean(cq * cq, axis=-1, keepdims=True) + EPS) * qg_ref[...]
    ckv = p[:, q_lora:q_lora + kv_lora]
    ckv = ckv * lax.rsqrt(jnp.mean(ckv * ckv, axis=-1, keepdims=True) + EPS) * kvg_ref[...]
    cos = cos_ref[...]
    sin = sin_ref[...]
    kr = _rope128(p[:, q_lora + kv_lora:], cos, sin).astype(BF16)
    q = jnp.dot(cq.astype(BF16), wq_ref[...], preferred_element_type=F32)
    kv = jnp.dot(ckv.astype(BF16), wkv_ref[...], preferred_element_type=F32)
    hq = MLA_NOPE + LANES
    for hh in range(HEADS):
        q_ref[:, hh * hq:hh * hq + MLA_NOPE] = (q[:, hh * hq:hh * hq + MLA_NOPE] * scale).astype(BF16)
        qr = _rope128(q[:, hh * hq + MLA_NOPE:(hh + 1) * hq], cos, sin) * scale
        q_ref[:, hh * hq + MLA_NOPE:(hh + 1) * hq] = qr.astype(BF16)
        k_ref[:, hh * hq:hh * hq + MLA_NOPE] = kv[:, hh * MLA_NOPE:(hh + 1) * MLA_NOPE].astype(BF16)
        k_ref[:, hh * hq + MLA_NOPE:(hh + 1) * hq] = kr
    v_ref[...] = kv[:, HEADS * MLA_NOPE:].T.astype(BF16)


def _mla_attn_kernel(q_ref, k_ref, vt_ref, o_ref, m_ref, l_ref, acc_ref, *, nct, ctx_len, ltot, hq):
    streams = [(q_ref[:, g * hq:(g + 1) * hq], slice(g * hq, (g + 1) * hq), slice(g * MLA_V, (g + 1) * MLA_V))
               for g in range(MLA_HEADS_PER_STEP)]
    _flash_ctx_or_all(streams, k_ref, vt_ref, m_ref, l_ref, acc_ref, nct, ctx_len, ltot)
    for g in range(MLA_HEADS_PER_STEP):
        o_ref[:, g * MLA_V:(g + 1) * MLA_V] = (acc_ref[g] / l_ref[g]).T.astype(BF16)


def _mla_mixer(xall, mod_i, norm_gain, w_down, q_gain, kv_gain, w_uq, w_ukv, dims, tables):
    bsz, ctx_len, seq, d, tm = dims
    ltot = ctx_len + seq
    nt, nct = ltot // tm, ctx_len // tm
    n_tiles = bsz * nt
    rows = n_tiles * tm
    q_lora, kv_lora = q_gain.shape[0], kv_gain.shape[0]
    cos, sin = tables
    mrow = _mod_row_map(nt, nct, bsz)
    hq = MLA_NOPE + LANES
    wd = jnp.pad(w_down, ((0, 0), (0, LANES - MLA_ROPE))).astype(BF16)
    nd = wd.shape[1]
    wq = jnp.pad(w_uq.reshape(q_lora, HEADS, MLA_NOPE + MLA_ROPE),
                 ((0, 0), (0, 0), (0, LANES - MLA_ROPE))).reshape(q_lora, HEADS * hq).astype(BF16)
    wkv = w_ukv.reshape(kv_lora, HEADS, MLA_NOPE + MLA_V)
    wkv = jnp.concatenate([wkv[:, :, :MLA_NOPE].reshape(kv_lora, HEADS * MLA_NOPE),
                           wkv[:, :, MLA_NOPE:].reshape(kv_lora, HEADS * MLA_V)], axis=1).astype(BF16)
    scale = (MLA_NOPE + MLA_ROPE) ** -0.5 * math.log2(math.e)
    const = lambda j: (0, 0)
    q, k, v = pl.pallas_call(
        functools.partial(_mla_proj_kernel, q_lora=q_lora, kv_lora=kv_lora, scale=scale),
        grid=(n_tiles,),
        in_specs=[pl.BlockSpec((tm, d), lambda j: (j, 0)),
                  pl.BlockSpec((None, 6, d), lambda j: (mrow(j), 0, 0)),
                  pl.BlockSpec((1, d), const),
                  pl.BlockSpec((d, nd), const),
                  pl.BlockSpec((1, q_lora), const),
                  pl.BlockSpec((1, kv_lora), const),
                  pl.BlockSpec((q_lora, HEADS * hq), const),
                  pl.BlockSpec((kv_lora, HEADS * (MLA_NOPE + MLA_V)), const),
                  pl.BlockSpec((tm, LANES), lambda j: (j % nt, 0)),
                  pl.BlockSpec((tm, LANES), lambda j: (j % nt, 0))],
        out_specs=[pl.BlockSpec((tm, HEADS * hq), lambda j: (j, 0)),
                   pl.BlockSpec((tm, HEADS * hq), lambda j: (j, 0)),
                   pl.BlockSpec((HEADS * MLA_V, tm), lambda j: (0, j))],
        out_shape=[jax.ShapeDtypeStruct((rows, HEADS * hq), BF16),
                   jax.ShapeDtypeStruct((rows, HEADS * hq), BF16),
                   jax.ShapeDtypeStruct((HEADS * MLA_V, rows), BF16)],
        compiler_params=_cparams("parallel"),
        name="mla_proj",
    )(xall, mod_i, norm_gain.reshape(1, d), wd, q_gain.reshape(1, q_lora), kv_gain.reshape(1, kv_lora),
      wq, wkv, cos, sin)
    hps = MLA_HEADS_PER_STEP
    o = pl.pallas_call(
        functools.partial(_mla_attn_kernel, nct=nct, ctx_len=ctx_len, ltot=ltot, hq=hq),
        grid=(bsz, HEADS // hps, nt),
        in_specs=[pl.BlockSpec((tm, hps * hq), lambda b, h, i: (b * nt + i, h)),
                  pl.BlockSpec((ltot, hps * hq), lambda b, h, i: (b, h), pipeline_mode=pl.Buffered(1)),
                  pl.BlockSpec((hps * MLA_V, ltot), lambda b, h, i: (h, b), pipeline_mode=pl.Buffered(1))],
        out_specs=pl.BlockSpec((tm, hps * MLA_V), lambda b, h, i: (b * nt + i, h)),
        out_shape=jax.ShapeDtypeStruct((rows, HEADS * MLA_V), BF16),
        scratch_shapes=[pltpu.VMEM((hps, 1, tm), F32), pltpu.VMEM((hps, 1, tm), F32),
                        pltpu.VMEM((hps, MLA_V, tm), F32)],
        compiler_params=_cparams("parallel", "parallel", "arbitrary"),
        name="mla_attn",
    )(q, k, v)
    return (o,)


def _gdn_proj_kernel(xp_ref, x_ref, xn_ref, mod_ref, g_ref, w_ref, cw_ref, alog_ref, dtb_ref,
                     q_ref, k_ref, v_ref, z_ref, gb_ref, pbuf, *, d, nt, nct, tm, dk):
    j = pl.program_id(0)
    r = j % nt
    first = jnp.logical_or(r == 0, r == nct)
    last = jnp.logical_or(r == nct - 1, r == nt - 1)
    halo = SUBLANES
    xe = jnp.concatenate([xp_ref[...], x_ref[...], xn_ref[...]], axis=0)
    h = _norm_mod(xe, g_ref[...], mod_ref[0:1, :], mod_ref[1:2, :])
    rid = lax.broadcasted_iota(I32, (tm + 2 * halo, 1), 0)
    keep = jnp.logical_and(jnp.logical_or(rid >= halo, jnp.logical_not(first)),
                           jnp.logical_or(rid < tm + halo, jnp.logical_not(last)))
    h = jnp.where(keep, h, 0.0)
    pbuf[...] = jnp.dot(h.astype(BF16), w_ref[...], preferred_element_type=F32)
    half = GDN_CONV // 2

    def conv_block(c0):
        col = pbuf[:, pl.ds(c0, LANES)]
        n = col.shape[0]
        acc = None
        for t in range(GDN_CONV):
            sh = col if t == half else pltpu.roll(col, (half - t) % n, 0)
            term = sh[halo:halo + tm, :] * cw_ref[t:t + 1, pl.ds(c0, LANES)]
            acc = term if acc is None else acc + term
        return _silu(acc)

    for hh in range(3 * d // LANES):
        c0 = hh * LANES
        blk = conv_block(c0)
        if hh < 2 * d // LANES:
            blk = blk * lax.rsqrt(jnp.sum(blk * blk, axis=-1, keepdims=True) + EPS)
        if hh < d // LANES:
            q_ref[:, c0:c0 + LANES] = (blk * dk ** -0.5).astype(BF16)
        elif hh < 2 * d // LANES:
            k_ref[:, c0 - d:c0 - d + LANES] = blk.astype(BF16)
        else:
            v_ref[:, c0 - 2 * d:c0 - 2 * d + LANES] = blk.astype(BF16)
    z_ref[...] = pbuf[halo:halo + tm, 3 * d:4 * d].astype(BF16)
    ab = pbuf[halo:halo + tm, 4 * d:4 * d + LANES]
    lane = lax.broadcasted_iota(I32, ab.shape, 1)
    is_a = (lane % 16) < 8
    g = -jnp.exp(alog_ref[...]) * jax.nn.softplus(ab + dtb_ref[...])
    gb_ref[...] = jnp.where(is_a, g, jax.nn.sigmoid(ab))


TRI_BASE = 16


def _mm(a, b):
    return jnp.dot(a.astype(BF16), b.astype(BF16), preferred_element_type=F32)


def _tri_inverse_many(lms, ri, ci):
    n = lms[0].shape[0]

    def same(s):
        shift = int(math.log2(s))
        return (ri >> shift) == (ci >> shift)

    eye = jnp.where(ri == ci, 1.0, 0.0)
    base = same(TRI_BASE)
    ms = [jnp.where(base, lm, 0.0) for lm in lms]
    ps = [eye - m for m in ms]
    for _ in range(int(math.log2(TRI_BASE)) - 1):
        ms = [_mm(m, m) for m in ms]
        ps = [p + _mm(p, m) for p, m in zip(ps, ms)]
    s = TRI_BASE
    while s < n:
        band = jnp.logical_and(same(2 * s), jnp.logical_not(same(s)))
        ts = [_mm(p, jnp.where(band, lm, 0.0)) for p, lm in zip(ps, lms)]
        ps = [p - _mm(t, p) for p, t in zip(ps, ts)]
        s *= 2
    return ps


def _gdn_prep(probs, ri, ci):
    c = probs[0][0].shape[0]
    nt_dims = (((1,), (1,)), ((), ()))
    incl = {False: ri >= ci, True: ri <= ci}
    strict = {False: ri > ci, True: ri < ci}
    decays, kbs, rhss, qgs, kdecs, glasts = [], [], [], [], [], []
    for q, k, v, gc, gct, beta, upper in probs:
        decays.append(jnp.exp(jnp.where(incl[upper], gc - gct, NEG)))
        kf = k.astype(F32)
        kb = kf * beta
        eg = jnp.exp(gc)
        g_last = gc[0:1, :] if upper else gc[c - 1:c, :]
        kbs.append(kb.astype(BF16))
        rhss.append(jnp.concatenate([v.astype(F32) * beta, kb * eg], axis=1).astype(BF16))
        qgs.append((q.astype(F32) * eg).astype(BF16))
        kdecs.append((kf * jnp.exp(g_last - gc)).astype(BF16))
        glasts.append(g_last)
    kks = [lax.dot_general(kb, p[1], nt_dims, preferred_element_type=F32) for kb, p in zip(kbs, probs)]
    qks = [lax.dot_general(p[0], p[1], nt_dims, preferred_element_type=F32) for p in probs]
    lowers = [jnp.where(strict[p[6]], kk * dec, 0.0) for kk, dec, p in zip(kks, decays, probs)]
    intras = [(qk * dec).astype(BF16) for qk, dec in zip(qks, decays)]
    tinvs = _tri_inverse_many(lowers, ri, ci)
    uws = [jnp.dot(t.astype(BF16), r, preferred_element_type=F32) for t, r in zip(tinvs, rhss)]
    return list(zip(uws, qgs, intras, kdecs, glasts))


def _gdn_advance(preps, states, dv):
    tn_dims = (((0,), (0,)), ((), ()))
    sbs = [st.astype(BF16) for st in states]
    wss = [jnp.dot(p[0][:, dv:].astype(BF16), sb, preferred_element_type=F32) for p, sb in zip(preps, sbs)]
    o1s = [jnp.dot(p[1], sb, preferred_element_type=F32) for p, sb in zip(preps, sbs)]
    v_news = [(p[0][:, :dv] - ws).astype(BF16) for p, ws in zip(preps, wss)]
    o2s = [jnp.dot(p[2], vn, preferred_element_type=F32) for p, vn in zip(preps, v_news)]
    upds = [lax.dot_general(p[3], vn, tn_dims, preferred_element_type=F32) for p, vn in zip(preps, v_news)]
    outs = [o1 + o2 for o1, o2 in zip(o1s, o2s)]
    new_states = [st * jnp.exp(p[4]) + upd for st, p, upd in zip(states, preps, upds)]
    return outs, new_states


GDN_CHUNKS_PER_STEP = 2


def _gdn_scan_kernel(qf_ref, kf_ref, vf_ref, gf_ref, qb_ref, kb_ref, vb_ref, gbk_ref,
                     of_ref, ob_ref, sf, sb, *, dk):
    s = pl.program_id(1)

    @pl.when(s == 0)
    def _():
        sf[...] = jnp.zeros_like(sf)
        sb[...] = jnp.zeros_like(sb)

    c = GDN_CHUNK
    cps = qf_ref.shape[0] // c
    ri = lax.broadcasted_iota(I32, (c, c), 0)
    ci = lax.broadcasted_iota(I32, (c, c), 1)
    tri_l = (ri >= ci).astype(F32)
    tri_u = (ri <= ci).astype(F32)
    dirs = ((qf_ref, kf_ref, vf_ref, gf_ref, of_ref, sf), (qb_ref, kb_ref, vb_ref, gbk_ref, ob_ref, sb))
    probs, sinks = [], []
    for t in range(cps):
        for upper, (q_ref, k_ref, v_ref, g_ref, o_ref, st) in enumerate(dirs):
            j = cps - 1 - t if upper else t
            rows = slice(j * c, (j + 1) * c)
            gbv = g_ref[rows, :]
            csum = jnp.dot(tri_u if upper else tri_l, gbv, precision=HIGHEST, preferred_element_type=F32)
            csum_t = csum.T
            base = 16 * upper
            for hh in range(HEADS):
                sl = slice(hh * dk, (hh + 1) * dk)
                probs.append((q_ref[rows, sl], k_ref[rows, sl], v_ref[rows, sl],
                              csum[:, base + hh:base + hh + 1], csum_t[base + hh:base + hh + 1, :],
                              gbv[:, base + 8 + hh:base + 9 + hh], bool(upper)))
                sinks.append((o_ref, rows, sl))
    preps = _gdn_prep(probs, ri, ci)
    per = 2 * HEADS
    states = [st[hh] for (_, _, _, _, _, st) in dirs for hh in range(HEADS)]
    for t in range(cps):
        outs, states = _gdn_advance(preps[t * per:(t + 1) * per], states, dk)
        for (o_ref, rows, sl), o in zip(sinks[t * per:(t + 1) * per], outs):
            o_ref[rows, sl] = o.astype(BF16)
    for i, (_, _, _, _, _, st) in enumerate(dirs):
        for hh in range(HEADS):
            st[hh] = states[i * HEADS + hh]


def _gdn_mixer(xall, mod_i, norm_gain, w_in, conv_w, a_log, dt_bias, dims):
    bsz, ctx_len, seq, d, tm = dims
    ltot = ctx_len + seq
    nt, nct = ltot // tm, ctx_len // tm
    n_tiles = bsz * nt
    rows = n_tiles * tm
    dk = d // HEADS
    n_in = w_in.shape[1]
    n_pad = -(-n_in // LANES) * LANES
    wp = jnp.pad(w_in, ((0, 0), (0, n_pad - n_in))).astype(BF16)
    zeros8 = jnp.zeros((2, HEADS), F32)
    lay = lambda t: jnp.pad(jnp.concatenate([t, zeros8], axis=1).reshape(1, 4 * HEADS),
                            ((0, 0), (0, LANES - 4 * HEADS)))
    mrow = _mod_row_map(nt, nct, bsz)
    hb = tm // SUBLANES
    last_hblk = rows // SUBLANES - 1
    const = lambda j: (0, 0)
    q, k, v, z, gb = pl.pallas_call(
        functools.partial(_gdn_proj_kernel, d=d, nt=nt, nct=nct, tm=tm, dk=dk),
        grid=(n_tiles,),
        in_specs=[pl.BlockSpec((SUBLANES, d), lambda j: (jnp.maximum(j * hb - 1, 0), 0)),
                  pl.BlockSpec((tm, d), lambda j: (j, 0)),
                  pl.BlockSpec((SUBLANES, d), lambda j: (jnp.minimum((j + 1) * hb, last_hblk), 0)),
                  pl.BlockSpec((None, 6, d), lambda j: (mrow(j), 0, 0)),
                  pl.BlockSpec((1, d), const),
                  pl.BlockSpec((d, n_pad), const),
                  pl.BlockSpec((GDN_CONV, 3 * d), const),
                  pl.BlockSpec((1, LANES), const),
                  pl.BlockSpec((1, LANES), const)],
        out_specs=[pl.BlockSpec((tm, d), lambda j: (j, 0))] * 4 + [pl.BlockSpec((tm, LANES), lambda j: (j, 0))],
        out_shape=[jax.ShapeDtypeStruct((rows, d), BF16)] * 4 + [jax.ShapeDtypeStruct((rows, LANES), F32)],
        scratch_shapes=[pltpu.VMEM((tm + 2 * SUBLANES, n_pad), F32)],
        compiler_params=_cparams("parallel"),
        name="gdn_proj",
    )(xall, xall, xall, mod_i, norm_gain.reshape(1, d), wp, conv_w, lay(a_log), lay(dt_bias))
    c = GDN_CHUNK * GDN_CHUNKS_PER_STEP
    assert ctx_len % c == 0 and ltot % c == 0
    ncl, ncc = ltot // c, ctx_len // c

    def fwd(b, s):
        return (b * ncl + s, 0)

    def bwd(b, s):
        return (b * ncl + jnp.where(s < ncc, ncc - 1 - s, ncl + ncc - 1 - s), 0)

    blk = lambda m: pl.BlockSpec((c, d), m)
    gblk = lambda m: pl.BlockSpec((c, LANES), m)
    o_f, o_b = pl.pallas_call(
        functools.partial(_gdn_scan_kernel, dk=dk),
        grid=(bsz, ncl),
        in_specs=[blk(fwd), blk(fwd), blk(fwd), gblk(fwd), blk(bwd), blk(bwd), blk(bwd), gblk(bwd)],
        out_specs=[blk(fwd), blk(bwd)],
        out_shape=[jax.ShapeDtypeStruct((rows, d), BF16)] * 2,
        scratch_shapes=[pltpu.VMEM((HEADS, dk, dk), F32), pltpu.VMEM((HEADS, dk, dk), F32)],
        compiler_params=_cparams("parallel", "arbitrary"),
        name="gdn_scan",
    )(q, k, v, gb, q, k, v, gb)
    return (o_f, o_b, z)


def _split_bf16(x):
    hi = x.astype(BF16)
    lo = (x - hi.astype(F32)).astype(BF16)
    return hi, lo


def _post_kernel(*refs, kind, d, tm, dk):
    if kind == 0:
        of_ref, ob_ref, z_ref, og_ref = refs[:4]
        refs = refs[4:]
    else:
        o_ref = refs[0]
        refs = refs[1:]
    (x_ref, mod_ref, wo_ref, g_ref, wrh_ref, wrl_ref, br_ref,
     xo_ref, h_ref, ids_ref, cnt_ref, base) = refs
    j = pl.program_id(0)

    @pl.when(j == 0)
    def _():
        base[...] = jnp.zeros_like(base)

    if kind == 0:
        parts = []
        for hh in range(d // dk):
            sl = slice(hh * dk, (hh + 1) * dk)
            o = of_ref[:, sl].astype(F32) + ob_ref[:, sl].astype(F32)
            o = o * lax.rsqrt(jnp.mean(o * o, axis=-1, keepdims=True) + EPS) * og_ref[...]
            parts.append((o * _silu(z_ref[:, sl].astype(F32))).astype(BF16))
        o_in = jnp.concatenate(parts, axis=1)
    else:
        o_in = o_ref[...]
    mod = mod_ref[...]
    x = x_ref[...] + mod[2:3, :] * jnp.dot(o_in, wo_ref[...], preferred_element_type=F32)
    xo_ref[...] = x
    h = _norm_mod(x, g_ref[...], mod[3:4, :], mod[4:5, :])
    h_ref[:, :d] = h
    hi, lo = _split_bf16(h)
    logits = (jnp.dot(hi, wrh_ref[...], preferred_element_type=F32)
              + jnp.dot(lo, wrh_ref[...], preferred_element_type=F32)
              + jnp.dot(hi, wrl_ref[...], preferred_element_type=F32)) + br_ref[...]
    lane = lax.broadcasted_iota(I32, logits.shape, 1)
    big = jnp.int32(1 << 20)
    is_g = lane < MOE_GROUPS
    gl = jnp.where(is_g, logits, NEG)
    gmax = jnp.max(gl, axis=-1, keepdims=True)
    gsel = jnp.min(jnp.where(gl == gmax, lane, big), axis=-1, keepdims=True)
    p_group = 1.0 / jnp.sum(jnp.where(is_g, jnp.exp(gl - gmax), 0.0), axis=-1, keepdims=True)
    in_grp = jnp.logical_and(lane >= MOE_GROUPS + gsel * MOE_PER_GROUP,
                             lane < MOE_GROUPS + (gsel + 1) * MOE_PER_GROUP)
    el = jnp.where(in_grp, logits, NEG)
    v0 = jnp.max(el, axis=-1, keepdims=True)
    i0 = jnp.min(jnp.where(el == v0, lane, big), axis=-1, keepdims=True)
    el1 = jnp.where(lane == i0, NEG, el)
    v1 = jnp.max(el1, axis=-1, keepdims=True)
    i1 = jnp.min(jnp.where(el1 == v1, lane, big), axis=-1, keepdims=True)
    e1 = jnp.exp(v1 - v0)
    w0 = p_group / (1.0 + e1)
    w1 = p_group * e1 / (1.0 + e1)
    a0 = i0 - MOE_GROUPS - gsel * MOE_PER_GROUP
    a1 = i1 - MOE_GROUPS - gsel * MOE_PER_GROUP
    lo_e = jnp.minimum(a0, a1)
    hi_e = jnp.maximum(a0, a1)
    lof = lo_e.astype(F32)
    pair = (lof * MOE_PER_GROUP - lof * (lof + 1.0) * 0.5).astype(I32) + (hi_e - lo_e - 1)
    cls = gsel * MOE_PAIRS + pair
    first_lo = a0 < a1
    g_lo = jnp.where(first_lo, w0, w1)
    g_hi = jnp.where(first_lo, w1, w0)
    oh = lane == cls
    onehot = jnp.where(oh, 1.0, 0.0)
    ri = lax.broadcasted_iota(I32, (tm, tm), 0)
    ci = lax.broadcasted_iota(I32, (tm, tm), 1)
    tri = jnp.where(ri > ci, 1.0, 0.0).astype(BF16)
    before = base[...] + jnp.dot(tri, onehot.astype(BF16), preferred_element_type=F32)
    rank = jnp.sum(jnp.where(oh, before, 0.0), axis=-1, keepdims=True)
    new_base = base[...] + jnp.sum(onehot, axis=0, keepdims=True)
    base[...] = new_base
    cnt_ref[...] = new_base
    ids = jnp.where(lane == 0, cls, jnp.where(lane == 1, rank.astype(I32), 0))
    ids_ref[...] = ids.T[0:SUBLANES, :]
    h_ref[:, d:] = jnp.where(lane == 0, g_lo, jnp.where(lane == 1, g_hi, 0.0))


def _post_mixer(kind, mixer_out, xall, mod_i, w_out, ffn_gain, w_group, b_group, w_expert, b_expert,
                dims, o_gain=None):
    bsz, ctx_len, seq, d, tm = dims
    ltot = ctx_len + seq
    nt, nct = ltot // tm, ctx_len // tm
    n_tiles = bsz * nt
    rows = n_tiles * tm
    dk = d // HEADS
    mrow = _mod_row_map(nt, nct, bsz)
    wr = jnp.pad(jnp.concatenate([w_group, w_expert], axis=1),
                 ((0, 0), (0, LANES - MOE_GROUPS - MOE_EXPERTS)))
    wr_hi = wr.astype(BF16)
    wr_lo = (wr - wr_hi.astype(F32)).astype(BF16)
    br = jnp.pad(jnp.concatenate([b_group, b_expert]), (0, LANES - MOE_GROUPS - MOE_EXPERTS)).reshape(1, LANES)
    const = lambda j: (0, 0)
    row = lambda j: (j, 0)
    lead_specs = [pl.BlockSpec((tm, d), row)] * len(mixer_out)
    lead_args = list(mixer_out)
    if kind == 0:
        lead_specs.append(pl.BlockSpec((1, dk), const))
        lead_args.append(o_gain.reshape(1, dk))
    n_lead = len(lead_args)
    outs = pl.pallas_call(
        functools.partial(_post_kernel, kind=kind, d=d, tm=tm, dk=dk),
        grid=(n_tiles,),
        in_specs=lead_specs + [pl.BlockSpec((tm, d), row),
                               pl.BlockSpec((None, 6, d), lambda j: (mrow(j), 0, 0)),
                               pl.BlockSpec((w_out.shape[0], d), const),
                               pl.BlockSpec((1, d), const),
                               pl.BlockSpec((d, LANES), const),
                               pl.BlockSpec((d, LANES), const),
                               pl.BlockSpec((1, LANES), const)],
        out_specs=[pl.BlockSpec((tm, d), row), pl.BlockSpec((tm, d + LANES), row),
                   pl.BlockSpec((SUBLANES, tm), lambda j: (0, j)),
                   pl.BlockSpec((1, LANES), const)],
        out_shape=[jax.ShapeDtypeStruct((rows, d), F32), jax.ShapeDtypeStruct((rows, d + LANES), F32),
                   jax.ShapeDtypeStruct((SUBLANES, rows), I32),
                   jax.ShapeDtypeStruct((1, LANES), F32)],
        scratch_shapes=[pltpu.VMEM((1, LANES), F32)],
        input_output_aliases={n_lead: 0},
        compiler_params=_cparams("arbitrary"),
        name="post_mixer",
    )(*lead_args, xall, mod_i, w_out.astype(BF16), ffn_gain.reshape(1, d), wr_hi, wr_lo, br)
    return outs


def _dispatch_kernel(zlo_ref, zhi_ref, nu_ref, dest_ref, h_ref, xs_ref, zblk, hbuf, sem, zsem, *, tm, blk,
                     n_blocks, n_tiles):
    j = pl.program_id(0)
    slot = j % 2

    def tile_wait(s):
        pltpu.make_async_copy(hbuf.at[s], xs_ref.at[pl.ds(0, tm)], sem.at[s]).wait()

    def row_copy(src, dst_row, s):
        return pltpu.make_async_copy(src, xs_ref.at[pl.ds(dst_row, 1)], s)

    def blk_copy(bi):
        return pltpu.make_async_copy(zblk, xs_ref.at[pl.ds(pl.multiple_of(bi * blk, blk), blk)], zsem)

    @pl.when(j == 0)
    def _():
        zblk[...] = jnp.zeros_like(zblk)

        def per_class(e, carry):
            lo, hi = zlo_ref[e], zhi_ref[e]

            def start(r, c):
                row_copy(zblk.at[pl.ds(0, 1)], r, zsem).start()
                return c

            def wait(r, c):
                row_copy(zblk.at[pl.ds(0, 1)], r, zsem).wait()
                return c

            lax.fori_loop(lo, hi, start, 0)
            lax.fori_loop(lo, hi, wait, 0)
            return carry

        lax.fori_loop(0, MOE_CLASSES, per_class, 0)

        def tail_start(bi, c):
            blk_copy(bi).start()
            return c

        def tail_wait(bi, c):
            blk_copy(bi).wait()
            return c

        lax.fori_loop(nu_ref[0], n_blocks, tail_start, 0)
        lax.fori_loop(nu_ref[0], n_blocks, tail_wait, 0)

    @pl.when(j > 0)
    def _():
        tile_wait(1 - slot)

    hbuf[slot] = h_ref[...]

    def body(g, c):
        for k in range(ROW_DMA_GROUP):
            r = g * ROW_DMA_GROUP + k
            row_copy(hbuf.at[slot, pl.ds(r, 1)], dest_ref[0, r], sem.at[slot]).start(priority=k % 2)
        return c

    lax.fori_loop(0, tm // ROW_DMA_GROUP, body, 0)

    @pl.when(j == n_tiles - 1)
    def _():
        tile_wait(slot)


def _expert_kernel(ea_ref, eb_ref, nu_ref, x_ref, w1a_ref, w3a_ref, w2a_ref, w1b_ref, w3b_ref, w2b_ref,
                   y_ref, *, d):
    j = pl.program_id(0)

    @pl.when(j < nu_ref[0])
    def _():
        x = x_ref[:, :d].astype(BF16)
        gates = x_ref[:, d:]

        def expert(w1_ref, w3_ref, w2_ref):
            a = jnp.dot(x, w1_ref[...], preferred_element_type=F32)
            b = jnp.dot(x, w3_ref[...], preferred_element_type=F32)
            return jnp.dot((_silu(a) * b).astype(BF16), w2_ref[...], preferred_element_type=F32)

        y_ref[...] = (expert(w1a_ref, w3a_ref, w2a_ref) * gates[:, 0:1]
                      + expert(w1b_ref, w3b_ref, w2b_ref) * gates[:, 1:2])

    @pl.when(j >= nu_ref[0])
    def _():
        y_ref[...] = jnp.zeros_like(y_ref)


def _combine_kernel(dest_ref, dnext_ref, x_ref, mod_ref, fg_ref, yb_ref, xo_ref, ybuf, sem, *, tm, final,
                    inner, n_steps):
    j = pl.program_id(0) if inner is None else pl.program_id(0) * inner + pl.program_id(1)
    slot = j % 2

    def gather(d_ref, s):
        def body(g, c):
            for k in range(ROW_DMA_GROUP):
                r = g * ROW_DMA_GROUP + k
                pltpu.make_async_copy(yb_ref.at[pl.ds(d_ref[0, r], 1)], ybuf.at[s, pl.ds(r, 1)],
                                      sem.at[s]).start(priority=k % 2)
            return c

        lax.fori_loop(0, tm // ROW_DMA_GROUP, body, 0)

    @pl.when(j == 0)
    def _():
        gather(dest_ref, slot)

    @pl.when(j + 1 < n_steps)
    def _():
        gather(dnext_ref, 1 - slot)

    pltpu.make_async_copy(yb_ref.at[pl.ds(0, tm)], ybuf.at[slot], sem.at[slot]).wait()
    x = x_ref[...] + mod_ref[5:6, :] * ybuf[slot]
    if final:
        x = x * lax.rsqrt(jnp.mean(x * x, axis=-1, keepdims=True) + EPS) * fg_ref[...]
    xo_ref[...] = x


def _class_experts():
    lo, hi = [], []
    for g in range(MOE_GROUPS):
        for a in range(MOE_PER_GROUP):
            for b in range(a + 1, MOE_PER_GROUP):
                lo.append(g * MOE_PER_GROUP + a)
                hi.append(g * MOE_PER_GROUP + b)
    return jnp.asarray(lo, I32), jnp.asarray(hi, I32)


def _moe(xall, h, ids, counts, mod_i, w1, w3, w2, final_gain, dims, final):
    bsz, ctx_len, seq, d, tm = dims
    ltot = ctx_len + seq
    nt, nct = ltot // tm, ctx_len // tm
    n_tiles = bsz * nt
    rows = n_tiles * tm
    ncls = MOE_CLASSES
    blk = MOE_BLOCK
    dw = d + LANES
    n_blocks = -(-(rows + ncls * (blk - 1)) // blk)
    cnt = counts[0, :ncls].astype(I32)
    padded = (cnt + blk - 1) // blk * blk
    pad_end = jnp.cumsum(padded)
    pad_start = pad_end - padded
    cls_of_row = ids[0]
    start_of_row = jnp.sum(jnp.where(cls_of_row[:, None] == jnp.arange(ncls, dtype=I32)[None, :],
                                     pad_start[None, :], 0), axis=1)
    dest = (start_of_row + ids[1]).astype(I32).reshape(n_tiles, 1, tm)
    n_used = (pad_end[-1] // blk).astype(I32).reshape(1)
    blk_first = jnp.arange(n_blocks, dtype=I32) * blk
    blk_class = jnp.minimum(jnp.sum((pad_end[None, :] <= blk_first[:, None]).astype(I32), axis=1), ncls - 1)
    cls_lo, cls_hi = _class_experts()
    blk_lo, blk_hi = cls_lo[blk_class], cls_hi[blk_class]
    smem_dest = pl.BlockSpec((None, 1, tm), lambda j, *_: (j, 0, 0), memory_space=pltpu.SMEM)
    xs = pl.pallas_call(
        functools.partial(_dispatch_kernel, tm=tm, blk=blk, n_blocks=n_blocks, n_tiles=n_tiles),
        grid_spec=pltpu.PrefetchScalarGridSpec(
            num_scalar_prefetch=3, grid=(n_tiles,),
            in_specs=[smem_dest, pl.BlockSpec((tm, dw), lambda j, *_: (j, 0))],
            out_specs=pl.BlockSpec(memory_space=pl.ANY),
            scratch_shapes=[pltpu.VMEM((blk, dw), F32), pltpu.VMEM((2, tm, dw), F32),
                            pltpu.SemaphoreType.DMA((2,)), pltpu.SemaphoreType.DMA]),
        out_shape=jax.ShapeDtypeStruct((n_blocks * blk, dw), F32),
        compiler_params=_cparams("arbitrary"),
        name="moe_dispatch",
    )((pad_start + cnt).astype(I32), pad_end.astype(I32), n_used, dest, h)

    def xmap(j, ea, eb, nu):
        return (jnp.minimum(j, nu[0] - 1), 0)

    def wmap_lo(j, ea, eb, nu):
        return (ea[jnp.minimum(j, nu[0] - 1)], 0, 0)

    def wmap_hi(j, ea, eb, nu):
        return (eb[jnp.minimum(j, nu[0] - 1)], 0, 0)

    f = w1.shape[-1]
    w1b, w3b, w2b = w1.astype(BF16), w3.astype(BF16), w2.astype(BF16)
    yb = pl.pallas_call(
        functools.partial(_expert_kernel, d=d),
        grid_spec=pltpu.PrefetchScalarGridSpec(
            num_scalar_prefetch=3, grid=(n_blocks,),
            in_specs=[pl.BlockSpec((blk, dw), xmap),
                      pl.BlockSpec((None, d, f), wmap_lo),
                      pl.BlockSpec((None, d, f), wmap_lo),
                      pl.BlockSpec((None, f, d), wmap_lo),
                      pl.BlockSpec((None, d, f), wmap_hi),
                      pl.BlockSpec((None, d, f), wmap_hi),
                      pl.BlockSpec((None, f, d), wmap_hi)],
            out_specs=pl.BlockSpec((blk, d), lambda j, ea, eb, nu: (j, 0))),
        out_shape=jax.ShapeDtypeStruct((n_blocks * blk, d), F32),
        compiler_params=_cparams("arbitrary"),
        name="moe_experts",
    )(blk_lo, blk_hi, n_used, xs, w1b, w3b, w2b, w1b, w3b, w2b)

    if final:
        nlt = seq // tm
        grid, inner, n_steps = (bsz, nlt), nlt, bsz * nlt
        lin = lambda b, i: b * nlt + i
        tile_of = lambda l: (l // nlt) * nt + nct + l % nlt
        mod_map = lambda b, i: (b, 0, 0)
        out_map = lambda b, i: (b * nlt + i, 0)
        out_rows, aliases = bsz * seq, {}
    else:
        mrow = _mod_row_map(nt, nct, bsz)
        grid, inner, n_steps = (n_tiles,), None, n_tiles
        lin = lambda j: j
        tile_of = lambda l: l
        mod_map = lambda j: (mrow(j), 0, 0)
        out_map = lambda j: (j, 0)
        out_rows, aliases = rows, {2: 0}
    tile = lambda *g: tile_of(lin(*g))
    tile_next = lambda *g: tile_of(jnp.minimum(lin(*g) + 1, n_steps - 1))
    out = pl.pallas_call(
        functools.partial(_combine_kernel, tm=tm, final=final, inner=inner, n_steps=n_steps),
        grid=grid,
        in_specs=[pl.BlockSpec((None, 1, tm), lambda *g: (tile(*g), 0, 0), memory_space=pltpu.SMEM),
                  pl.BlockSpec((None, 1, tm), lambda *g: (tile_next(*g), 0, 0), memory_space=pltpu.SMEM),
                  pl.BlockSpec((tm, d), lambda *g: (tile(*g), 0)),
                  pl.BlockSpec((None, 6, d), mod_map),
                  pl.BlockSpec((1, d), lambda *g: (0, 0)),
                  pl.BlockSpec(memory_space=pl.ANY)],
        out_specs=pl.BlockSpec((tm, d), out_map),
        out_shape=jax.ShapeDtypeStruct((out_rows, d), F32),
        scratch_shapes=[pltpu.VMEM((2, tm, d), F32), pltpu.SemaphoreType.DMA((2,))],
        input_output_aliases=aliases,
        compiler_params=_cparams(*(("arbitrary",) * len(grid))),
        name="moe_combine",
    )(dest, dest, xall, mod_i, final_gain.reshape(1, d), yb)
    return out


def kernel(x, c, ctx, c_ctx, w_mod, b_mod, norm_mix, norm_ffn, gdn_w_in, gdn_conv, gdn_a_log, gdn_dt_bias, gdn_norm, gdn_w_out, diff_w_qkv, diff_lambda, diff_norm, diff_w_out, mla_w_down, mla_q_norm, mla_kv_norm, mla_w_uq, mla_w_ukv, mla_w_out, moe_w_group, moe_b_group, moe_w_expert, moe_b_expert, moe_w1, moe_w3, moe_w2, final_norm):
    bsz, seq, d = x.shape
    ctx_len = ctx.shape[1]
    depth = w_mod.shape[0]
    tm = _row_tile(ctx_len)
    assert d % LANES == 0 and d // HEADS == LANES
    assert ctx_len % tm == 0 and seq % tm == 0 and ctx_len % GDN_CHUNK == 0 and seq % GDN_CHUNK == 0
    dims = (bsz, ctx_len, seq, d, tm)
    ltot = ctx_len + seq
    xall = jnp.concatenate([ctx, x], axis=1).reshape(bsz * ltot, d)
    mod = _mod_vectors(c, c_ctx, w_mod, b_mod)
    tables = _rope_tables(seq, ctx_len)
    for i in range(depth):
        kind, j = i % N_MIXERS, i // N_MIXERS
        if kind == 0:
            mixer_out = _gdn_mixer(xall, mod[i], norm_mix[i], gdn_w_in[j], gdn_conv[j], gdn_a_log[j],
                                   gdn_dt_bias[j], dims)
            w_out, o_gain = gdn_w_out[j], gdn_norm[j]
        elif kind == 1:
            lam_init = 0.8 - 0.6 * math.exp(-0.3 * i)
            mixer_out = _diff_mixer(xall, mod[i], norm_mix[i], diff_w_qkv[j], diff_lambda[j], diff_norm[j],
                                    dims, tables, lam_init)
            w_out, o_gain = diff_w_out[j], None
        else:
            mixer_out = _mla_mixer(xall, mod[i], norm_mix[i], mla_w_down[j], mla_q_norm[j], mla_kv_norm[j],
                                   mla_w_uq[j], mla_w_ukv[j], dims, tables)
            w_out, o_gain = mla_w_out[j], None
        xall, h, ids, counts = _post_mixer(kind, mixer_out, xall, mod[i], w_out, norm_ffn[i],
                                           moe_w_group[i], moe_b_group[i], moe_w_expert[i],
                                           moe_b_expert[i], dims, o_gain)
        xall = _moe(xall, h, ids, counts, mod[i], moe_w1[i], moe_w3[i], moe_w2[i], final_norm, dims,
                    final=(i == depth - 1))
    return xall.reshape(bsz, seq, d)
```
